```python
import math, functools
import jax, jax.numpy as jnp
from jax import lax
import numpy as np

D_MODEL = 1024
BATCH = 8
SEQ = 2048
DEPTH = 1
DEC_BATCH = 128
DEC_SEQ = 4
PAST_LEN = 8192
PAGE_SIZE = 128

N_META = 16
POOL_WIDTH = D_MODEL // 2
POOL_WINDOWS = (2, 4, 8, 16)
POOL_GROUPS = len(POOL_WINDOWS)
POOL_GROUP_DIM = POOL_WIDTH // POOL_GROUPS
POOL_STATE = max(POOL_WINDOWS) - 1
HEAD_DIM = 64
N_HEADS = (D_MODEL // 2) // HEAD_DIM
N_KV_HEADS = 2
GROUP = N_HEADS // N_KV_HEADS
WINDOW = 128
ATTN_BLOCK = 128
ATTN_SCALE = HEAD_DIM ** -0.5
NUM_BUCKETS = 32
MAX_DISTANCE = 128
Q_DIM = N_HEADS * HEAD_DIM
KV_DIM = N_KV_HEADS * HEAD_DIM
OFF_Q = POOL_WIDTH
OFF_K = OFF_Q + Q_DIM
OFF_V = OFF_K + KV_DIM
OFF_GA = OFF_V + KV_DIM
OFF_GB = OFF_GA + D_MODEL
IN_DIM = OFF_GB + D_MODEL
PEER_HEADS = 8
N_KEYS = 128
N_EXPERTS = N_KEYS * N_KEYS
PEER_TOPK = 16
PEER_QDIM = 256
PEER_HALF = PEER_QDIM // 2
PEER_BLOCK = 128
ALPHA = (2 * DEPTH) ** 0.25
BETA = (8 * DEPTH) ** -0.25
LN_EPS = 1e-5

kernel_name = 'hybrid_pool_swa_peer_step'


def layer_norm(x, g, b):
    xf = x.astype(jnp.float32)
    mu = jnp.mean(xf, axis=-1, keepdims=True)
    var = jnp.mean(jnp.square(xf - mu), axis=-1, keepdims=True)
    return ((xf - mu) * lax.rsqrt(var + LN_EPS) * g + b).astype(x.dtype)


def rel_bucket(dist):
    n = jnp.maximum(dist, 0)
    max_exact = NUM_BUCKETS // 2
    large = max_exact + (jnp.log(jnp.maximum(n, max_exact).astype(jnp.float32) / max_exact)
                         / math.log(MAX_DISTANCE / max_exact)
                         * (NUM_BUCKETS - max_exact)).astype(jnp.int32)
    large = jnp.minimum(large, NUM_BUCKETS - 1)
    return jnp.where(n < max_exact, n, large)


def rel_bias(table, dist):
    b = table[rel_bucket(dist)].astype(jnp.float32)
    b = jnp.moveaxis(b, -1, 0)
    return b.reshape(N_KV_HEADS, GROUP, *dist.shape)


def sink_softmax(s, sink):
    sk = sink.astype(jnp.float32).reshape(N_KV_HEADS, GROUP, 1, 1)
    m = jnp.maximum(jnp.max(s, axis=-1, keepdims=True), sk)
    e = jnp.exp(s - m)
    return e / (jnp.sum(e, axis=-1, keepdims=True) + jnp.exp(sk - m))


def pool_mixer(p_new, p_prev, pos0, w_pool_mix, pool_scale):
    B, S, _ = p_new.shape
    ext = jnp.concatenate([p_prev, p_new], axis=1)
    cs = jnp.cumsum(ext.astype(jnp.float32), axis=1)
    cs = jnp.concatenate([jnp.zeros_like(cs[:, :1]), cs], axis=1)
    end = cs[:, POOL_STATE + 1:]
    pos = pos0 + jnp.arange(S)
    outs = []
    for g, w in enumerate(POOL_WINDOWS):
        sl = slice(g * POOL_GROUP_DIM, (g + 1) * POOL_GROUP_DIM)
        start = cs[:, POOL_STATE + 1 - w:POOL_STATE + 1 - w + S, sl]
        cnt = jnp.minimum(w, pos + 1).astype(jnp.float32)[None, :, None]
        outs.append((end[..., sl] - start) / cnt)
    pooled = jnp.stack(outs, axis=2) - p_new.reshape(B, S, POOL_GROUPS, POOL_GROUP_DIM).astype(jnp.float32)
    mixed = jnp.einsum('bsgc,gcd->bsgd', pooled.astype(p_new.dtype), w_pool_mix).reshape(B, S, POOL_WIDTH)
    return mixed * pool_scale, ext[:, -POOL_STATE:]


def swa_prompt(q, k, v, sink, rel_table):
    B, T = q.shape[:2]
    pad = (-T) % ATTN_BLOCK
    Tp = T + pad
    nb = Tp // ATTN_BLOCK
    padf = lambda a: jnp.pad(a, ((0, 0), (pad, 0), (0, 0), (0, 0)))
    qb = padf(q).reshape(B, nb, ATTN_BLOCK, N_KV_HEADS, GROUP, HEAD_DIM)
    kb = padf(k).reshape(B, nb, ATTN_BLOCK, N_KV_HEADS, HEAD_DIM)
    vb = padf(v).reshape(B, nb, ATTN_BLOCK, N_KV_HEADS, HEAD_DIM)
    shift = lambda a: jnp.concatenate([jnp.zeros_like(a[:, :1]), a[:, :-1]], axis=1)
    kk = jnp.concatenate([shift(kb), kb], axis=2)
    vv = jnp.concatenate([shift(vb), vb], axis=2)
    qi = jnp.arange(ATTN_BLOCK)[:, None]
    kj = jnp.arange(2 * ATTN_BLOCK)[None, :]
    dist = ATTN_BLOCK + qi - kj
    kpos = (jnp.arange(nb)[:, None] - 1) * ATTN_BLOCK + jnp.arange(2 * ATTN_BLOCK)[None, :] - pad
    mask = ((dist >= 0) & (dist < WINDOW))[None] & (kpos >= 0)[:, None, :]
    bias = rel_bias(rel_table, dist)
    s = jnp.einsum('bnqkgd,bnskd->bnkgqs', qb, kk).astype(jnp.float32) * ATTN_SCALE + bias
    s = jnp.where(mask[None, :, None, None], s, -jnp.inf)
    p = sink_softmax(s, sink)
    o = jnp.einsum('bnkgqs,bnskd->bnqkgd', p.astype(vv.dtype), vv).reshape(B, Tp, Q_DIM)[:, pad:]
    w = min(WINDOW, T)
    return o, k[:, -w:], v[:, -w:]


def swa_sample(q, k, v, ck, cv, sink, rel_table, pos0):
    B, S = q.shape[:2]
    W = ck.shape[1]
    kk = jnp.concatenate([ck, k], axis=1)
    vv = jnp.concatenate([cv, v], axis=1)
    qpos = pos0 + jnp.arange(S)
    kpos = jnp.concatenate([pos0 - W + jnp.arange(W), qpos])
    dist = qpos[:, None] - kpos[None, :]
    mask = (dist >= 0) & (dist < WINDOW)
    bias = rel_bias(rel_table, dist)
    qg = q.reshape(B, S, N_KV_HEADS, GROUP, HEAD_DIM)
    s = jnp.einsum('bqkgd,bskd->bkgqs', qg, kk).astype(jnp.float32) * ATTN_SCALE + bias
    s = jnp.where(mask, s, -jnp.inf)
    p = sink_softmax(s, sink)
    o = jnp.einsum('bkgqs,bskd->bqkgd', p.astype(vv.dtype), vv).reshape(B, S, Q_DIM)
    return o, kk[:, -W:], vv[:, -W:]


def peer(x, w_query, sub_keys, expert_u, expert_v):
    B, T, D = x.shape
    n = B * T
    pad = (-n) % PEER_BLOCK
    xt = jnp.pad(x.reshape(n, D), ((0, pad), (0, 0))).reshape(-1, PEER_BLOCK, D)

    def block(xb):
        q = (xb @ w_query).reshape(PEER_BLOCK, PEER_HEADS, 2, PEER_HALF)
        sc = jnp.einsum('thcd,hcnd->thcn', q, sub_keys).astype(jnp.float32)
        top_s, top_i = lax.top_k(sc, PEER_TOPK)
        cand = top_s[:, :, 0, :, None] + top_s[:, :, 1, None, :]
        cand_idx = top_i[:, :, 0, :, None] * N_KEYS + top_i[:, :, 1, None, :]
        best_s, best_j = lax.top_k(cand.reshape(PEER_BLOCK, PEER_HEADS, -1), PEER_TOPK)
        idx = jnp.take_along_axis(cand_idx.reshape(PEER_BLOCK, PEER_HEADS, -1), best_j, axis=-1)
        gate = jax.nn.softmax(best_s, axis=-1)
        u = jnp.take(expert_u, idx, axis=0)
        h = jax.nn.gelu(jnp.einsum('td,thkd->thk', xb, u).astype(jnp.float32), approximate=False)
        wgt = (gate * h).astype(xb.dtype)
        return jnp.einsum('thk,thkd->td', wgt, jnp.take(expert_v, idx, axis=0))

    y = lax.map(block, xt).reshape(-1, D)[:n]
    return y.reshape(B, T, D)


def layer_step(x, pool_prev, pos0, attn_fn, w_in, w_pool_mix, pool_scale, w_up_pool, w_up_attn,
               w_out, ln1_g, ln1_b, peer_w_query, peer_sub_keys, peer_u, peer_v, ln2_g, ln2_b):
    B, T, _ = x.shape
    z = x @ w_in
    p = z[..., :OFF_Q]
    q = z[..., OFF_Q:OFF_K].reshape(B, T, N_HEADS, HEAD_DIM)
    k = z[..., OFF_K:OFF_V].reshape(B, T, N_KV_HEADS, HEAD_DIM)
    v = z[..., OFF_V:OFF_GA].reshape(B, T, N_KV_HEADS, HEAD_DIM)
    g_pool = jax.nn.sigmoid(z[..., OFF_GA:OFF_GB])
    g_attn = jax.nn.sigmoid(z[..., OFF_GB:])
    pool_out, pool_state = pool_mixer(p, pool_prev, pos0, w_pool_mix, pool_scale)
    attn_out, k_state, v_state = attn_fn(q, k, v)
    m = g_pool * (pool_out @ w_up_pool) + g_attn * (attn_out @ w_up_attn)
    x = layer_norm(ALPHA * x + m @ w_out, ln1_g, ln1_b)
    x = layer_norm(ALPHA * x + peer(x, peer_w_query, peer_sub_keys, peer_u, peer_v), ln2_g, ln2_b)
    return x, k_state, v_state, pool_state


def setup_inputs(seed: int = 0) -> dict:
    key = jax.random.key(seed)
    ks = jax.random.split(key, 24)
    nrm = lambda k, shape, scale: jax.random.normal(k, shape, jnp.float32) * scale
    w_cache = min(WINDOW, PAST_LEN)
    return {
        'x_prompt': nrm(ks[0], (BATCH, SEQ, D_MODEL), 1.0),
        'x_sample': nrm(ks[1], (DEC_BATCH, DEC_SEQ, D_MODEL), 1.0),
        'cache_k': nrm(ks[2], (DEPTH, DEC_BATCH, w_cache, N_KV_HEADS, HEAD_DIM), 1.0),
        'cache_v': nrm(ks[3], (DEPTH, DEC_BATCH, w_cache, N_KV_HEADS, HEAD_DIM), 1.0),
        'state_pool': nrm(ks[4], (DEPTH, DEC_BATCH, POOL_STATE, POOL_WIDTH), 1.0),
        'meta_tokens': nrm(ks[5], (N_META, D_MODEL), 1.0),
        'rel_bias_table': nrm(ks[6], (NUM_BUCKETS, N_HEADS), 0.5),
        'w_in': nrm(ks[7], (DEPTH, D_MODEL, IN_DIM), D_MODEL ** -0.5),
        'w_pool_mix': nrm(ks[8], (DEPTH, POOL_GROUPS, POOL_GROUP_DIM, POOL_GROUP_DIM), POOL_GROUP_DIM ** -0.5),
        'pool_scale': 1.0 + nrm(ks[9], (DEPTH, POOL_WIDTH), 0.1),
        'attn_sinks': nrm(ks[10], (DEPTH, N_HEADS), 1.0),
        'w_up_pool': nrm(ks[11], (DEPTH, POOL_WIDTH, D_MODEL), POOL_WIDTH ** -0.5),
        'w_up_attn': nrm(ks[12], (DEPTH, Q_DIM, D_MODEL), Q_DIM ** -0.5),
        'w_out': nrm(ks[13], (DEPTH, D_MODEL, D_MODEL), BETA * D_MODEL ** -0.5),
        'ln1_g': 1.0 + nrm(ks[14], (DEPTH, D_MODEL), 0.02),
        'ln1_b': nrm(ks[15], (DEPTH, D_MODEL), 0.02),
        'peer_w_query': nrm(ks[16], (DEPTH, D_MODEL, PEER_HEADS * PEER_QDIM), D_MODEL ** -0.5),
        'peer_sub_keys': nrm(ks[17], (DEPTH, PEER_HEADS, 2, N_KEYS, PEER_HALF), PEER_HALF ** -0.5),
        'peer_u': nrm(ks[18], (DEPTH, N_EXPERTS, D_MODEL), D_MODEL ** -0.5),
        'peer_v': nrm(ks[19], (DEPTH, N_EXPERTS, D_MODEL), BETA),
        'ln2_g': 1.0 + nrm(ks[20], (DEPTH, D_MODEL), 0.02),
        'ln2_b': nrm(ks[21], (DEPTH, D_MODEL), 0.02),
    }


def reference(x_prompt, x_sample, cache_k, cache_v, state_pool, meta_tokens, rel_bias_table, w_in,
              w_pool_mix, pool_scale, attn_sinks, w_up_pool, w_up_attn, w_out, ln1_g, ln1_b,
              peer_w_query, peer_sub_keys, peer_u, peer_v, ln2_g, ln2_b):
    B = x_prompt.shape[0]
    meta = jnp.broadcast_to(meta_tokens.astype(x_prompt.dtype)[None], (B, N_META, D_MODEL))
    xp = jnp.concatenate([meta, x_prompt], axis=1)
    xs = x_sample
    kps, vps, sps, kss, vss, sss = [], [], [], [], [], []
    for l in range(DEPTH):
        lw = (w_in[l], w_pool_mix[l], pool_scale[l], w_up_pool[l], w_up_attn[l], w_out[l],
              ln1_g[l], ln1_b[l], peer_w_query[l], peer_sub_keys[l], peer_u[l], peer_v[l],
              ln2_g[l], ln2_b[l])
        attn_p = functools.partial(swa_prompt, sink=attn_sinks[l], rel_table=rel_bias_table)
        pool_zero = jnp.zeros((B, POOL_STATE, POOL_WIDTH), xp.dtype)
        xp, kp, vp, sp = layer_step(xp, pool_zero, 0, attn_p, *lw)
        attn_s = functools.partial(swa_sample, ck=cache_k[l], cv=cache_v[l], sink=attn_sinks[l],
                                   rel_table=rel_bias_table, pos0=PAST_LEN)
        xs, ksm, vsm, ssm = layer_step(xs, state_pool[l], PAST_LEN, attn_s, *lw)
        kps.append(kp); vps.append(vp); sps.append(sp)
        kss.append(ksm); vss.append(vsm); sss.append(ssm)
    return (xp[:, N_META:], xs, jnp.stack(kps), jnp.stack(vps), jnp.stack(sps),
            jnp.stack(kss), jnp.stack(vss), jnp.stack(sss))
```

```python
import functools
import math

import jax
import jax.numpy as jnp
import numpy as np
from jax import lax
from jax.experimental import pallas as pl
from jax.experimental.pallas import tpu as pltpu

F32 = jnp.float32
BF16 = jnp.bfloat16

D_MODEL = 1024
N_META = 16
POOL_WIDTH = 512
POOL_WINDOWS = (2, 4, 8, 16)
POOL_GROUP_DIM = 128
POOL_STATE = 15
HEAD_DIM = 64
N_HEADS = 8
N_KV_HEADS = 2
GROUP = N_HEADS // N_KV_HEADS
WINDOW = 128
ATTN_BLOCK = 128
ATTN_SCALE = HEAD_DIM ** -0.5
NUM_BUCKETS = 32
MAX_DISTANCE = 128
Q_DIM = N_HEADS * HEAD_DIM
KV_DIM = N_KV_HEADS * HEAD_DIM
OFF_Q = POOL_WIDTH
OFF_K = OFF_Q + Q_DIM
OFF_V = OFF_K + KV_DIM
OFF_GA = OFF_V + KV_DIM
PEER_HEADS = 8
N_KEYS = 128
PEER_TOPK = 16
PEER_HALF = 128
DEPTH = 1
ALPHA = (2 * DEPTH) ** 0.25
LN_EPS = 1e-5
NEG = -1e30

LANES = 128
SUBLANES = 8
VMEM_LIMIT = 56 * 1024 * 1024

TOK_TILE = 512
ATTN_TILE = 512
SAMPLE_SEQS = 8
EXPERT_CHUNK = 1024


def _cparams(sem):
    return pltpu.CompilerParams(dimension_semantics=sem, vmem_limit_bytes=VMEM_LIMIT)


def _inproj_kernel(x_ref, w_ref, p_ref, q_ref, k_ref, v_ref):
    z = jnp.dot(x_ref[...].astype(BF16), w_ref[...], preferred_element_type=F32)
    p_ref[...] = z[:, :OFF_Q]
    q_ref[...] = (z[:, OFF_Q:OFF_K] * ATTN_SCALE).astype(BF16)
    k_ref[...] = z[:, OFF_K:OFF_V]
    v_ref[...] = z[:, OFF_V:OFF_GA]


def _inproj(x, w_pqkv):
    rows = x.shape[0]
    tm = min(TOK_TILE, rows)
    assert rows % tm == 0
    row = lambda c: pl.BlockSpec((tm, c), lambda i: (i, 0))
    return pl.pallas_call(
        _inproj_kernel,
        grid=(rows // tm,),
        in_specs=[row(D_MODEL), pl.BlockSpec((D_MODEL, OFF_GA), lambda i: (0, 0))],
        out_specs=[row(POOL_WIDTH), row(Q_DIM), row(KV_DIM), row(KV_DIM)],
        out_shape=[jax.ShapeDtypeStruct((rows, POOL_WIDTH), F32),
                   jax.ShapeDtypeStruct((rows, Q_DIM), BF16),
                   jax.ShapeDtypeStruct((rows, KV_DIM), F32),
                   jax.ShapeDtypeStruct((rows, KV_DIM), F32)],
        compiler_params=_cparams(("parallel",)),
        name="inproj",
    )(x, w_pqkv)


def _window_pool(ext_ref, first, rows, out_ref):
    for g, w in enumerate(POOL_WINDOWS):
        cols = slice(g * POOL_GROUP_DIM, (g + 1) * POOL_GROUP_DIM)
        cur = ext_ref[first:first + rows, cols]
        acc = cur
        for r in range(1, w):
            acc = acc + ext_ref[first - r:first - r + rows, cols]
        out_ref[:, cols] = (acc * (1.0 / w) - cur).astype(out_ref.dtype)


def _sink_softmax(s, sink):
    m = jnp.maximum(jnp.max(s, axis=-1, keepdims=True), sink)
    e = jnp.exp(s - m)
    denom = jnp.sum(e, axis=-1, keepdims=True) + jnp.exp(sink - m)
    return e * (1.0 / denom)


_NT = (((1,), (1,)), ((), ()))


def _prompt_mix_kernel(sink_ref, q_ref, k_ref, v_ref, p_ref, kprev_ref, vprev_ref, pprev_ref,
                       mk_ref, mv_ref, mp_ref, bias_ref, pooled_ref, attn_ref,
                       kbuf, vbuf, pbuf):
    first = pl.program_id(1) == 0
    hist = ATTN_BLOCK
    kbuf[0:hist, :] = jnp.where(first, mk_ref[...], kprev_ref[...]).astype(BF16)
    vbuf[0:hist, :] = jnp.where(first, mv_ref[...], vprev_ref[...]).astype(BF16)
    kbuf[hist:, :] = k_ref[...].astype(BF16)
    vbuf[hist:, :] = v_ref[...].astype(BF16)
    pbuf[0:N_META, :] = jnp.where(first, mp_ref[...], pprev_ref[...])
    pbuf[N_META:, :] = p_ref[...]

    _window_pool(pbuf, N_META, ATTN_TILE, pooled_ref)

    first_i = jnp.where(first, 1, 0)
    for j in range(ATTN_TILE // ATTN_BLOCK):
        rows = slice(j * ATTN_BLOCK, (j + 1) * ATTN_BLOCK)
        keys = slice(j * ATTN_BLOCK, j * ATTN_BLOCK + 2 * ATTN_BLOCK)
        for kh in range(N_KV_HEADS):
            kk = kbuf[keys, kh * HEAD_DIM:(kh + 1) * HEAD_DIM]
            vv = vbuf[keys, kh * HEAD_DIM:(kh + 1) * HEAD_DIM]
            for g in range(GROUP):
                h = kh * GROUP + g
                hc = slice(h * HEAD_DIM, (h + 1) * HEAD_DIM)
                s = lax.dot_general(q_ref[rows, hc], kk, _NT, preferred_element_type=F32)
                bias = bias_ref[first_i, h] if j == 0 else bias_ref[0, h]
                prob = _sink_softmax(s + bias, sink_ref[h])
                o = jnp.dot(prob.astype(BF16), vv, preferred_element_type=F32)
                attn_ref[rows, hc] = o.astype(attn_ref.dtype)


def _prompt_mix(sinks, q, k, v, p, mk, mv, mp, bias, batch, seq):
    nt = seq // ATTN_TILE
    blocks_per_tile = ATTN_TILE // ATTN_BLOCK
    cur = lambda c: pl.BlockSpec((ATTN_TILE, c), lambda b, s: (b * nt + s, 0))
    prev_blk = lambda b, s: (jnp.maximum((b * nt + s) * blocks_per_tile - 1, 0), 0)
    prev_p = lambda b, s: (jnp.maximum((b * nt + s) * (ATTN_TILE // N_META) - 1, 0), 0)
    const2 = lambda b, s: (0, 0)
    rows = batch * seq
    return pl.pallas_call(
        _prompt_mix_kernel,
        grid=(batch, nt),
        in_specs=[pl.BlockSpec(memory_space=pltpu.SMEM),
                  cur(Q_DIM), cur(KV_DIM), cur(KV_DIM), cur(POOL_WIDTH),
                  pl.BlockSpec((ATTN_BLOCK, KV_DIM), prev_blk),
                  pl.BlockSpec((ATTN_BLOCK, KV_DIM), prev_blk),
                  pl.BlockSpec((N_META, POOL_WIDTH), prev_p),
                  pl.BlockSpec((ATTN_BLOCK, KV_DIM), const2),
                  pl.BlockSpec((ATTN_BLOCK, KV_DIM), const2),
                  pl.BlockSpec((N_META, POOL_WIDTH), const2),
                  pl.BlockSpec((2, N_HEADS, ATTN_BLOCK, 2 * ATTN_BLOCK), lambda b, s: (0, 0, 0, 0))],
        out_specs=[cur(POOL_WIDTH), cur(Q_DIM)],
        out_shape=[jax.ShapeDtypeStruct((rows, POOL_WIDTH), BF16),
                   jax.ShapeDtypeStruct((rows, Q_DIM), BF16)],
        scratch_shapes=[pltpu.VMEM((ATTN_BLOCK + ATTN_TILE, KV_DIM), BF16),
                        pltpu.VMEM((ATTN_BLOCK + ATTN_TILE, KV_DIM), BF16),
                        pltpu.VMEM((N_META + ATTN_TILE, POOL_WIDTH), F32)],
        compiler_params=_cparams(("parallel", "parallel")),
        name="prompt_mix",
    )(sinks, q, k, v, p, k, v, p, mk, mv, mp, bias)


def _sample_mix_kernel(sink_ref, q_ref, k_ref, v_ref, p_ref, ck_ref, cv_ref, st_ref,
                       biasc_ref, biasn_ref,
                       pooled_ref, attn_ref, newk_ref, newv_ref, newp_ref, ext, pooled_buf):
    nseq, w_cache, s_new = SAMPLE_SEQS, WINDOW, q_ref.shape[0] // SAMPLE_SEQS
    keep = POOL_STATE - s_new
    for i in range(nseq):
        new = slice(i * s_new, (i + 1) * s_new)
        ext[1:1 + POOL_STATE, :] = st_ref[i]
        ext[1 + POOL_STATE:1 + POOL_STATE + s_new, :] = p_ref[new, :]
        _window_pool(ext, 1 + POOL_STATE, s_new, pooled_buf)
        pooled_ref[new, :] = pooled_buf[...].astype(pooled_ref.dtype)
        newp_ref[i, 0:keep, :] = st_ref[i, s_new:POOL_STATE, :]
        newp_ref[i, keep:POOL_STATE, :] = p_ref[new, :]
        newk_ref[i, 0:w_cache - s_new, :] = ck_ref[i, s_new:w_cache, :]
        newk_ref[i, w_cache - s_new:w_cache, :] = k_ref[new, :]
        newv_ref[i, 0:w_cache - s_new, :] = cv_ref[i, s_new:w_cache, :]
        newv_ref[i, w_cache - s_new:w_cache, :] = v_ref[new, :]

    kc = ck_ref[...].reshape(nseq * w_cache, KV_DIM).astype(BF16)
    vc = cv_ref[...].reshape(nseq * w_cache, KV_DIM).astype(BF16)
    kn = k_ref[...].astype(BF16)
    vn = v_ref[...].astype(BF16)
    q = q_ref[...]
    nq = q.shape[0]
    for kh in range(N_KV_HEADS):
        kvc = slice(kh * HEAD_DIM, (kh + 1) * HEAD_DIM)
        qg = jnp.concatenate([q[:, (kh * GROUP + g) * HEAD_DIM:(kh * GROUP + g + 1) * HEAD_DIM]
                              for g in range(GROUP)], axis=0)
        sc = lax.dot_general(qg, kc[:, kvc], _NT, preferred_element_type=F32) + biasc_ref[kh]
        sn = lax.dot_general(qg, kn[:, kvc], _NT, preferred_element_type=F32) + biasn_ref[kh]
        sink = jnp.concatenate([jnp.full((nq, 1), sink_ref[kh * GROUP + g], F32) for g in range(GROUP)], axis=0)
        m = jnp.maximum(jnp.maximum(jnp.max(sc, axis=-1, keepdims=True),
                                    jnp.max(sn, axis=-1, keepdims=True)), sink)
        ec = jnp.exp(sc - m)
        en = jnp.exp(sn - m)
        denom = (jnp.sum(ec, axis=-1, keepdims=True) + jnp.sum(en, axis=-1, keepdims=True)
                 + jnp.exp(sink - m))
        inv = 1.0 / denom
        o = (jnp.dot((ec * inv).astype(BF16), vc[:, kvc], preferred_element_type=F32)
             + jnp.dot((en * inv).astype(BF16), vn[:, kvc], preferred_element_type=F32))
        for g in range(GROUP):
            h = kh * GROUP + g
            attn_ref[:, h * HEAD_DIM:(h + 1) * HEAD_DIM] = o[g * nq:(g + 1) * nq].astype(attn_ref.dtype)


def _sample_mix(sinks, q, k, v, p, cache_k, cache_v, state, biasc, biasn, s_new):
    nb = cache_k.shape[0]
    rows = SAMPLE_SEQS * s_new
    tok = lambda c: pl.BlockSpec((rows, c), lambda i: (i, 0))
    seq3 = lambda r, c: pl.BlockSpec((SAMPLE_SEQS, r, c), lambda i: (i, 0, 0))
    const3 = lambda a: pl.BlockSpec(a.shape, lambda i: (0, 0, 0))
    return pl.pallas_call(
        _sample_mix_kernel,
        grid=(nb // SAMPLE_SEQS,),
        in_specs=[pl.BlockSpec(memory_space=pltpu.SMEM),
                  tok(Q_DIM), tok(KV_DIM), tok(KV_DIM), tok(POOL_WIDTH),
                  seq3(WINDOW, KV_DIM), seq3(WINDOW, KV_DIM), seq3(POOL_STATE, POOL_WIDTH),
                  const3(biasc), const3(biasn)],
        out_specs=[tok(POOL_WIDTH), tok(Q_DIM),
                   seq3(WINDOW, KV_DIM), seq3(WINDOW, KV_DIM), seq3(POOL_STATE, POOL_WIDTH)],
        out_shape=[jax.ShapeDtypeStruct((nb * s_new, POOL_WIDTH), F32),
                   jax.ShapeDtypeStruct((nb * s_new, Q_DIM), F32),
                   jax.ShapeDtypeStruct((nb, WINDOW, KV_DIM), F32),
                   jax.ShapeDtypeStruct((nb, WINDOW, KV_DIM), F32),
                   jax.ShapeDtypeStruct((nb, POOL_STATE, POOL_WIDTH), F32)],
        scratch_shapes=[pltpu.VMEM((1 + POOL_STATE + SUBLANES, POOL_WIDTH), F32),
                        pltpu.VMEM((s_new, POOL_WIDTH), F32)],
        compiler_params=_cparams(("parallel",)),
        name="sample_mix",
    )(sinks, q, k, v, p, cache_k, cache_v, state, biasc, biasn)


def _layer_norm(x, g, b):
    mu = jnp.mean(x, axis=-1, keepdims=True)
    xc = x - mu
    var = jnp.mean(xc * xc, axis=-1, keepdims=True)
    return xc * lax.rsqrt(var + LN_EPS) * g + b


def _merge_kernel(x_ref, pooled_ref, attn_ref, wg_ref, wmix_ref, scale_ref, wup_ref, wua_ref,
                  wout_ref, g_ref, b_ref, y_ref):
    x = x_ref[...]
    glog = jnp.dot(x.astype(BF16), wg_ref[...], preferred_element_type=F32)
    g_pool = jax.nn.sigmoid(glog[:, :D_MODEL])
    g_attn = jax.nn.sigmoid(glog[:, D_MODEL:])
    pooled = pooled_ref[...].astype(BF16)
    mixed = jnp.concatenate(
        [jnp.dot(pooled[:, g * POOL_GROUP_DIM:(g + 1) * POOL_GROUP_DIM], wmix_ref[g],
                 preferred_element_type=F32) for g in range(len(POOL_WINDOWS))], axis=1)
    pool_out = (mixed * scale_ref[...]).astype(BF16)
    a = jnp.dot(pool_out, wup_ref[...], preferred_element_type=F32)
    b = jnp.dot(attn_ref[...].astype(BF16), wua_ref[...], preferred_element_type=F32)
    m = (g_pool * a + g_attn * b).astype(BF16)
    r = jnp.dot(m, wout_ref[...], preferred_element_type=F32)
    y_ref[...] = _layer_norm(ALPHA * x + r, g_ref[...], b_ref[...])


def _merge(x, pooled, attn, wg, wmix, scale, wup, wua, wout, ln_g, ln_b):
    rows = x.shape[0]
    tm = min(TOK_TILE, rows)
    assert rows % tm == 0
    row = lambda c: pl.BlockSpec((tm, c), lambda i: (i, 0))
    full = lambda a: pl.BlockSpec(a.shape, lambda i: (0,) * a.ndim)
    weights = (wg, wmix, scale, wup, wua, wout, ln_g, ln_b)
    return pl.pallas_call(
        _merge_kernel,
        grid=(rows // tm,),
        in_specs=[row(D_MODEL), row(POOL_WIDTH), row(Q_DIM)] + [full(a) for a in weights],
        out_specs=row(D_MODEL),
        out_shape=jax.ShapeDtypeStruct((rows, D_MODEL), F32),
        compiler_params=_cparams(("parallel",)),
        name="merge",
    )(x, pooled, attn, *weights)


def _oddeven_merge_sort_pairs(n):
    pairs = []
    p = 1
    while p < n:
        k = p
        while k >= 1:
            for j in range(k % p, n - k, 2 * k):
                for i in range(min(k, n - j - k)):
                    if (i + j) // (2 * p) == (i + j + k) // (2 * p):
                        pairs.append((i + j, i + j + k))
            k //= 2
        p *= 2
    return pairs


def _bitonic_merge_pairs(n):
    pairs = []
    k = n // 2
    while k >= 1:
        pairs += [(i, i + k) for i in range(n) if not i & k]
        k //= 2
    return pairs


def _apply_network(vals, pairs):
    vals = list(vals)
    for i, j in pairs:
        a, b = vals[i], vals[j]
        if b is None:
            continue
        if a is None:
            vals[i], vals[j] = b, None
        else:
            vals[i], vals[j] = jnp.maximum(a, b), jnp.minimum(a, b)
    return vals


_SORT16 = _oddeven_merge_sort_pairs(PEER_TOPK)
_MERGE16 = _bitonic_merge_pairs(PEER_TOPK)
_SORT64 = _oddeven_merge_sort_pairs(64)
_CANDIDATES = [(a, b) for a in range(PEER_TOPK) for b in range(PEER_TOPK) if (a + 1) * (b + 1) <= PEER_TOPK]


def _top16_rows(sc):
    n = sc.shape[0] // SUBLANES
    assert n == PEER_TOPK
    x = _apply_network([sc[k * SUBLANES:(k + 1) * SUBLANES, :] for k in range(n)], _SORT16)
    for shift in (4, 2, 1):
        y = [jnp.maximum(x[k], pltpu.roll(x[n - 1 - k], shift, 0)) for k in range(n)]
        x = _apply_network(y, _MERGE16)
    return x


def _peer_scores_kernel(x_ref, wq_ref, keys_ref, st_ref, stats_ref):
    q = jnp.dot(x_ref[...].astype(BF16), wq_ref[...], preferred_element_type=F32).astype(BF16)
    for h in range(PEER_HEADS):
        tops = []
        for c in range(2):
            col = (h * 2 + c) * PEER_HALF
            sc = lax.dot_general(keys_ref[h * 2 + c], q[:, col:col + PEER_HALF], _NT,
                                 preferred_element_type=F32)
            st_ref[h * 2 + c] = sc
            tops.append(_top16_rows(sc))
        cand = [tops[0][a] + tops[1][b] for a, b in _CANDIDATES]
        best = _apply_network(cand + [None] * (64 - len(cand)), _SORT64)[:PEER_TOPK]
        z = jnp.ones_like(best[0])
        for r in range(1, PEER_TOPK):
            z = z + jnp.exp(best[r] - best[0])
        row = lax.broadcasted_iota(jnp.int32, z.shape, 0)
        stats_ref[h] = jnp.where(row == 0, best[PEER_TOPK - 1],
                                 jnp.where(row == 1, tops[0][0],
                                           jnp.where(row == 2, tops[1][0], 1.0 / z)))


def _peer_scores(x, wq, keys):
    rows = x.shape[0]
    tm = min(TOK_TILE, rows)
    assert rows % tm == 0
    return pl.pallas_call(
        _peer_scores_kernel,
        grid=(rows // tm,),
        in_specs=[pl.BlockSpec((tm, D_MODEL), lambda i: (i, 0)),
                  pl.BlockSpec(wq.shape, lambda i: (0, 0)),
                  pl.BlockSpec(keys.shape, lambda i: (0, 0, 0))],
        out_specs=[pl.BlockSpec((2 * PEER_HEADS, N_KEYS, tm), lambda i: (0, 0, i)),
                   pl.BlockSpec((PEER_HEADS, SUBLANES, tm), lambda i: (0, 0, i))],
        out_shape=[jax.ShapeDtypeStruct((2 * PEER_HEADS, N_KEYS, rows), F32),
                   jax.ShapeDtypeStruct((PEER_HEADS, SUBLANES, rows), F32)],
        compiler_params=_cparams(("parallel",)),
        name="peer_scores",
    )(x, wq, keys)


def _gelu(x):
    return 0.5 * x * (1.0 + lax.erf(x * math.sqrt(0.5)))


def _peer_dense_kernel(x_ref, st_ref, stats_ref, u_ref, vt_ref, g_ref, b_ref, y_ref,
                       xt, e1, e2, wt, acc):
    c = pl.program_id(1)
    rows_per_chunk = EXPERT_CHUNK // N_KEYS

    @pl.when(c == 0)
    def _():
        xt[...] = x_ref[...].T.astype(BF16)
        for h in range(PEER_HEADS):
            m1, m2, inv_z = stats_ref[h, 1:2, :], stats_ref[h, 2:3, :], stats_ref[h, 3:4, :]
            e1[h] = jnp.exp(st_ref[2 * h] - m1) * inv_z
            e2[h] = jnp.exp(st_ref[2 * h + 1] - m2)
        acc[...] = jnp.zeros_like(acc)

    hid = jnp.dot(u_ref[...], xt[...], preferred_element_type=F32)
    for il in range(rows_per_chunk):
        i = c * rows_per_chunk + il
        gate = None
        for h in range(PEER_HEADS):
            s = st_ref[2 * h + 1] + st_ref[2 * h, pl.ds(i, 1), :]
            g = jnp.where(s >= stats_ref[h, 0:1, :], e2[h] * e1[h, pl.ds(i, 1), :], 0.0)
            gate = g if gate is None else gate + g
        rows = slice(il * N_KEYS, (il + 1) * N_KEYS)
        wt[rows, :] = (_gelu(hid[rows, :]) * gate).astype(BF16)
    acc[...] += jnp.dot(vt_ref[...], wt[...], preferred_element_type=F32)

    @pl.when(c == pl.num_programs(1) - 1)
    def _():
        y_ref[...] = _layer_norm(ALPHA * x_ref[...] + acc[...].T, g_ref[...], b_ref[...])


def _peer_dense(x, st, stats, u, vt, ln_g, ln_b):
    rows = x.shape[0]
    tm = min(TOK_TILE, rows)
    assert rows % tm == 0
    n_exp = u.shape[0]
    assert n_exp % EXPERT_CHUNK == 0
    return pl.pallas_call(
        _peer_dense_kernel,
        grid=(rows // tm, n_exp // EXPERT_CHUNK),
        in_specs=[pl.BlockSpec((tm, D_MODEL), lambda t, c: (t, 0)),
                  pl.BlockSpec((2 * PEER_HEADS, N_KEYS, tm), lambda t, c: (0, 0, t)),
                  pl.BlockSpec((PEER_HEADS, SUBLANES, tm), lambda t, c: (0, 0, t)),
                  pl.BlockSpec((EXPERT_CHUNK, D_MODEL), lambda t, c: (c, 0)),
                  pl.BlockSpec((D_MODEL, EXPERT_CHUNK), lambda t, c: (0, c)),
                  pl.BlockSpec((1, D_MODEL), lambda t, c: (0, 0)),
                  pl.BlockSpec((1, D_MODEL), lambda t, c: (0, 0))],
        out_specs=pl.BlockSpec((tm, D_MODEL), lambda t, c: (t, 0)),
        out_shape=jax.ShapeDtypeStruct((rows, D_MODEL), F32),
        scratch_shapes=[pltpu.VMEM((D_MODEL, tm), BF16),
                        pltpu.VMEM((PEER_HEADS, N_KEYS, tm), F32),
                        pltpu.VMEM((PEER_HEADS, N_KEYS, tm), F32),
                        pltpu.VMEM((EXPERT_CHUNK, tm), BF16),
                        pltpu.VMEM((D_MODEL, tm), F32)],
        compiler_params=_cparams(("parallel", "arbitrary")),
        name="peer_dense",
    )(x, st, stats, u, vt, ln_g, ln_b)


def _rel_bucket_np(dist):
    n = np.maximum(dist, 0)
    max_exact = NUM_BUCKETS // 2
    large = max_exact + (np.log(np.maximum(n, max_exact).astype(np.float32) / max_exact)
                         / math.log(MAX_DISTANCE / max_exact) * (NUM_BUCKETS - max_exact)).astype(np.int32)
    return np.where(n < max_exact, n, np.minimum(large, NUM_BUCKETS - 1))


def _prompt_bias(table):
    qi = np.arange(ATTN_BLOCK)[:, None]
    kj = np.arange(2 * ATTN_BLOCK)[None, :]
    dist = ATTN_BLOCK + qi - kj
    ok = (dist >= 0) & (dist < WINDOW)
    ok = np.stack([ok, ok & (kj >= ATTN_BLOCK - N_META)])
    vals = jnp.moveaxis(table[_rel_bucket_np(dist)], -1, 0)
    return jnp.where(ok[:, None], vals[None], NEG).astype(F32)


def _sample_bias(table, s_new):
    g, s, t = np.meshgrid(np.arange(GROUP), np.arange(SAMPLE_SEQS), np.arange(s_new), indexing="ij")
    g, s, t = g.reshape(-1, 1), s.reshape(-1, 1), t.reshape(-1, 1)
    out = []
    for cols, offset in ((WINDOW, WINDOW), (s_new, 0)):
        s2, c = np.meshgrid(np.arange(SAMPLE_SEQS), np.arange(cols), indexing="ij")
        s2, c = s2.reshape(1, -1), c.reshape(1, -1)
        dist = offset + t - c + 0 * s2
        ok = (s == s2) & (dist >= 0) & (dist < WINDOW)
        bucket = _rel_bucket_np(dist)
        tiles = [jnp.where(ok, table[bucket, kh * GROUP + g], NEG) for kh in range(N_KV_HEADS)]
        out.append(jnp.stack(tiles).astype(F32))
    return out


def kernel(x_prompt, x_sample, cache_k, cache_v, state_pool, meta_tokens, rel_bias_table, w_in,
           w_pool_mix, pool_scale, attn_sinks, w_up_pool, w_up_attn, w_out, ln1_g, ln1_b,
           peer_w_query, peer_sub_keys, peer_u, peer_v, ln2_g, ln2_b):
    batch, seq, d = x_prompt.shape
    nb, s_new, _ = x_sample.shape
    assert w_in.shape[0] == DEPTH and d == D_MODEL and seq % ATTN_TILE == 0
    assert N_META >= max(POOL_WINDOWS) - 1 and cache_k.shape[2] == WINDOW

    w_in0 = w_in[0].astype(BF16)
    w_pqkv, w_gates = w_in0[:, :OFF_GA], w_in0[:, OFF_GA:]
    wmix = w_pool_mix[0].astype(BF16)
    scale = pool_scale[0].reshape(1, POOL_WIDTH)
    wup, wua, wout = w_up_pool[0].astype(BF16), w_up_attn[0].astype(BF16), w_out[0].astype(BF16)
    g1, b1 = ln1_g[0].reshape(1, d), ln1_b[0].reshape(1, d)
    g2, b2 = ln2_g[0].reshape(1, d), ln2_b[0].reshape(1, d)
    wq = peer_w_query[0].astype(BF16)
    keys = peer_sub_keys[0].reshape(2 * PEER_HEADS, N_KEYS, PEER_HALF).astype(BF16)
    u = peer_u[0].astype(BF16)
    vt = peer_v[0].astype(BF16).T
    sinks = attn_sinks[0].astype(F32)
    bias_p = _prompt_bias(rel_bias_table)
    bias_c, bias_n = _sample_bias(rel_bias_table, s_new)

    xp = x_prompt.reshape(batch * seq, d)
    xs = x_sample.reshape(nb * s_new, d)
    xm = jnp.concatenate([jnp.zeros((ATTN_BLOCK - N_META, d), F32), meta_tokens.astype(F32)], axis=0)

    p_p, q_p, k_p, v_p = _inproj(xp, w_pqkv)
    p_s, q_s, k_s, v_s = _inproj(xs, w_pqkv)
    p_m, _, k_m, v_m = _inproj(xm, w_pqkv)

    pooled_p, attn_p = _prompt_mix(sinks, q_p, k_p, v_p, p_p, k_m, v_m, p_m[ATTN_BLOCK - N_META:],
                                   bias_p, batch, seq)
    ck = cache_k[0].reshape(nb, WINDOW, KV_DIM)
    cv = cache_v[0].reshape(nb, WINDOW, KV_DIM)
    pooled_s, attn_s, newk, newv, newp = _sample_mix(sinks, q_s, k_s, v_s, p_s, ck, cv,
                                                     state_pool[0], bias_c, bias_n, s_new)

    outs = []
    for x, pooled, attn in ((xp, pooled_p, attn_p), (xs, pooled_s, attn_s)):
        x1 = _merge(x, pooled, attn, w_gates, wmix, scale, wup, wua, wout, g1, b1)
        st, stats = _peer_scores(x1, wq, keys)
        outs.append(_peer_dense(x1, st, stats, u, vt, g2, b2))

    w_keep = min(WINDOW, seq + N_META)
    kv_shape = (batch, seq, N_KV_HEADS, HEAD_DIM)
    return (outs[0].reshape(batch, seq, d),
            outs[1].reshape(nb, s_new, d),
            k_p.reshape(kv_shape)[None, :, seq - w_keep:],
            v_p.reshape(kv_shape)[None, :, seq - w_keep:],
            p_p.reshape(batch, seq, POOL_WIDTH)[None, :, seq - POOL_STATE:],
            newk.reshape(1, nb, WINDOW, N_KV_HEADS, HEAD_DIM),
            newv.reshape(1, nb, WINDOW, N_KV_HEADS, HEAD_DIM),
            newp[None])
```

```python
import functools
import math

import jax
import jax.numpy as jnp
import numpy as np
from jax import lax
from jax.experimental import pallas as pl
from jax.experimental.pallas import tpu as pltpu

F32 = jnp.float32
BF16 = jnp.bfloat16

D_MODEL = 1024
N_META = 16
POOL_WIDTH = 512
POOL_WINDOWS = (2, 4, 8, 16)
POOL_GROUP_DIM = 128
POOL_STATE = 15
HEAD_DIM = 64
N_HEADS = 8
N_KV_HEADS = 2
GROUP = N_HEADS // N_KV_HEADS
WINDOW = 128
ATTN_BLOCK = 128
ATTN_SCALE = HEAD_DIM ** -0.5
NUM_BUCKETS = 32
MAX_DISTANCE = 128
Q_DIM = N_HEADS * HEAD_DIM
KV_DIM = N_KV_HEADS * HEAD_DIM
OFF_Q = POOL_WIDTH
OFF_K = OFF_Q + Q_DIM
OFF_V = OFF_K + KV_DIM
OFF_GA = OFF_V + KV_DIM
PEER_HEADS = 8
N_KEYS = 128
PEER_TOPK = 16
PEER_HALF = 128
DEPTH = 1
ALPHA = (2 * DEPTH) ** 0.25
LN_EPS = 1e-5
NEG = -1e30

LANES = 128
SUBLANES = 8
VMEM_LIMIT = 56 * 1024 * 1024

TOK_TILE = 512
ATTN_TILE = 512
SAMPLE_SEQS = 8
EXPERT_CHUNK = 1024


def _cparams(sem):
    return pltpu.CompilerParams(dimension_semantics=sem, vmem_limit_bytes=VMEM_LIMIT)


def _inproj_kernel(x_ref, w_ref, p_ref, q_ref, k_ref, v_ref):
    z = jnp.dot(x_ref[...].astype(BF16), w_ref[...], preferred_element_type=F32)
    p_ref[...] = z[:, :OFF_Q]
    q_ref[...] = (z[:, OFF_Q:OFF_K] * ATTN_SCALE).astype(BF16)
    k_ref[...] = z[:, OFF_K:OFF_V]
    v_ref[...] = z[:, OFF_V:OFF_GA]


def _inproj(x, w_pqkv):
    rows = x.shape[0]
    tm = min(TOK_TILE, rows)
    assert rows % tm == 0
    row = lambda c: pl.BlockSpec((tm, c), lambda i: (i, 0))
    return pl.pallas_call(
        _inproj_kernel,
        grid=(rows // tm,),
        in_specs=[row(D_MODEL), pl.BlockSpec((D_MODEL, OFF_GA), lambda i: (0, 0))],
        out_specs=[row(POOL_WIDTH), row(Q_DIM), row(KV_DIM), row(KV_DIM)],
        out_shape=[jax.ShapeDtypeStruct((rows, POOL_WIDTH), F32),
                   jax.ShapeDtypeStruct((rows, Q_DIM), BF16),
                   jax.ShapeDtypeStruct((rows, KV_DIM), F32),
                   jax.ShapeDtypeStruct((rows, KV_DIM), F32)],
        compiler_params=_cparams(("parallel",)),
        name="inproj",
    )(x, w_pqkv)


def _window_pool(ext_ref, first, rows, out_ref):
    for g, w in enumerate(POOL_WINDOWS):
        cols = slice(g * POOL_GROUP_DIM, (g + 1) * POOL_GROUP_DIM)
        cur = ext_ref[first:first + rows, cols]
        acc = cur
        for r in range(1, w):
            acc = acc + ext_ref[first - r:first - r + rows, cols]
        out_ref[:, cols] = (acc * (1.0 / w) - cur).astype(out_ref.dtype)


def _sink_softmax(s, sink):
    m = jnp.maximum(jnp.max(s, axis=-1, keepdims=True), sink)
    e = jnp.exp(s - m)
    denom = jnp.sum(e, axis=-1, keepdims=True) + jnp.exp(sink - m)
    return e * (1.0 / denom)


_NT = (((1,), (1,)), ((), ()))


def _prompt_mix_kernel(sink_ref, q_ref, k_ref, v_ref, p_ref, kprev_ref, vprev_ref, pprev_ref,
                       mk_ref, mv_ref, mp_ref, bias_ref, pooled_ref, attn_ref,
                       kbuf, vbuf, pbuf):
    first = pl.program_id(1) == 0
    hist = ATTN_BLOCK
    kbuf[0:hist, :] = jnp.where(first, mk_ref[...], kprev_ref[...]).astype(BF16)
    vbuf[0:hist, :] = jnp.where(first, mv_ref[...], vprev_ref[...]).astype(BF16)
    kbuf[hist:, :] = k_ref[...].astype(BF16)
    vbuf[hist:, :] = v_ref[...].astype(BF16)
    pbuf[0:N_META, :] = jnp.where(first, mp_ref[...], pprev_ref[...])
    pbuf[N_META:, :] = p_ref[...]

    _window_pool(pbuf, N_META, ATTN_TILE, pooled_ref)

    first_i = jnp.where(first, 1, 0)
    for j in range(ATTN_TILE // ATTN_BLOCK):
        rows = slice(j * ATTN_BLOCK, (j + 1) * ATTN_BLOCK)
        keys = slice(j * ATTN_BLOCK, j * ATTN_BLOCK + 2 * ATTN_BLOCK)
        for kh in range(N_KV_HEADS):
            kk = kbuf[keys, kh * HEAD_DIM:(kh + 1) * HEAD_DIM]
            vv = vbuf[keys, kh * HEAD_DIM:(kh + 1) * HEAD_DIM]
            for g in range(GROUP):
                h = kh * GROUP + g
                hc = slice(h * HEAD_DIM, (h + 1) * HEAD_DIM)
                s = lax.dot_general(q_ref[rows, hc], kk, _NT, preferred_element_type=F32)
                bias = bias_ref[first_i, h] if j == 0 else bias_ref[0, h]
                prob = _sink_softmax(s + bias, sink_ref[h])
                o = jnp.dot(prob.astype(BF16), vv, preferred_element_type=F32)
                attn_ref[rows, hc] = o.astype(attn_ref.dtype)


def _prompt_mix(sinks, q, k, v, p, mk, mv, mp, bias, batch, seq):
    nt = seq // ATTN_TILE
    blocks_per_tile = ATTN_TILE // ATTN_BLOCK
    cur = lambda c: pl.BlockSpec((ATTN_TILE, c), lambda b, s: (b * nt + s, 0))
    prev_blk = lambda b, s: (jnp.maximum((b * nt + s) * blocks_per_tile - 1, 0), 0)
    prev_p = lambda b, s: (jnp.maximum((b * nt + s) * (ATTN_TILE // N_META) - 1, 0), 0)
    const2 = lambda b, s: (0, 0)
    rows = batch * seq
    return pl.pallas_call(
        _prompt_mix_kernel,
        grid=(batch, nt),
        in_specs=[pl.BlockSpec(memory_space=pltpu.SMEM),
                  cur(Q_DIM), cur(KV_DIM), cur(KV_DIM), cur(POOL_WIDTH),
                  pl.BlockSpec((ATTN_BLOCK, KV_DIM), prev_blk),
                  pl.BlockSpec((ATTN_BLOCK, KV_DIM), prev_blk),
                  pl.BlockSpec((N_META, POOL_WIDTH), prev_p),
                  pl.BlockSpec((ATTN_BLOCK, KV_DIM), const2),
                  pl.BlockSpec((ATTN_BLOCK, KV_DIM), const2),
                  pl.BlockSpec((N_META, POOL_WIDTH), const2),
                  pl.BlockSpec((2, N_HEADS, ATTN_BLOCK, 2 * ATTN_BLOCK), lambda b, s: (0, 0, 0, 0))],
        out_specs=[cur(POOL_WIDTH), cur(Q_DIM)],
        out_shape=[jax.ShapeDtypeStruct((rows, POOL_WIDTH), BF16),
                   jax.ShapeDtypeStruct((rows, Q_DIM), BF16)],
        scratch_shapes=[pltpu.VMEM((ATTN_BLOCK + ATTN_TILE, KV_DIM), BF16),
                        pltpu.VMEM((ATTN_BLOCK + ATTN_TILE, KV_DIM), BF16),
                        pltpu.VMEM((N_META + ATTN_TILE, POOL_WIDTH), F32)],
        compiler_params=_cparams(("parallel", "parallel")),
        name="prompt_mix",
    )(sinks, q, k, v, p, k, v, p, mk, mv, mp, bias)


def _sample_mix_kernel(sink_ref, q_ref, k_ref, v_ref, p_ref, ck_ref, cv_ref, st_ref,
                       biasc_ref, biasn_ref,
                       pooled_ref, attn_ref, newk_ref, newv_ref, newp_ref, ext, pooled_buf):
    nseq, w_cache, s_new = SAMPLE_SEQS, WINDOW, q_ref.shape[0] // SAMPLE_SEQS
    keep = POOL_STATE - s_new
    for i in range(nseq):
        new = slice(i * s_new, (i + 1) * s_new)
        ext[1:1 + POOL_STATE, :] = st_ref[i]
        ext[1 + POOL_STATE:1 + POOL_STATE + s_new, :] = p_ref[new, :]
        _window_pool(ext, 1 + POOL_STATE, s_new, pooled_buf)
        pooled_ref[new, :] = pooled_buf[...].astype(pooled_ref.dtype)
        newp_ref[i, 0:keep, :] = st_ref[i, s_new:POOL_STATE, :]
        newp_ref[i, keep:POOL_STATE, :] = p_ref[new, :]
        newk_ref[i, 0:w_cache - s_new, :] = ck_ref[i, s_new:w_cache, :]
        newk_ref[i, w_cache - s_new:w_cache, :] = k_ref[new, :]
        newv_ref[i, 0:w_cache - s_new, :] = cv_ref[i, s_new:w_cache, :]
        newv_ref[i, w_cache - s_new:w_cache, :] = v_ref[new, :]

    kc = ck_ref[...].reshape(nseq * w_cache, KV_DIM).astype(BF16)
    vc = cv_ref[...].reshape(nseq * w_cache, KV_DIM).astype(BF16)
    kn = k_ref[...].astype(BF16)
    vn = v_ref[...].astype(BF16)
    q = q_ref[...]
    nq = q.shape[0]
    for kh in range(N_KV_HEADS):
        kvc = slice(kh * HEAD_DIM, (kh + 1) * HEAD_DIM)
        qg = jnp.concatenate([q[:, (kh * GROUP + g) * HEAD_DIM:(kh * GROUP + g + 1) * HEAD_DIM]
                              for g in range(GROUP)], axis=0)
        sc = lax.dot_general(qg, kc[:, kvc], _NT, preferred_element_type=F32) + biasc_ref[kh]
        sn = lax.dot_general(qg, kn[:, kvc], _NT, preferred_element_type=F32) + biasn_ref[kh]
        sink = jnp.concatenate([jnp.full((nq, 1), sink_ref[kh * GROUP + g], F32) for g in range(GROUP)], axis=0)
        m = jnp.maximum(jnp.maximum(jnp.max(sc, axis=-1, keepdims=True),
                                    jnp.max(sn, axis=-1, keepdims=True)), sink)
        ec = jnp.exp(sc - m)
        en = jnp.exp(sn - m)
        denom = (jnp.sum(ec, axis=-1, keepdims=True) + jnp.sum(en, axis=-1, keepdims=True)
                 + jnp.exp(sink - m))
        inv = 1.0 / denom
        o = (jnp.dot((ec * inv).astype(BF16), vc[:, kvc], preferred_element_type=F32)
             + jnp.dot((en * inv).astype(BF16), vn[:, kvc], preferred_element_type=F32))
        for g in range(GROUP):
            h = kh * GROUP + g
            attn_ref[:, h * HEAD_DIM:(h + 1) * HEAD_DIM] = o[g * nq:(g + 1) * nq].astype(attn_ref.dtype)


def _sample_mix(sinks, q, k, v, p, cache_k, cache_v, state, biasc, biasn, s_new):
    nb = cache_k.shape[0]
    rows = SAMPLE_SEQS * s_new
    tok = lambda c: pl.BlockSpec((rows, c), lambda i: (i, 0))
    seq3 = lambda r, c: pl.BlockSpec((SAMPLE_SEQS, r, c), lambda i: (i, 0, 0))
    const3 = lambda a: pl.BlockSpec(a.shape, lambda i: (0, 0, 0))
    return pl.pallas_call(
        _sample_mix_kernel,
        grid=(nb // SAMPLE_SEQS,),
        in_specs=[pl.BlockSpec(memory_space=pltpu.SMEM),
                  tok(Q_DIM), tok(KV_DIM), tok(KV_DIM), tok(POOL_WIDTH),
                  seq3(WINDOW, KV_DIM), seq3(WINDOW, KV_DIM), seq3(POOL_STATE, POOL_WIDTH),
                  const3(biasc), const3(biasn)],
        out_specs=[tok(POOL_WIDTH), tok(Q_DIM),
                   seq3(WINDOW, KV_DIM), seq3(WINDOW, KV_DIM), seq3(POOL_STATE, POOL_WIDTH)],
        out_shape=[jax.ShapeDtypeStruct((nb * s_new, POOL_WIDTH), F32),
                   jax.ShapeDtypeStruct((nb * s_new, Q_DIM), F32),
                   jax.ShapeDtypeStruct((nb, WINDOW, KV_DIM), F32),
                   jax.ShapeDtypeStruct((nb, WINDOW, KV_DIM), F32),
                   jax.ShapeDtypeStruct((nb, POOL_STATE, POOL_WIDTH), F32)],
        scratch_shapes=[pltpu.VMEM((1 + POOL_STATE + SUBLANES, POOL_WIDTH), F32),
                        pltpu.VMEM((s_new, POOL_WIDTH), F32)],
        compiler_params=_cparams(("parallel",)),
        name="sample_mix",
    )(sinks, q, k, v, p, cache_k, cache_v, state, biasc, biasn)


def _layer_norm(x, g, b):
    mu = jnp.mean(x, axis=-1, keepdims=True)
    xc = x - mu
    var = jnp.mean(xc * xc, axis=-1, keepdims=True)
    return xc * lax.rsqrt(var + LN_EPS) * g + b


def _merge_kernel(x_ref, pooled_ref, attn_ref, wg_ref, wmix_ref, scale_ref, wup_ref, wua_ref,
                  wout_ref, g_ref, b_ref, y_ref):
    x = x_ref[...]
    glog = jnp.dot(x.astype(BF16), wg_ref[...], preferred_element_type=F32)
    g_pool = jax.nn.sigmoid(glog[:, :D_MODEL])
    g_attn = jax.nn.sigmoid(glog[:, D_MODEL:])
    pooled = pooled_ref[...].astype(BF16)
    mixed = jnp.concatenate(
        [jnp.dot(pooled[:, g * POOL_GROUP_DIM:(g + 1) * POOL_GROUP_DIM], wmix_ref[g],
                 preferred_element_type=F32) for g in range(len(POOL_WINDOWS))], axis=1)
    pool_out = (mixed * scale_ref[...]).astype(BF16)
    a = jnp.dot(pool_out, wup_ref[...], preferred_element_type=F32)
    b = jnp.dot(attn_ref[...].astype(BF16), wua_ref[...], preferred_element_type=F32)
    m = (g_pool * a + g_attn * b).astype(BF16)
    r = jnp.dot(m, wout_ref[...], preferred_element_type=F32)
    y_ref[...] = _layer_norm(ALPHA * x + r, g_ref[...], b_ref[...])


def _merge(x, pooled, attn, wg, wmix, scale, wup, wua, wout, ln_g, ln_b):
    rows = x.shape[0]
    tm = min(TOK_TILE, rows)
    assert rows % tm == 0
    row = lambda c: pl.BlockSpec((tm, c), lambda i: (i, 0))
    full = lambda a: pl.BlockSpec(a.shape, lambda i: (0,) * a.ndim)
    weights = (wg, wmix, scale, wup, wua, wout, ln_g, ln_b)
    return pl.pallas_call(
        _merge_kernel,
        grid=(rows // tm,),
        in_specs=[row(D_MODEL), row(POOL_WIDTH), row(Q_DIM)] + [full(a) for a in weights],
        out_specs=row(D_MODEL),
        out_shape=jax.ShapeDtypeStruct((rows, D_MODEL), F32),
        compiler_params=_cparams(("parallel",)),
        name="merge",
    )(x, pooled, attn, *weights)


def _oddeven_merge_sort_pairs(n):
    pairs = []
    p = 1
    while p < n:
        k = p
        while k >= 1:
            for j in range(k % p, n - k, 2 * k):
                for i in range(min(k, n - j - k)):
                    if (i + j) // (2 * p) == (i + j + k) // (2 * p):
                        pairs.append((i + j, i + j + k))
            k //= 2
        p *= 2
    return pairs


def _bitonic_merge_pairs(n):
    pairs = []
    k = n // 2
    while k >= 1:
        pairs += [(i, i + k) for i in range(n) if not i & k]
        k //= 2
    return pairs


def _apply_network(vals, pairs):
    vals = list(vals)
    for i, j in pairs:
        a, b = vals[i], vals[j]
        if b is None:
            continue
        if a is None:
            vals[i], vals[j] = b, None
        else:
            vals[i], vals[j] = jnp.maximum(a, b), jnp.minimum(a, b)
    return vals


_SORT16 = _oddeven_merge_sort_pairs(PEER_TOPK)
_MERGE16 = _bitonic_merge_pairs(PEER_TOPK)
_SORT64 = _oddeven_merge_sort_pairs(64)
_CANDIDATES = [(a, b) for a in range(PEER_TOPK) for b in range(PEER_TOPK) if (a + 1) * (b + 1) <= PEER_TOPK]


def _top16_rows(sc):
    n = sc.shape[0] // SUBLANES
    assert n == PEER_TOPK
    x = _apply_network([sc[k * SUBLANES:(k + 1) * SUBLANES, :] for k in range(n)], _SORT16)
    for shift in (4, 2, 1):
        y = [jnp.maximum(x[k], pltpu.roll(x[n - 1 - k], shift, 0)) for k in range(n)]
        x = _apply_network(y, _MERGE16)
    return x


def _peer_scores_kernel(x_ref, wq_ref, keys_ref, st_ref, stats_ref):
    q = jnp.dot(x_ref[...].astype(BF16), wq_ref[...], preferred_element_type=F32).astype(BF16)
    for h in range(PEER_HEADS):
        tops = []
        for c in range(2):
            col = (h * 2 + c) * PEER_HALF
            sc = lax.dot_general(keys_ref[h * 2 + c], q[:, col:col + PEER_HALF], _NT,
                                 preferred_element_type=F32)
            for t in range(sc.shape[1] // LANES):
                st_ref[h * 2 + c, t] = sc[:, t * LANES:(t + 1) * LANES]
            tops.append(_top16_rows(sc))
        cand = [tops[0][a] + tops[1][b] for a, b in _CANDIDATES]
        best = _apply_network(cand + [None] * (64 - len(cand)), _SORT64)[:PEER_TOPK]
        z = jnp.ones_like(best[0])
        for r in range(1, PEER_TOPK):
            z = z + jnp.exp(best[r] - best[0])
        row = lax.broadcasted_iota(jnp.int32, z.shape, 0)
        stats_ref[h] = jnp.where(row == 0, best[PEER_TOPK - 1],
                                 jnp.where(row == 1, tops[0][0],
                                           jnp.where(row == 2, tops[1][0], 1.0 / z)))


def _peer_scores(x, wq, keys):
    rows = x.shape[0]
    tm = min(TOK_TILE, rows)
    assert rows % tm == 0
    return pl.pallas_call(
        _peer_scores_kernel,
        grid=(rows // tm,),
        in_specs=[pl.BlockSpec((tm, D_MODEL), lambda i: (i, 0)),
                  pl.BlockSpec(wq.shape, lambda i: (0, 0)),
                  pl.BlockSpec(keys.shape, lambda i: (0, 0, 0))],
        out_specs=[pl.BlockSpec((2 * PEER_HEADS, tm // LANES, N_KEYS, LANES), lambda i: (0, i, 0, 0)),
                   pl.BlockSpec((PEER_HEADS, SUBLANES, tm), lambda i: (0, 0, i))],
        out_shape=[jax.ShapeDtypeStruct((2 * PEER_HEADS, rows // LANES, N_KEYS, LANES), F32),
                   jax.ShapeDtypeStruct((PEER_HEADS, SUBLANES, rows), F32)],
        compiler_params=_cparams(("parallel",)),
        name="peer_scores",
    )(x, wq, keys)


def _gelu(x):
    return 0.5 * x * (1.0 + lax.erf(x * math.sqrt(0.5)))


def _peer_dense_kernel(x_ref, st_ref, stats_ref, u_ref, vt_ref, g_ref, b_ref, y_ref,
                       xt, e1, e2, rowb, hid, wt, acc):
    c = pl.program_id(1)
    rows_per_chunk = EXPERT_CHUNK // N_KEYS
    ncol = xt.shape[1] // LANES
    col_lanes = [slice(t * LANES, (t + 1) * LANES) for t in range(ncol)]

    @pl.when(c == 0)
    def _():
        xt[...] = x_ref[...].T.astype(BF16)
        for h in range(PEER_HEADS):
            for t, lanes in enumerate(col_lanes):
                m1, m2, inv_z = stats_ref[h, 1:2, lanes], stats_ref[h, 2:3, lanes], stats_ref[h, 3:4, lanes]
                e1[h, t] = jnp.exp(st_ref[2 * h, t] - m1) * inv_z
                e2[h, t] = jnp.exp(st_ref[2 * h + 1, t] - m2)
        acc[...] = jnp.zeros_like(acc)

    res = jnp.dot(u_ref[...], xt[...], preferred_element_type=F32)
    for t, lanes in enumerate(col_lanes):
        hid[t] = res[:, lanes]
    assert rows_per_chunk == SUBLANES
    group = pl.ds(pl.multiple_of(c * SUBLANES, SUBLANES), SUBLANES)
    for h in range(PEER_HEADS):
        for t in range(ncol):
            s1_rows, g1_rows = st_ref[2 * h, t, group, :], e1[h, t, group, :]
            for r in range(SUBLANES):
                rowb[r, h, t] = jnp.broadcast_to(s1_rows[r:r + 1, :], s1_rows.shape)
                rowb[r, PEER_HEADS + h, t] = jnp.broadcast_to(g1_rows[r:r + 1, :], g1_rows.shape)

    for il in range(rows_per_chunk):
        base = il * N_KEYS
        for t, lanes in enumerate(col_lanes):
            gate = [None] * (N_KEYS // SUBLANES)
            for h in range(PEER_HEADS):
                s1 = rowb[il, h, t]
                g1 = rowb[il, PEER_HEADS + h, t]
                tau = jnp.broadcast_to(stats_ref[h, 0:1, lanes], (SUBLANES, LANES))
                for k in range(N_KEYS // SUBLANES):
                    keys = slice(k * SUBLANES, (k + 1) * SUBLANES)
                    s = st_ref[2 * h + 1, t, keys, :] + s1
                    g = jnp.where(s >= tau, e2[h, t, keys, :] * g1, 0.0)
                    gate[k] = g if gate[k] is None else gate[k] + g
            for k in range(0, N_KEYS // SUBLANES, 2):
                w = [_gelu(hid[t, pl.ds(base + kk * SUBLANES, SUBLANES), :]) * gate[kk] for kk in (k, k + 1)]
                wt[pl.ds(base + k * SUBLANES, 2 * SUBLANES), lanes] = jnp.concatenate(w, axis=0).astype(BF16)
    acc[...] += jnp.dot(vt_ref[...], wt[...], preferred_element_type=F32)

    @pl.when(c == pl.num_programs(1) - 1)
    def _():
        y_ref[...] = _layer_norm(ALPHA * x_ref[...] + acc[...].T, g_ref[...], b_ref[...])


def _peer_dense(x, st, stats, u, vt, ln_g, ln_b):
    rows = x.shape[0]
    tm = min(TOK_TILE, rows)
    assert rows % tm == 0
    n_exp = u.shape[0]
    assert n_exp % EXPERT_CHUNK == 0
    return pl.pallas_call(
        _peer_dense_kernel,
        grid=(rows // tm, n_exp // EXPERT_CHUNK),
        in_specs=[pl.BlockSpec((tm, D_MODEL), lambda t, c: (t, 0)),
                  pl.BlockSpec((2 * PEER_HEADS, tm // LANES, N_KEYS, LANES), lambda t, c: (0, t, 0, 0)),
                  pl.BlockSpec((PEER_HEADS, SUBLANES, tm), lambda t, c: (0, 0, t)),
                  pl.BlockSpec((EXPERT_CHUNK, D_MODEL), lambda t, c: (c, 0)),
                  pl.BlockSpec((D_MODEL, EXPERT_CHUNK), lambda t, c: (0, c)),
                  pl.BlockSpec((1, D_MODEL), lambda t, c: (0, 0)),
                  pl.BlockSpec((1, D_MODEL), lambda t, c: (0, 0))],
        out_specs=pl.BlockSpec((tm, D_MODEL), lambda t, c: (t, 0)),
        out_shape=jax.ShapeDtypeStruct((rows, D_MODEL), F32),
        scratch_shapes=[pltpu.VMEM((D_MODEL, tm), BF16),
                        pltpu.VMEM((PEER_HEADS, tm // LANES, N_KEYS, LANES), F32),
                        pltpu.VMEM((PEER_HEADS, tm // LANES, N_KEYS, LANES), F32),
                        pltpu.VMEM((SUBLANES, 2 * PEER_HEADS, tm // LANES, SUBLANES, LANES), F32),
                        pltpu.VMEM((tm // LANES, EXPERT_CHUNK, LANES), F32),
                        pltpu.VMEM((EXPERT_CHUNK, tm), BF16),
                        pltpu.VMEM((D_MODEL, tm), F32)],
        compiler_params=_cparams(("parallel", "arbitrary")),
        name="peer_dense",
    )(x, st, stats, u, vt, ln_g, ln_b)


def _rel_bucket_np(dist):
    n = np.maximum(dist, 0)
    max_exact = NUM_BUCKETS // 2
    large = max_exact + (np.log(np.maximum(n, max_exact).astype(np.float32) / max_exact)
                         / math.log(MAX_DISTANCE / max_exact) * (NUM_BUCKETS - max_exact)).astype(np.int32)
    return np.where(n < max_exact, n, np.minimum(large, NUM_BUCKETS - 1))


def _bias_kernel(table_ref, bucket_ref, out_ref, *, head_stride):
    head = pl.program_id(0) * head_stride + pl.program_id(1)
    bucket = bucket_ref[0]
    acc = jnp.full(bucket.shape, NEG, F32)
    for b in range(NUM_BUCKETS):
        acc = jnp.where(bucket == b, table_ref[b, head], acc)
    out_ref[0, 0] = acc


def _bias_tiles(table, bucket, n_outer, n_inner, head_stride):
    r, c = bucket.shape[1:]
    bmap = (lambda a, b: (a, 0, 0)) if bucket.shape[0] > 1 else (lambda a, b: (0, 0, 0))
    return pl.pallas_call(
        functools.partial(_bias_kernel, head_stride=head_stride),
        grid=(n_outer, n_inner),
        in_specs=[pl.BlockSpec(memory_space=pltpu.SMEM), pl.BlockSpec((1, r, c), bmap)],
        out_specs=pl.BlockSpec((1, 1, r, c), lambda a, b: (a, b, 0, 0)),
        out_shape=jax.ShapeDtypeStruct((n_outer, n_inner, r, c), F32),
        compiler_params=_cparams(("parallel", "parallel")),
        name="bias_tiles",
    )(table, jnp.asarray(bucket, jnp.int32))


def _prompt_bias(table):
    qi = np.arange(ATTN_BLOCK)[:, None]
    kj = np.arange(2 * ATTN_BLOCK)[None, :]
    dist = ATTN_BLOCK + qi - kj
    ok = (dist >= 0) & (dist < WINDOW)
    ok = np.stack([ok, ok & (kj >= ATTN_BLOCK - N_META)])
    bucket = np.where(ok, _rel_bucket_np(dist)[None], -1)
    return _bias_tiles(table, bucket, 2, N_HEADS, 0)


def _sample_bias(table, s_new):
    s, t = np.meshgrid(np.arange(SAMPLE_SEQS), np.arange(s_new), indexing="ij")
    s, t = s.reshape(-1, 1), t.reshape(-1, 1)
    out = []
    for cols, offset in ((WINDOW, WINDOW), (s_new, 0)):
        s2, c = np.meshgrid(np.arange(SAMPLE_SEQS), np.arange(cols), indexing="ij")
        s2, c = s2.reshape(1, -1), c.reshape(1, -1)
        dist = offset + t - c + 0 * s2
        ok = (s == s2) & (dist >= 0) & (dist < WINDOW)
        bucket = np.where(ok, _rel_bucket_np(dist), -1)[None]
        tiles = _bias_tiles(table, bucket, N_KV_HEADS, GROUP, GROUP)
        out.append(tiles.reshape(N_KV_HEADS, GROUP * bucket.shape[1], bucket.shape[2]))
    return out


def kernel(x_prompt, x_sample, cache_k, cache_v, state_pool, meta_tokens, rel_bias_table, w_in,
           w_pool_mix, pool_scale, attn_sinks, w_up_pool, w_up_attn, w_out, ln1_g, ln1_b,
           peer_w_query, peer_sub_keys, peer_u, peer_v, ln2_g, ln2_b):
    batch, seq, d = x_prompt.shape
    nb, s_new, _ = x_sample.shape
    assert w_in.shape[0] == DEPTH and d == D_MODEL and seq % ATTN_TILE == 0
    assert N_META >= max(POOL_WINDOWS) - 1 and cache_k.shape[2] == WINDOW

    w_in0 = w_in[0].astype(BF16)
    w_pqkv, w_gates = w_in0[:, :OFF_GA], w_in0[:, OFF_GA:]
    wmix = w_pool_mix[0].astype(BF16)
    scale = pool_scale[0].reshape(1, POOL_WIDTH)
    wup, wua, wout = w_up_pool[0].astype(BF16), w_up_attn[0].astype(BF16), w_out[0].astype(BF16)
    g1, b1 = ln1_g[0].reshape(1, d), ln1_b[0].reshape(1, d)
    g2, b2 = ln2_g[0].reshape(1, d), ln2_b[0].reshape(1, d)
    wq = peer_w_query[0].astype(BF16)
    keys = peer_sub_keys[0].reshape(2 * PEER_HEADS, N_KEYS, PEER_HALF).astype(BF16)
    u = peer_u[0].astype(BF16)
    vt = peer_v[0].astype(BF16).T
    sinks = attn_sinks[0].astype(F32)
    table = rel_bias_table.astype(F32)
    bias_p = _prompt_bias(table)
    bias_c, bias_n = _sample_bias(table, s_new)

    xp = x_prompt.reshape(batch * seq, d)
    xs = x_sample.reshape(nb * s_new, d)
    xm = jnp.concatenate([jnp.zeros((ATTN_BLOCK - N_META, d), F32), meta_tokens.astype(F32)], axis=0)

    p_p, q_p, k_p, v_p = _inproj(xp, w_pqkv)
    p_s, q_s, k_s, v_s = _inproj(xs, w_pqkv)
    p_m, _, k_m, v_m = _inproj(xm, w_pqkv)

    pooled_p, attn_p = _prompt_mix(sinks, q_p, k_p, v_p, p_p, k_m, v_m, p_m[ATTN_BLOCK - N_META:],
                                   bias_p, batch, seq)
    ck = cache_k[0].reshape(nb, WINDOW, KV_DIM)
    cv = cache_v[0].reshape(nb, WINDOW, KV_DIM)
    pooled_s, attn_s, newk, newv, newp = _sample_mix(sinks, q_s, k_s, v_s, p_s, ck, cv,
                                                     state_pool[0], bias_c, bias_n, s_new)

    outs = []
    for x, pooled, attn in ((xp, pooled_p, attn_p), (xs, pooled_s, attn_s)):
        x1 = _merge(x, pooled, attn, w_gates, wmix, scale, wup, wua, wout, g1, b1)
        st, stats = _peer_scores(x1, wq, keys)
        outs.append(_peer_dense(x1, st, stats, u, vt, g2, b2))

    w_keep = min(WINDOW, seq + N_META)
    kv_shape = (batch, seq, N_KV_HEADS, HEAD_DIM)
    return (outs[0].reshape(batch, seq, d),
            outs[1].reshape(nb, s_new, d),
            k_p.reshape(kv_shape)[None, :, seq - w_keep:],
            v_p.reshape(kv_shape)[None, :, seq - w_keep:],
            p_p.reshape(batch, seq, POOL_WIDTH)[None, :, seq - POOL_STATE:],
            newk.reshape(1, nb, WINDOW, N_KV_HEADS, HEAD_DIM),
            newv.reshape(1, nb, WINDOW, N_KV_HEADS, HEAD_DIM),
            newp[None])
```

```python
import functools
import math

import jax
import jax.numpy as jnp
import numpy as np
from jax import lax
from jax.experimental import pallas as pl
from jax.experimental.pallas import tpu as pltpu

F32 = jnp.float32
BF16 = jnp.bfloat16

D_MODEL = 1024
N_META = 16
POOL_WIDTH = 512
POOL_WINDOWS = (2, 4, 8, 16)
POOL_GROUP_DIM = 128
POOL_STATE = 15
HEAD_DIM = 64
N_HEADS = 8
N_KV_HEADS = 2
GROUP = N_HEADS // N_KV_HEADS
WINDOW = 128
ATTN_BLOCK = 128
ATTN_SCALE = HEAD_DIM ** -0.5
NUM_BUCKETS = 32
MAX_DISTANCE = 128
Q_DIM = N_HEADS * HEAD_DIM
KV_DIM = N_KV_HEADS * HEAD_DIM
OFF_Q = POOL_WIDTH
OFF_K = OFF_Q + Q_DIM
OFF_V = OFF_K + KV_DIM
OFF_GA = OFF_V + KV_DIM
PEER_HEADS = 8
N_KEYS = 128
PEER_TOPK = 16
PEER_HALF = 128
DEPTH = 1
ALPHA = (2 * DEPTH) ** 0.25
LN_EPS = 1e-5
NEG = -1e30

LANES = 128
SUBLANES = 8
VMEM_LIMIT = 56 * 1024 * 1024

TOK_TILE = 512
ATTN_TILE = 512
SAMPLE_SEQS = 8
EXPERT_CHUNK = 1024


def _cparams(sem):
    return pltpu.CompilerParams(dimension_semantics=sem, vmem_limit_bytes=VMEM_LIMIT)


def _inproj_kernel(x_ref, w_ref, p_ref, q_ref, k_ref, v_ref):
    z = jnp.dot(x_ref[...].astype(BF16), w_ref[...], preferred_element_type=F32)
    p_ref[...] = z[:, :OFF_Q]
    q_ref[...] = (z[:, OFF_Q:OFF_K] * ATTN_SCALE).astype(BF16)
    k_ref[...] = z[:, OFF_K:OFF_V]
    v_ref[...] = z[:, OFF_V:OFF_GA]


def _inproj(x, w_pqkv):
    rows = x.shape[0]
    tm = min(TOK_TILE, rows)
    assert rows % tm == 0
    row = lambda c: pl.BlockSpec((tm, c), lambda i: (i, 0))
    return pl.pallas_call(
        _inproj_kernel,
        grid=(rows // tm,),
        in_specs=[row(D_MODEL), pl.BlockSpec((D_MODEL, OFF_GA), lambda i: (0, 0))],
        out_specs=[row(POOL_WIDTH), row(Q_DIM), row(KV_DIM), row(KV_DIM)],
        out_shape=[jax.ShapeDtypeStruct((rows, POOL_WIDTH), F32),
                   jax.ShapeDtypeStruct((rows, Q_DIM), BF16),
                   jax.ShapeDtypeStruct((rows, KV_DIM), F32),
                   jax.ShapeDtypeStruct((rows, KV_DIM), F32)],
        compiler_params=_cparams(("parallel",)),
        name="inproj",
    )(x, w_pqkv)


def _window_pool(ext_ref, first, rows, out_ref):
    for g, w in enumerate(POOL_WINDOWS):
        cols = slice(g * POOL_GROUP_DIM, (g + 1) * POOL_GROUP_DIM)
        cur = ext_ref[first:first + rows, cols]
        acc = cur
        for r in range(1, w):
            acc = acc + ext_ref[first - r:first - r + rows, cols]
        out_ref[:, cols] = (acc * (1.0 / w) - cur).astype(out_ref.dtype)


def _sink_softmax(s, sink):
    m = jnp.maximum(jnp.max(s, axis=-1, keepdims=True), sink)
    e = jnp.exp(s - m)
    denom = jnp.sum(e, axis=-1, keepdims=True) + jnp.exp(sink - m)
    return e * (1.0 / denom)


_NT = (((1,), (1,)), ((), ()))


def _prompt_mix_kernel(sink_ref, q_ref, k_ref, v_ref, p_ref, kprev_ref, vprev_ref, pprev_ref,
                       mk_ref, mv_ref, mp_ref, bias_ref, pooled_ref, attn_ref,
                       kbuf, vbuf, pbuf):
    first = pl.program_id(1) == 0
    hist = ATTN_BLOCK
    kbuf[0:hist, :] = jnp.where(first, mk_ref[...], kprev_ref[...]).astype(BF16)
    vbuf[0:hist, :] = jnp.where(first, mv_ref[...], vprev_ref[...]).astype(BF16)
    kbuf[hist:, :] = k_ref[...].astype(BF16)
    vbuf[hist:, :] = v_ref[...].astype(BF16)
    pbuf[0:N_META, :] = jnp.where(first, mp_ref[...], pprev_ref[...])
    pbuf[N_META:, :] = p_ref[...]

    _window_pool(pbuf, N_META, ATTN_TILE, pooled_ref)

    first_i = jnp.where(first, 1, 0)
    for j in range(ATTN_TILE // ATTN_BLOCK):
        rows = slice(j * ATTN_BLOCK, (j + 1) * ATTN_BLOCK)
        keys = slice(j * ATTN_BLOCK, j * ATTN_BLOCK + 2 * ATTN_BLOCK)
        for kh in range(N_KV_HEADS):
            kk = kbuf[keys, kh * HEAD_DIM:(kh + 1) * HEAD_DIM]
            vv = vbuf[keys, kh * HEAD_DIM:(kh + 1) * HEAD_DIM]
            for g in range(GROUP):
                h = kh * GROUP + g
                hc = slice(h * HEAD_DIM, (h + 1) * HEAD_DIM)
                s = lax.dot_general(q_ref[rows, hc], kk, _NT, preferred_element_type=F32)
                bias = bias_ref[first_i, h] if j == 0 else bias_ref[0, h]
                prob = _sink_softmax(s + bias, sink_ref[h])
                o = jnp.dot(prob.astype(BF16), vv, preferred_element_type=F32)
                attn_ref[rows, hc] = o.astype(attn_ref.dtype)


def _prompt_mix(sinks, q, k, v, p, mk, mv, mp, bias, batch, seq):
    nt = seq // ATTN_TILE
    blocks_per_tile = ATTN_TILE // ATTN_BLOCK
    cur = lambda c: pl.BlockSpec((ATTN_TILE, c), lambda b, s: (b * nt + s, 0))
    prev_blk = lambda b, s: (jnp.maximum((b * nt + s) * blocks_per_tile - 1, 0), 0)
    prev_p = lambda b, s: (jnp.maximum((b * nt + s) * (ATTN_TILE // N_META) - 1, 0), 0)
    const2 = lambda b, s: (0, 0)
    rows = batch * seq
    return pl.pallas_call(
        _prompt_mix_kernel,
        grid=(batch, nt),
        in_specs=[pl.BlockSpec(memory_space=pltpu.SMEM),
                  cur(Q_DIM), cur(KV_DIM), cur(KV_DIM), cur(POOL_WIDTH),
                  pl.BlockSpec((ATTN_BLOCK, KV_DIM), prev_blk),
                  pl.BlockSpec((ATTN_BLOCK, KV_DIM), prev_blk),
                  pl.BlockSpec((N_META, POOL_WIDTH), prev_p),
                  pl.BlockSpec((ATTN_BLOCK, KV_DIM), const2),
                  pl.BlockSpec((ATTN_BLOCK, KV_DIM), const2),
                  pl.BlockSpec((N_META, POOL_WIDTH), const2),
                  pl.BlockSpec((2, N_HEADS, ATTN_BLOCK, 2 * ATTN_BLOCK), lambda b, s: (0, 0, 0, 0))],
        out_specs=[cur(POOL_WIDTH), cur(Q_DIM)],
        out_shape=[jax.ShapeDtypeStruct((rows, POOL_WIDTH), BF16),
                   jax.ShapeDtypeStruct((rows, Q_DIM), BF16)],
        scratch_shapes=[pltpu.VMEM((ATTN_BLOCK + ATTN_TILE, KV_DIM), BF16),
                        pltpu.VMEM((ATTN_BLOCK + ATTN_TILE, KV_DIM), BF16),
                        pltpu.VMEM((N_META + ATTN_TILE, POOL_WIDTH), F32)],
        compiler_params=_cparams(("parallel", "parallel")),
        name="prompt_mix",
    )(sinks, q, k, v, p, k, v, p, mk, mv, mp, bias)


def _sample_mix_kernel(sink_ref, q_ref, k_ref, v_ref, p_ref, ck_ref, cv_ref, st_ref,
                       biasc_ref, biasn_ref,
                       pooled_ref, attn_ref, newk_ref, newv_ref, newp_ref, ext, pooled_buf):
    nseq, w_cache, s_new = SAMPLE_SEQS, WINDOW, q_ref.shape[0] // SAMPLE_SEQS
    keep = POOL_STATE - s_new
    for i in range(nseq):
        new = slice(i * s_new, (i + 1) * s_new)
        ext[1:1 + POOL_STATE, :] = st_ref[i]
        ext[1 + POOL_STATE:1 + POOL_STATE + s_new, :] = p_ref[new, :]
        _window_pool(ext, 1 + POOL_STATE, s_new, pooled_buf)
        pooled_ref[new, :] = pooled_buf[...].astype(pooled_ref.dtype)
        newp_ref[i, 0:keep, :] = st_ref[i, s_new:POOL_STATE, :]
        newp_ref[i, keep:POOL_STATE, :] = p_ref[new, :]
        newk_ref[i, 0:w_cache - s_new, :] = ck_ref[i, s_new:w_cache, :]
        newk_ref[i, w_cache - s_new:w_cache, :] = k_ref[new, :]
        newv_ref[i, 0:w_cache - s_new, :] = cv_ref[i, s_new:w_cache, :]
        newv_ref[i, w_cache - s_new:w_cache, :] = v_ref[new, :]

    kc = ck_ref[...].reshape(nseq * w_cache, KV_DIM).astype(BF16)
    vc = cv_ref[...].reshape(nseq * w_cache, KV_DIM).astype(BF16)
    kn = k_ref[...].astype(BF16)
    vn = v_ref[...].astype(BF16)
    q = q_ref[...]
    nq = q.shape[0]
    for kh in range(N_KV_HEADS):
        kvc = slice(kh * HEAD_DIM, (kh + 1) * HEAD_DIM)
        qg = jnp.concatenate([q[:, (kh * GROUP + g) * HEAD_DIM:(kh * GROUP + g + 1) * HEAD_DIM]
                              for g in range(GROUP)], axis=0)
        sc = lax.dot_general(qg, kc[:, kvc], _NT, preferred_element_type=F32) + biasc_ref[kh]
        sn = lax.dot_general(qg, kn[:, kvc], _NT, preferred_element_type=F32) + biasn_ref[kh]
        sink = jnp.concatenate([jnp.full((nq, 1), sink_ref[kh * GROUP + g], F32) for g in range(GROUP)], axis=0)
        m = jnp.maximum(jnp.maximum(jnp.max(sc, axis=-1, keepdims=True),
                                    jnp.max(sn, axis=-1, keepdims=True)), sink)
        ec = jnp.exp(sc - m)
        en = jnp.exp(sn - m)
        denom = (jnp.sum(ec, axis=-1, keepdims=True) + jnp.sum(en, axis=-1, keepdims=True)
                 + jnp.exp(sink - m))
        inv = 1.0 / denom
        o = (jnp.dot((ec * inv).astype(BF16), vc[:, kvc], preferred_element_type=F32)
             + jnp.dot((en * inv).astype(BF16), vn[:, kvc], preferred_element_type=F32))
        for g in range(GROUP):
            h = kh * GROUP + g
            attn_ref[:, h * HEAD_DIM:(h + 1) * HEAD_DIM] = o[g * nq:(g + 1) * nq].astype(attn_ref.dtype)


def _sample_mix(sinks, q, k, v, p, cache_k, cache_v, state, biasc, biasn, s_new):
    nb = cache_k.shape[0]
    rows = SAMPLE_SEQS * s_new
    tok = lambda c: pl.BlockSpec((rows, c), lambda i: (i, 0))
    seq3 = lambda r, c: pl.BlockSpec((SAMPLE_SEQS, r, c), lambda i: (i, 0, 0))
    const3 = lambda a: pl.BlockSpec(a.shape, lambda i: (0, 0, 0))
    return pl.pallas_call(
        _sample_mix_kernel,
        grid=(nb // SAMPLE_SEQS,),
        in_specs=[pl.BlockSpec(memory_space=pltpu.SMEM),
                  tok(Q_DIM), tok(KV_DIM), tok(KV_DIM), tok(POOL_WIDTH),
                  seq3(WINDOW, KV_DIM), seq3(WINDOW, KV_DIM), seq3(POOL_STATE, POOL_WIDTH),
                  const3(biasc), const3(biasn)],
        out_specs=[tok(POOL_WIDTH), tok(Q_DIM),
                   seq3(WINDOW, KV_DIM), seq3(WINDOW, KV_DIM), seq3(POOL_STATE, POOL_WIDTH)],
        out_shape=[jax.ShapeDtypeStruct((nb * s_new, POOL_WIDTH), F32),
                   jax.ShapeDtypeStruct((nb * s_new, Q_DIM), F32),
                   jax.ShapeDtypeStruct((nb, WINDOW, KV_DIM), F32),
                   jax.ShapeDtypeStruct((nb, WINDOW, KV_DIM), F32),
                   jax.ShapeDtypeStruct((nb, POOL_STATE, POOL_WIDTH), F32)],
        scratch_shapes=[pltpu.VMEM((1 + POOL_STATE + SUBLANES, POOL_WIDTH), F32),
                        pltpu.VMEM((s_new, POOL_WIDTH), F32)],
        compiler_params=_cparams(("parallel",)),
        name="sample_mix",
    )(sinks, q, k, v, p, cache_k, cache_v, state, biasc, biasn)


def _layer_norm(x, g, b):
    mu = jnp.mean(x, axis=-1, keepdims=True)
    xc = x - mu
    var = jnp.mean(xc * xc, axis=-1, keepdims=True)
    return xc * lax.rsqrt(var + LN_EPS) * g + b


def _merge_kernel(x_ref, pooled_ref, attn_ref, wg_ref, wmix_ref, scale_ref, wup_ref, wua_ref,
                  wout_ref, g_ref, b_ref, y_ref):
    x = x_ref[...]
    glog = jnp.dot(x.astype(BF16), wg_ref[...], preferred_element_type=F32)
    g_pool = jax.nn.sigmoid(glog[:, :D_MODEL])
    g_attn = jax.nn.sigmoid(glog[:, D_MODEL:])
    pooled = pooled_ref[...].astype(BF16)
    mixed = jnp.concatenate(
        [jnp.dot(pooled[:, g * POOL_GROUP_DIM:(g + 1) * POOL_GROUP_DIM], wmix_ref[g],
                 preferred_element_type=F32) for g in range(len(POOL_WINDOWS))], axis=1)
    pool_out = (mixed * scale_ref[...]).astype(BF16)
    a = jnp.dot(pool_out, wup_ref[...], preferred_element_type=F32)
    b = jnp.dot(attn_ref[...].astype(BF16), wua_ref[...], preferred_element_type=F32)
    m = (g_pool * a + g_attn * b).astype(BF16)
    r = jnp.dot(m, wout_ref[...], preferred_element_type=F32)
    y_ref[...] = _layer_norm(ALPHA * x + r, g_ref[...], b_ref[...])


def _merge(x, pooled, attn, wg, wmix, scale, wup, wua, wout, ln_g, ln_b):
    rows = x.shape[0]
    tm = min(TOK_TILE, rows)
    assert rows % tm == 0
    row = lambda c: pl.BlockSpec((tm, c), lambda i: (i, 0))
    full = lambda a: pl.BlockSpec(a.shape, lambda i: (0,) * a.ndim)
    weights = (wg, wmix, scale, wup, wua, wout, ln_g, ln_b)
    return pl.pallas_call(
        _merge_kernel,
        grid=(rows // tm,),
        in_specs=[row(D_MODEL), row(POOL_WIDTH), row(Q_DIM)] + [full(a) for a in weights],
        out_specs=row(D_MODEL),
        out_shape=jax.ShapeDtypeStruct((rows, D_MODEL), F32),
        compiler_params=_cparams(("parallel",)),
        name="merge",
    )(x, pooled, attn, *weights)


def _oddeven_merge_sort_pairs(n):
    pairs = []
    p = 1
    while p < n:
        k = p
        while k >= 1:
            for j in range(k % p, n - k, 2 * k):
                for i in range(min(k, n - j - k)):
                    if (i + j) // (2 * p) == (i + j + k) // (2 * p):
                        pairs.append((i + j, i + j + k))
            k //= 2
        p *= 2
    return pairs


def _bitonic_merge_pairs(n):
    pairs = []
    k = n // 2
    while k >= 1:
        pairs += [(i, i + k) for i in range(n) if not i & k]
        k //= 2
    return pairs


def _apply_network(vals, pairs):
    vals = list(vals)
    for i, j in pairs:
        a, b = vals[i], vals[j]
        if b is None:
            continue
        if a is None:
            vals[i], vals[j] = b, None
        else:
            vals[i], vals[j] = jnp.maximum(a, b), jnp.minimum(a, b)
    return vals


_SORT16 = _oddeven_merge_sort_pairs(PEER_TOPK)
_MERGE16 = _bitonic_merge_pairs(PEER_TOPK)
_SORT64 = _oddeven_merge_sort_pairs(64)
_CANDIDATES = [(a, b) for a in range(PEER_TOPK) for b in range(PEER_TOPK) if (a + 1) * (b + 1) <= PEER_TOPK]


def _top16_rows(sc):
    n = sc.shape[0] // SUBLANES
    assert n == PEER_TOPK
    x = _apply_network([sc[k * SUBLANES:(k + 1) * SUBLANES, :] for k in range(n)], _SORT16)
    for shift in (4, 2, 1):
        y = [jnp.maximum(x[k], pltpu.roll(x[n - 1 - k], shift, 0)) for k in range(n)]
        x = _apply_network(y, _MERGE16)
    return x


def _peer_scores_kernel(x_ref, wq_ref, keys_ref, st_ref, stats_ref):
    q = jnp.dot(x_ref[...].astype(BF16), wq_ref[...], preferred_element_type=F32).astype(BF16)
    for h in range(PEER_HEADS):
        tops = []
        for c in range(2):
            col = (h * 2 + c) * PEER_HALF
            sc = lax.dot_general(keys_ref[h * 2 + c], q[:, col:col + PEER_HALF], _NT,
                                 preferred_element_type=F32)
            for t in range(sc.shape[1] // LANES):
                st_ref[h * 2 + c, t] = sc[:, t * LANES:(t + 1) * LANES]
            tops.append(_top16_rows(sc))
        cand = [tops[0][a] + tops[1][b] for a, b in _CANDIDATES]
        best = _apply_network(cand + [None] * (64 - len(cand)), _SORT64)[:PEER_TOPK]
        z = jnp.ones_like(best[0])
        for r in range(1, PEER_TOPK):
            z = z + jnp.exp(best[r] - best[0])
        row = lax.broadcasted_iota(jnp.int32, z.shape, 0)
        stats_ref[h] = jnp.where(row == 0, best[PEER_TOPK - 1],
                                 jnp.where(row == 1, tops[0][0],
                                           jnp.where(row == 2, tops[1][0], 1.0 / z)))


def _peer_scores(x, wq, keys):
    rows = x.shape[0]
    tm = min(TOK_TILE, rows)
    assert rows % tm == 0
    return pl.pallas_call(
        _peer_scores_kernel,
        grid=(rows // tm,),
        in_specs=[pl.BlockSpec((tm, D_MODEL), lambda i: (i, 0)),
                  pl.BlockSpec(wq.shape, lambda i: (0, 0)),
                  pl.BlockSpec(keys.shape, lambda i: (0, 0, 0))],
        out_specs=[pl.BlockSpec((2 * PEER_HEADS, tm // LANES, N_KEYS, LANES), lambda i: (0, i, 0, 0)),
                   pl.BlockSpec((PEER_HEADS, SUBLANES, tm), lambda i: (0, 0, i))],
        out_shape=[jax.ShapeDtypeStruct((2 * PEER_HEADS, rows // LANES, N_KEYS, LANES), F32),
                   jax.ShapeDtypeStruct((PEER_HEADS, SUBLANES, rows), F32)],
        compiler_params=_cparams(("parallel",)),
        name="peer_scores",
    )(x, wq, keys)


def _gelu(x):
    return 0.5 * x * (1.0 + lax.erf(x * math.sqrt(0.5)))


def _peer_dense_kernel(x_ref, st_ref, stats_ref, u_ref, vt_ref, g_ref, b_ref, y_ref,
                       xt, e1, e2, rowb, hid, wt, acc):
    c = pl.program_id(1)
    rows_per_chunk = EXPERT_CHUNK // N_KEYS
    ncol = xt.shape[1] // LANES
    col_lanes = [slice(t * LANES, (t + 1) * LANES) for t in range(ncol)]

    @pl.when(c == 0)
    def _():
        xt[...] = x_ref[...].T.astype(BF16)
        for h in range(PEER_HEADS):
            for t, lanes in enumerate(col_lanes):
                m1, m2, inv_z = stats_ref[h, 1:2, lanes], stats_ref[h, 2:3, lanes], stats_ref[h, 3:4, lanes]
                e1[h, t] = jnp.exp(st_ref[2 * h, t] - m1) * inv_z
                e2[h, t] = jnp.exp(st_ref[2 * h + 1, t] - m2)
        acc[...] = jnp.zeros_like(acc)

    res = jnp.dot(u_ref[...], xt[...], preferred_element_type=F32)
    for t, lanes in enumerate(col_lanes):
        hid[t] = res[:, lanes]
    assert rows_per_chunk == SUBLANES
    group = pl.ds(pl.multiple_of(c * SUBLANES, SUBLANES), SUBLANES)
    for h in range(PEER_HEADS):
        for t in range(ncol):
            s1_rows, g1_rows = st_ref[2 * h, t, group, :], e1[h, t, group, :]
            for r in range(SUBLANES):
                rowb[r, h, t] = jnp.broadcast_to(s1_rows[r:r + 1, :], s1_rows.shape)
                rowb[r, PEER_HEADS + h, t] = jnp.broadcast_to(g1_rows[r:r + 1, :], g1_rows.shape)

    def expert_row(il, carry):
        base = pl.multiple_of(il * N_KEYS, N_KEYS)
        for t, lanes in enumerate(col_lanes):
            gate = [None] * (N_KEYS // SUBLANES)
            for h in range(PEER_HEADS):
                s1 = rowb[il, h, t]
                g1 = rowb[il, PEER_HEADS + h, t]
                tau = jnp.broadcast_to(stats_ref[h, 0:1, lanes], (SUBLANES, LANES))
                for k in range(N_KEYS // SUBLANES):
                    keys = slice(k * SUBLANES, (k + 1) * SUBLANES)
                    s = st_ref[2 * h + 1, t, keys, :] + s1
                    g = jnp.where(s >= tau, e2[h, t, keys, :] * g1, 0.0)
                    gate[k] = g if gate[k] is None else gate[k] + g
            for k in range(0, N_KEYS // SUBLANES, 2):
                w = [_gelu(hid[t, pl.ds(base + kk * SUBLANES, SUBLANES), :]) * gate[kk] for kk in (k, k + 1)]
                wt[pl.ds(base + k * SUBLANES, 2 * SUBLANES), lanes] = jnp.concatenate(w, axis=0).astype(BF16)
        return carry

    lax.fori_loop(0, rows_per_chunk, expert_row, 0)
    acc[...] += jnp.dot(vt_ref[...], wt[...], preferred_element_type=F32)

    @pl.when(c == pl.num_programs(1) - 1)
    def _():
        y_ref[...] = _layer_norm(ALPHA * x_ref[...] + acc[...].T, g_ref[...], b_ref[...])


def _peer_dense(x, st, stats, u, vt, ln_g, ln_b):
    rows = x.shape[0]
    tm = min(TOK_TILE, rows)
    assert rows % tm == 0
    n_exp = u.shape[0]
    assert n_exp % EXPERT_CHUNK == 0
    return pl.pallas_call(
        _peer_dense_kernel,
        grid=(rows // tm, n_exp // EXPERT_CHUNK),
        in_specs=[pl.BlockSpec((tm, D_MODEL), lambda t, c: (t, 0)),
                  pl.BlockSpec((2 * PEER_HEADS, tm // LANES, N_KEYS, LANES), lambda t, c: (0, t, 0, 0)),
                  pl.BlockSpec((PEER_HEADS, SUBLANES, tm), lambda t, c: (0, 0, t)),
                  pl.BlockSpec((EXPERT_CHUNK, D_MODEL), lambda t, c: (c, 0)),
                  pl.BlockSpec((D_MODEL, EXPERT_CHUNK), lambda t, c: (0, c)),
                  pl.BlockSpec((1, D_MODEL), lambda t, c: (0, 0)),
                  pl.BlockSpec((1, D_MODEL), lambda t, c: (0, 0))],
        out_specs=pl.BlockSpec((tm, D_MODEL), lambda t, c: (t, 0)),
        out_shape=jax.ShapeDtypeStruct((rows, D_MODEL), F32),
        scratch_shapes=[pltpu.VMEM((D_MODEL, tm), BF16),
                        pltpu.VMEM((PEER_HEADS, tm // LANES, N_KEYS, LANES), F32),
                        pltpu.VMEM((PEER_HEADS, tm // LANES, N_KEYS, LANES), F32),
                        pltpu.VMEM((SUBLANES, 2 * PEER_HEADS, tm // LANES, SUBLANES, LANES), F32),
                        pltpu.VMEM((tm // LANES, EXPERT_CHUNK, LANES), F32),
                        pltpu.VMEM((EXPERT_CHUNK, tm), BF16),
                        pltpu.VMEM((D_MODEL, tm), F32)],
        compiler_params=_cparams(("parallel", "arbitrary")),
        name="peer_dense",
    )(x, st, stats, u, vt, ln_g, ln_b)


def _rel_bucket_np(dist):
    n = np.maximum(dist, 0)
    max_exact = NUM_BUCKETS // 2
    large = max_exact + (np.log(np.maximum(n, max_exact).astype(np.float32) / max_exact)
                         / math.log(MAX_DISTANCE / max_exact) * (NUM_BUCKETS - max_exact)).astype(np.int32)
    return np.where(n < max_exact, n, np.minimum(large, NUM_BUCKETS - 1))


def _bias_kernel(table_ref, bucket_ref, out_ref, *, head_stride):
    head = pl.program_id(0) * head_stride + pl.program_id(1)
    bucket = bucket_ref[0]
    acc = jnp.full(bucket.shape, NEG, F32)
    for b in range(NUM_BUCKETS):
        acc = jnp.where(bucket == b, table_ref[b, head], acc)
    out_ref[0, 0] = acc


def _bias_tiles(table, bucket, n_outer, n_inner, head_stride):
    r, c = bucket.shape[1:]
    bmap = (lambda a, b: (a, 0, 0)) if bucket.shape[0] > 1 else (lambda a, b: (0, 0, 0))
    return pl.pallas_call(
        functools.partial(_bias_kernel, head_stride=head_stride),
        grid=(n_outer, n_inner),
        in_specs=[pl.BlockSpec(memory_space=pltpu.SMEM), pl.BlockSpec((1, r, c), bmap)],
        out_specs=pl.BlockSpec((1, 1, r, c), lambda a, b: (a, b, 0, 0)),
        out_shape=jax.ShapeDtypeStruct((n_outer, n_inner, r, c), F32),
        compiler_params=_cparams(("parallel", "parallel")),
        name="bias_tiles",
    )(table, jnp.asarray(bucket, jnp.int32))


def _prompt_bias(table):
    qi = np.arange(ATTN_BLOCK)[:, None]
    kj = np.arange(2 * ATTN_BLOCK)[None, :]
    dist = ATTN_BLOCK + qi - kj
    ok = (dist >= 0) & (dist < WINDOW)
    ok = np.stack([ok, ok & (kj >= ATTN_BLOCK - N_META)])
    bucket = np.where(ok, _rel_bucket_np(dist)[None], -1)
    return _bias_tiles(table, bucket, 2, N_HEADS, 0)


def _sample_bias(table, s_new):
    s, t = np.meshgrid(np.arange(SAMPLE_SEQS), np.arange(s_new), indexing="ij")
    s, t = s.reshape(-1, 1), t.reshape(-1, 1)
    out = []
    for cols, offset in ((WINDOW, WINDOW), (s_new, 0)):
        s2, c = np.meshgrid(np.arange(SAMPLE_SEQS), np.arange(cols), indexing="ij")
        s2, c = s2.reshape(1, -1), c.reshape(1, -1)
        dist = offset + t - c + 0 * s2
        ok = (s == s2) & (dist >= 0) & (dist < WINDOW)
        bucket = np.where(ok, _rel_bucket_np(dist), -1)[None]
        tiles = _bias_tiles(table, bucket, N_KV_HEADS, GROUP, GROUP)
        out.append(tiles.reshape(N_KV_HEADS, GROUP * bucket.shape[1], bucket.shape[2]))
    return out


def kernel(x_prompt, x_sample, cache_k, cache_v, state_pool, meta_tokens, rel_bias_table, w_in,
           w_pool_mix, pool_scale, attn_sinks, w_up_pool, w_up_attn, w_out, ln1_g, ln1_b,
           peer_w_query, peer_sub_keys, peer_u, peer_v, ln2_g, ln2_b):
    batch, seq, d = x_prompt.shape
    nb, s_new, _ = x_sample.shape
    assert w_in.shape[0] == DEPTH and d == D_MODEL and seq % ATTN_TILE == 0
    assert N_META >= max(POOL_WINDOWS) - 1 and cache_k.shape[2] == WINDOW

    w_in0 = w_in[0].astype(BF16)
    w_pqkv, w_gates = w_in0[:, :OFF_GA], w_in0[:, OFF_GA:]
    wmix = w_pool_mix[0].astype(BF16)
    scale = pool_scale[0].reshape(1, POOL_WIDTH)
    wup, wua, wout = w_up_pool[0].astype(BF16), w_up_attn[0].astype(BF16), w_out[0].astype(BF16)
    g1, b1 = ln1_g[0].reshape(1, d), ln1_b[0].reshape(1, d)
    g2, b2 = ln2_g[0].reshape(1, d), ln2_b[0].reshape(1, d)
    wq = peer_w_query[0].astype(BF16)
    keys = peer_sub_keys[0].reshape(2 * PEER_HEADS, N_KEYS, PEER_HALF).astype(BF16)
    u = peer_u[0].astype(BF16)
    vt = peer_v[0].astype(BF16).T
    sinks = attn_sinks[0].astype(F32)
    table = rel_bias_table.astype(F32)
    bias_p = _prompt_bias(table)
    bias_c, bias_n = _sample_bias(table, s_new)

    xp = x_prompt.reshape(batch * seq, d)
    xs = x_sample.reshape(nb * s_new, d)
    xm = jnp.concatenate([jnp.zeros((ATTN_BLOCK - N_META, d), F32), meta_tokens.astype(F32)], axis=0)

    p_p, q_p, k_p, v_p = _inproj(xp, w_pqkv)
    p_s, q_s, k_s, v_s = _inproj(xs, w_pqkv)
    p_m, _, k_m, v_m = _inproj(xm, w_pqkv)

    pooled_p, attn_p = _prompt_mix(sinks, q_p, k_p, v_p, p_p, k_m, v_m, p_m[ATTN_BLOCK - N_META:],
                                   bias_p, batch, seq)
    ck = cache_k[0].reshape(nb, WINDOW, KV_DIM)
    cv = cache_v[0].reshape(nb, WINDOW, KV_DIM)
    pooled_s, attn_s, newk, newv, newp = _sample_mix(sinks, q_s, k_s, v_s, p_s, ck, cv,
                                                     state_pool[0], bias_c, bias_n, s_new)

    outs = []
    for x, pooled, attn in ((xp, pooled_p, attn_p), (xs, pooled_s, attn_s)):
        x1 = _merge(x, pooled, attn, w_gates, wmix, scale, wup, wua, wout, g1, b1)
        st, stats = _peer_scores(x1, wq, keys)
        outs.append(_peer_dense(x1, st, stats, u, vt, g2, b2))

    w_keep = min(WINDOW, seq + N_META)
    kv_shape = (batch, seq, N_KV_HEADS, HEAD_DIM)
    return (outs[0].reshape(batch, seq, d),
            outs[1].reshape(nb, s_new, d),
            k_p.reshape(kv_shape)[None, :, seq - w_keep:],
            v_p.reshape(kv_shape)[None, :, seq - w_keep:],
            p_p.reshape(batch, seq, POOL_WIDTH)[None, :, seq - POOL_STATE:],
            newk.reshape(1, nb, WINDOW, N_KV_HEADS, HEAD_DIM),
            newv.reshape(1, nb, WINDOW, N_KV_HEADS, HEAD_DIM),
            newp[None])
```

```python
import functools
import math

import jax
import jax.numpy as jnp
import numpy as np
from jax import lax
from jax.experimental import pallas as pl
from jax.experimental.pallas import tpu as pltpu

F32 = jnp.float32
BF16 = jnp.bfloat16

D_MODEL = 1024
N_META = 16
POOL_WIDTH = 512
POOL_WINDOWS = (2, 4, 8, 16)
POOL_GROUP_DIM = 128
POOL_STATE = 15
HEAD_DIM = 64
N_HEADS = 8
N_KV_HEADS = 2
GROUP = N_HEADS // N_KV_HEADS
WINDOW = 128
ATTN_BLOCK = 128
ATTN_SCALE = HEAD_DIM ** -0.5
NUM_BUCKETS = 32
MAX_DISTANCE = 128
Q_DIM = N_HEADS * HEAD_DIM
KV_DIM = N_KV_HEADS * HEAD_DIM
OFF_Q = POOL_WIDTH
OFF_K = OFF_Q + Q_DIM
OFF_V = OFF_K + KV_DIM
OFF_GA = OFF_V + KV_DIM
PEER_HEADS = 8
N_KEYS = 128
PEER_TOPK = 16
PEER_HALF = 128
DEPTH = 1
ALPHA = (2 * DEPTH) ** 0.25
LN_EPS = 1e-5
NEG = -1e30

LANES = 128
SUBLANES = 8
VMEM_LIMIT = 56 * 1024 * 1024

TOK_TILE = 512
ATTN_TILE = 512
SAMPLE_SEQS = 8
EXPERT_CHUNK = 1024


def _cparams(sem):
    return pltpu.CompilerParams(dimension_semantics=sem, vmem_limit_bytes=VMEM_LIMIT)


def _inproj_kernel(x_ref, w_ref, p_ref, q_ref, k_ref, v_ref):
    z = jnp.dot(x_ref[...].astype(BF16), w_ref[...], preferred_element_type=F32)
    p_ref[...] = z[:, :OFF_Q]
    q_ref[...] = (z[:, OFF_Q:OFF_K] * ATTN_SCALE).astype(BF16)
    k_ref[...] = z[:, OFF_K:OFF_V]
    v_ref[...] = z[:, OFF_V:OFF_GA]


def _inproj(x, w_pqkv):
    rows = x.shape[0]
    tm = min(TOK_TILE, rows)
    assert rows % tm == 0
    row = lambda c: pl.BlockSpec((tm, c), lambda i: (i, 0))
    return pl.pallas_call(
        _inproj_kernel,
        grid=(rows // tm,),
        in_specs=[row(D_MODEL), pl.BlockSpec((D_MODEL, OFF_GA), lambda i: (0, 0))],
        out_specs=[row(POOL_WIDTH), row(Q_DIM), row(KV_DIM), row(KV_DIM)],
        out_shape=[jax.ShapeDtypeStruct((rows, POOL_WIDTH), F32),
                   jax.ShapeDtypeStruct((rows, Q_DIM), BF16),
                   jax.ShapeDtypeStruct((rows, KV_DIM), F32),
                   jax.ShapeDtypeStruct((rows, KV_DIM), F32)],
        compiler_params=_cparams(("parallel",)),
        name="inproj",
    )(x, w_pqkv)


def _window_pool(ext_ref, first, rows, out_ref):
    for g, w in enumerate(POOL_WINDOWS):
        cols = slice(g * POOL_GROUP_DIM, (g + 1) * POOL_GROUP_DIM)
        cur = ext_ref[first:first + rows, cols]
        acc = cur
        for r in range(1, w):
            acc = acc + ext_ref[first - r:first - r + rows, cols]
        out_ref[:, cols] = (acc * (1.0 / w) - cur).astype(out_ref.dtype)


def _sink_softmax(s, sink):
    m = jnp.maximum(jnp.max(s, axis=-1, keepdims=True), sink)
    e = jnp.exp(s - m)
    denom = jnp.sum(e, axis=-1, keepdims=True) + jnp.exp(sink - m)
    return e * (1.0 / denom)


_NT = (((1,), (1,)), ((), ()))


def _prompt_mix_kernel(sink_ref, q_ref, k_ref, v_ref, p_ref, kprev_ref, vprev_ref, pprev_ref,
                       mk_ref, mv_ref, mp_ref, bias_ref, pooled_ref, attn_ref,
                       kbuf, vbuf, pbuf):
    first = pl.program_id(1) == 0
    hist = ATTN_BLOCK
    kbuf[0:hist, :] = jnp.where(first, mk_ref[...], kprev_ref[...]).astype(BF16)
    vbuf[0:hist, :] = jnp.where(first, mv_ref[...], vprev_ref[...]).astype(BF16)
    kbuf[hist:, :] = k_ref[...].astype(BF16)
    vbuf[hist:, :] = v_ref[...].astype(BF16)
    pbuf[0:N_META, :] = jnp.where(first, mp_ref[...], pprev_ref[...])
    pbuf[N_META:, :] = p_ref[...]

    _window_pool(pbuf, N_META, ATTN_TILE, pooled_ref)

    first_i = jnp.where(first, 1, 0)
    for j in range(ATTN_TILE // ATTN_BLOCK):
        rows = slice(j * ATTN_BLOCK, (j + 1) * ATTN_BLOCK)
        keys = slice(j * ATTN_BLOCK, j * ATTN_BLOCK + 2 * ATTN_BLOCK)
        for kh in range(N_KV_HEADS):
            kk = kbuf[keys, kh * HEAD_DIM:(kh + 1) * HEAD_DIM]
            vv = vbuf[keys, kh * HEAD_DIM:(kh + 1) * HEAD_DIM]
            for g in range(GROUP):
                h = kh * GROUP + g
                hc = slice(h * HEAD_DIM, (h + 1) * HEAD_DIM)
                s = lax.dot_general(q_ref[rows, hc], kk, _NT, preferred_element_type=F32)
                bias = bias_ref[first_i, h] if j == 0 else bias_ref[0, h]
                prob = _sink_softmax(s + bias, sink_ref[h])
                o = jnp.dot(prob.astype(BF16), vv, preferred_element_type=F32)
                attn_ref[rows, hc] = o.astype(attn_ref.dtype)


def _prompt_mix(sinks, q, k, v, p, mk, mv, mp, bias, batch, seq):
    nt = seq // ATTN_TILE
    blocks_per_tile = ATTN_TILE // ATTN_BLOCK
    cur = lambda c: pl.BlockSpec((ATTN_TILE, c), lambda b, s: (b * nt + s, 0))
    prev_blk = lambda b, s: (jnp.maximum((b * nt + s) * blocks_per_tile - 1, 0), 0)
    prev_p = lambda b, s: (jnp.maximum((b * nt + s) * (ATTN_TILE // N_META) - 1, 0), 0)
    const2 = lambda b, s: (0, 0)
    rows = batch * seq
    return pl.pallas_call(
        _prompt_mix_kernel,
        grid=(batch, nt),
        in_specs=[pl.BlockSpec(memory_space=pltpu.SMEM),
                  cur(Q_DIM), cur(KV_DIM), cur(KV_DIM), cur(POOL_WIDTH),
                  pl.BlockSpec((ATTN_BLOCK, KV_DIM), prev_blk),
                  pl.BlockSpec((ATTN_BLOCK, KV_DIM), prev_blk),
                  pl.BlockSpec((N_META, POOL_WIDTH), prev_p),
                  pl.BlockSpec((ATTN_BLOCK, KV_DIM), const2),
                  pl.BlockSpec((ATTN_BLOCK, KV_DIM), const2),
                  pl.BlockSpec((N_META, POOL_WIDTH), const2),
                  pl.BlockSpec((2, N_HEADS, ATTN_BLOCK, 2 * ATTN_BLOCK), lambda b, s: (0, 0, 0, 0))],
        out_specs=[cur(POOL_WIDTH), cur(Q_DIM)],
        out_shape=[jax.ShapeDtypeStruct((rows, POOL_WIDTH), BF16),
                   jax.ShapeDtypeStruct((rows, Q_DIM), BF16)],
        scratch_shapes=[pltpu.VMEM((ATTN_BLOCK + ATTN_TILE, KV_DIM), BF16),
                        pltpu.VMEM((ATTN_BLOCK + ATTN_TILE, KV_DIM), BF16),
                        pltpu.VMEM((N_META + ATTN_TILE, POOL_WIDTH), F32)],
        compiler_params=_cparams(("parallel", "parallel")),
        name="prompt_mix",
    )(sinks, q, k, v, p, k, v, p, mk, mv, mp, bias)


def _sample_mix_kernel(sink_ref, q_ref, k_ref, v_ref, p_ref, ck_ref, cv_ref, st_ref,
                       biasc_ref, biasn_ref,
                       pooled_ref, attn_ref, newk_ref, newv_ref, newp_ref, ext, pooled_buf):
    nseq, w_cache, s_new = SAMPLE_SEQS, WINDOW, q_ref.shape[0] // SAMPLE_SEQS
    keep = POOL_STATE - s_new
    for i in range(nseq):
        new = slice(i * s_new, (i + 1) * s_new)
        ext[1:1 + POOL_STATE, :] = st_ref[i]
        ext[1 + POOL_STATE:1 + POOL_STATE + s_new, :] = p_ref[new, :]
        _window_pool(ext, 1 + POOL_STATE, s_new, pooled_buf)
        pooled_ref[new, :] = pooled_buf[...].astype(pooled_ref.dtype)
        newp_ref[i, 0:keep, :] = st_ref[i, s_new:POOL_STATE, :]
        newp_ref[i, keep:POOL_STATE, :] = p_ref[new, :]
        newk_ref[i, 0:w_cache - s_new, :] = ck_ref[i, s_new:w_cache, :]
        newk_ref[i, w_cache - s_new:w_cache, :] = k_ref[new, :]
        newv_ref[i, 0:w_cache - s_new, :] = cv_ref[i, s_new:w_cache, :]
        newv_ref[i, w_cache - s_new:w_cache, :] = v_ref[new, :]

    kc = ck_ref[...].reshape(nseq * w_cache, KV_DIM).astype(BF16)
    vc = cv_ref[...].reshape(nseq * w_cache, KV_DIM).astype(BF16)
    kn = k_ref[...].astype(BF16)
    vn = v_ref[...].astype(BF16)
    q = q_ref[...]
    nq = q.shape[0]
    for kh in range(N_KV_HEADS):
        kvc = slice(kh * HEAD_DIM, (kh + 1) * HEAD_DIM)
        qg = jnp.concatenate([q[:, (kh * GROUP + g) * HEAD_DIM:(kh * GROUP + g + 1) * HEAD_DIM]
                              for g in range(GROUP)], axis=0)
        sc = lax.dot_general(qg, kc[:, kvc], _NT, preferred_element_type=F32) + biasc_ref[kh]
        sn = lax.dot_general(qg, kn[:, kvc], _NT, preferred_element_type=F32) + biasn_ref[kh]
        sink = jnp.concatenate([jnp.full((nq, 1), sink_ref[kh * GROUP + g], F32) for g in range(GROUP)], axis=0)
        m = jnp.maximum(jnp.maximum(jnp.max(sc, axis=-1, keepdims=True),
                                    jnp.max(sn, axis=-1, keepdims=True)), sink)
        ec = jnp.exp(sc - m)
        en = jnp.exp(sn - m)
        denom = (jnp.sum(ec, axis=-1, keepdims=True) + jnp.sum(en, axis=-1, keepdims=True)
                 + jnp.exp(sink - m))
        inv = 1.0 / denom
        o = (jnp.dot((ec * inv).astype(BF16), vc[:, kvc], preferred_element_type=F32)
             + jnp.dot((en * inv).astype(BF16), vn[:, kvc], preferred_element_type=F32))
        for g in range(GROUP):
            h = kh * GROUP + g
            attn_ref[:, h * HEAD_DIM:(h + 1) * HEAD_DIM] = o[g * nq:(g + 1) * nq].astype(attn_ref.dtype)


def _sample_mix(sinks, q, k, v, p, cache_k, cache_v, state, biasc, biasn, s_new):
    nb = cache_k.shape[0]
    rows = SAMPLE_SEQS * s_new
    tok = lambda c: pl.BlockSpec((rows, c), lambda i: (i, 0))
    seq3 = lambda r, c: pl.BlockSpec((SAMPLE_SEQS, r, c), lambda i: (i, 0, 0))
    const3 = lambda a: pl.BlockSpec(a.shape, lambda i: (0, 0, 0))
    return pl.pallas_call(
        _sample_mix_kernel,
        grid=(nb // SAMPLE_SEQS,),
        in_specs=[pl.BlockSpec(memory_space=pltpu.SMEM),
                  tok(Q_DIM), tok(KV_DIM), tok(KV_DIM), tok(POOL_WIDTH),
                  seq3(WINDOW, KV_DIM), seq3(WINDOW, KV_DIM), seq3(POOL_STATE, POOL_WIDTH),
                  const3(biasc), const3(biasn)],
        out_specs=[tok(POOL_WIDTH), tok(Q_DIM),
                   seq3(WINDOW, KV_DIM), seq3(WINDOW, KV_DIM), seq3(POOL_STATE, POOL_WIDTH)],
        out_shape=[jax.ShapeDtypeStruct((nb * s_new, POOL_WIDTH), F32),
                   jax.ShapeDtypeStruct((nb * s_new, Q_DIM), F32),
                   jax.ShapeDtypeStruct((nb, WINDOW, KV_DIM), F32),
                   jax.ShapeDtypeStruct((nb, WINDOW, KV_DIM), F32),
                   jax.ShapeDtypeStruct((nb, POOL_STATE, POOL_WIDTH), F32)],
        scratch_shapes=[pltpu.VMEM((1 + POOL_STATE + SUBLANES, POOL_WIDTH), F32),
                        pltpu.VMEM((s_new, POOL_WIDTH), F32)],
        compiler_params=_cparams(("parallel",)),
        name="sample_mix",
    )(sinks, q, k, v, p, cache_k, cache_v, state, biasc, biasn)


def _layer_norm(x, g, b):
    mu = jnp.mean(x, axis=-1, keepdims=True)
    xc = x - mu
    var = jnp.mean(xc * xc, axis=-1, keepdims=True)
    return xc * lax.rsqrt(var + LN_EPS) * g + b


def _merge_kernel(x_ref, pooled_ref, attn_ref, wg_ref, wmix_ref, scale_ref, wup_ref, wua_ref,
                  wout_ref, g_ref, b_ref, y_ref):
    x = x_ref[...]
    glog = jnp.dot(x.astype(BF16), wg_ref[...], preferred_element_type=F32)
    g_pool = jax.nn.sigmoid(glog[:, :D_MODEL])
    g_attn = jax.nn.sigmoid(glog[:, D_MODEL:])
    pooled = pooled_ref[...].astype(BF16)
    mixed = jnp.concatenate(
        [jnp.dot(pooled[:, g * POOL_GROUP_DIM:(g + 1) * POOL_GROUP_DIM], wmix_ref[g],
                 preferred_element_type=F32) for g in range(len(POOL_WINDOWS))], axis=1)
    pool_out = (mixed * scale_ref[...]).astype(BF16)
    a = jnp.dot(pool_out, wup_ref[...], preferred_element_type=F32)
    b = jnp.dot(attn_ref[...].astype(BF16), wua_ref[...], preferred_element_type=F32)
    m = (g_pool * a + g_attn * b).astype(BF16)
    r = jnp.dot(m, wout_ref[...], preferred_element_type=F32)
    y_ref[...] = _layer_norm(ALPHA * x + r, g_ref[...], b_ref[...])


def _merge(x, pooled, attn, wg, wmix, scale, wup, wua, wout, ln_g, ln_b):
    rows = x.shape[0]
    tm = min(TOK_TILE, rows)
    assert rows % tm == 0
    row = lambda c: pl.BlockSpec((tm, c), lambda i: (i, 0))
    full = lambda a: pl.BlockSpec(a.shape, lambda i: (0,) * a.ndim)
    weights = (wg, wmix, scale, wup, wua, wout, ln_g, ln_b)
    return pl.pallas_call(
        _merge_kernel,
        grid=(rows // tm,),
        in_specs=[row(D_MODEL), row(POOL_WIDTH), row(Q_DIM)] + [full(a) for a in weights],
        out_specs=row(D_MODEL),
        out_shape=jax.ShapeDtypeStruct((rows, D_MODEL), F32),
        compiler_params=_cparams(("parallel",)),
        name="merge",
    )(x, pooled, attn, *weights)


def _oddeven_merge_sort_pairs(n):
    pairs = []
    p = 1
    while p < n:
        k = p
        while k >= 1:
            for j in range(k % p, n - k, 2 * k):
                for i in range(min(k, n - j - k)):
                    if (i + j) // (2 * p) == (i + j + k) // (2 * p):
                        pairs.append((i + j, i + j + k))
            k //= 2
        p *= 2
    return pairs


def _bitonic_merge_pairs(n):
    pairs = []
    k = n // 2
    while k >= 1:
        pairs += [(i, i + k) for i in range(n) if not i & k]
        k //= 2
    return pairs


def _apply_network(vals, pairs):
    vals = list(vals)
    for i, j in pairs:
        a, b = vals[i], vals[j]
        if b is None:
            continue
        if a is None:
            vals[i], vals[j] = b, None
        else:
            vals[i], vals[j] = jnp.maximum(a, b), jnp.minimum(a, b)
    return vals


_SORT16 = _oddeven_merge_sort_pairs(PEER_TOPK)
_MERGE16 = _bitonic_merge_pairs(PEER_TOPK)
_SORT64 = _oddeven_merge_sort_pairs(64)
_CANDIDATES = [(a, b) for a in range(PEER_TOPK) for b in range(PEER_TOPK) if (a + 1) * (b + 1) <= PEER_TOPK]


def _top16_rows(sc):
    n = sc.shape[0] // SUBLANES
    assert n == PEER_TOPK
    x = _apply_network([sc[k * SUBLANES:(k + 1) * SUBLANES, :] for k in range(n)], _SORT16)
    for shift in (4, 2, 1):
        y = [jnp.maximum(x[k], pltpu.roll(x[n - 1 - k], shift, 0)) for k in range(n)]
        x = _apply_network(y, _MERGE16)
    return x


def _peer_scores_kernel(x_ref, wq_ref, keys_ref, cnt_ref, e1_ref, rank_ref, e2_ref):
    q = jnp.dot(x_ref[...].astype(BF16), wq_ref[...], preferred_element_type=F32).astype(BF16)
    for h in range(PEER_HEADS):
        sc, tops = [], []
        for c in range(2):
            col = (h * 2 + c) * PEER_HALF
            sc.append(lax.dot_general(keys_ref[h * 2 + c], q[:, col:col + PEER_HALF], _NT,
                                      preferred_element_type=F32))
            tops.append(_top16_rows(sc[c]))
        v1, v2 = tops
        cand = {(a, b): v1[a] + v2[b] for a, b in _CANDIDATES}
        best = _apply_network(list(cand.values()) + [None] * (64 - len(cand)), _SORT64)[:PEER_TOPK]
        tau = best[PEER_TOPK - 1]
        z = jnp.ones_like(tau)
        for r in range(1, PEER_TOPK):
            z = z + jnp.exp(best[r] - best[0])
        inv_z = 1.0 / z
        cnt_of_rank = []
        for a in range(PEER_TOPK):
            n = jnp.zeros_like(tau)
            for b in range(PEER_TOPK):
                if (a, b) in cand:
                    n = n + jnp.where(cand[(a, b)] >= tau, 1.0, 0.0)
            cnt_of_rank.append(n)
        for t in range(tau.shape[1] // LANES):
            lanes = slice(t * LANES, (t + 1) * LANES)
            row = lambda r: r[0:1, lanes]
            s1, s2 = sc[0][:, lanes], sc[1][:, lanes]
            cnt = jnp.zeros_like(s1)
            rank = jnp.zeros_like(s2)
            for a in range(PEER_TOPK):
                cnt = jnp.where(s1 == row(v1[a]), row(cnt_of_rank[a]), cnt)
                rank = jnp.where(row(v2[a]) > s2, float(a + 1), rank)
            cnt_ref[h, t] = cnt
            e1_ref[h, t] = jnp.exp(s1 - row(v1[0])) * row(inv_z)
            rank_ref[h, t] = pltpu.bitcast(rank.astype(BF16), jnp.uint32)
            e2_ref[h, t] = pltpu.bitcast(jnp.exp(s2 - row(v2[0])).astype(BF16), jnp.uint32)


def _peer_scores(x, wq, keys):
    rows = x.shape[0]
    tm = min(TOK_TILE, rows)
    assert rows % tm == 0
    return pl.pallas_call(
        _peer_scores_kernel,
        grid=(rows // tm,),
        in_specs=[pl.BlockSpec((tm, D_MODEL), lambda i: (i, 0)),
                  pl.BlockSpec(wq.shape, lambda i: (0, 0)),
                  pl.BlockSpec(keys.shape, lambda i: (0, 0, 0))],
        out_specs=[pl.BlockSpec((PEER_HEADS, tm // LANES, n, LANES), lambda i: (0, i, 0, 0))
                   for n in (N_KEYS, N_KEYS, N_KEYS // 2, N_KEYS // 2)],
        out_shape=[jax.ShapeDtypeStruct((PEER_HEADS, rows // LANES, n, LANES), dt)
                   for n, dt in ((N_KEYS, F32), (N_KEYS, F32), (N_KEYS // 2, jnp.uint32), (N_KEYS // 2, jnp.uint32))],
        compiler_params=_cparams(("parallel",)),
        name="peer_scores",
    )(x, wq, keys)


def _gelu(x):
    return 0.5 * x * (1.0 + lax.erf(x * math.sqrt(0.5)))


def _peer_dense_kernel(x_ref, cnt_ref, e1_ref, rank_ref, e2_ref, u_ref, vt_ref, g_ref, b_ref, y_ref,
                       xt, rowb, hid, wt, acc):
    c = pl.program_id(1)
    rows_per_chunk = EXPERT_CHUNK // N_KEYS
    ncol = xt.shape[1] // LANES
    col_lanes = [slice(t * LANES, (t + 1) * LANES) for t in range(ncol)]
    packed = 2 * SUBLANES

    @pl.when(c == 0)
    def _():
        xt[...] = x_ref[...].T.astype(BF16)
        acc[...] = jnp.zeros_like(acc)

    res = jnp.dot(u_ref[...], xt[...], preferred_element_type=F32)
    for t, lanes in enumerate(col_lanes):
        hid[t] = res[:, lanes]
    assert rows_per_chunk == SUBLANES
    group = pl.ds(pl.multiple_of(c * SUBLANES, SUBLANES), SUBLANES)
    for h in range(PEER_HEADS):
        for t in range(ncol):
            cnt_rows, g1_rows = cnt_ref[h, t, group, :], e1_ref[h, t, group, :]
            for r in range(SUBLANES):
                rowb[r, h, t] = jnp.broadcast_to(cnt_rows[r:r + 1, :], (packed, LANES)).astype(BF16)
                rowb[r, PEER_HEADS + h, t] = jnp.broadcast_to(g1_rows[r:r + 1, :], (packed, LANES)).astype(BF16)

    def expert_row(il, carry):
        base = pl.multiple_of(il * N_KEYS, N_KEYS)
        for t, lanes in enumerate(col_lanes):
            gate = [None] * (N_KEYS // packed)
            for h in range(PEER_HEADS):
                cnt = rowb[il, h, t]
                g1 = rowb[il, PEER_HEADS + h, t]
                for k in range(N_KEYS // packed):
                    words = slice(k * SUBLANES, (k + 1) * SUBLANES)
                    rank = pltpu.bitcast(rank_ref[h, t, words, :], BF16)
                    g2 = pltpu.bitcast(e2_ref[h, t, words, :], BF16)
                    g = jnp.where(rank < cnt, g2 * g1, jnp.zeros((), BF16))
                    gate[k] = g if gate[k] is None else gate[k] + g
            for k in range(N_KEYS // packed):
                act = _gelu(hid[t, pl.ds(base + k * packed, packed), :]).astype(BF16)
                wt[pl.ds(base + k * packed, packed), lanes] = act * gate[k]
        return carry

    lax.fori_loop(0, rows_per_chunk, expert_row, 0)
    acc[...] += jnp.dot(vt_ref[...], wt[...], preferred_element_type=F32)

    @pl.when(c == pl.num_programs(1) - 1)
    def _():
        y_ref[...] = _layer_norm(ALPHA * x_ref[...] + acc[...].T, g_ref[...], b_ref[...])


def _peer_dense(x, cnt, e1, rank, e2, u, vt, ln_g, ln_b):
    rows = x.shape[0]
    tm = min(TOK_TILE, rows)
    assert rows % tm == 0
    n_exp = u.shape[0]
    assert n_exp % EXPERT_CHUNK == 0
    return pl.pallas_call(
        _peer_dense_kernel,
        grid=(rows // tm, n_exp // EXPERT_CHUNK),
        in_specs=[pl.BlockSpec((tm, D_MODEL), lambda t, c: (t, 0))]
                 + [pl.BlockSpec((PEER_HEADS, tm // LANES, n, LANES), lambda t, c: (0, t, 0, 0))
                    for n in (N_KEYS, N_KEYS, N_KEYS // 2, N_KEYS // 2)]
                 + [pl.BlockSpec((EXPERT_CHUNK, D_MODEL), lambda t, c: (c, 0)),
                  pl.BlockSpec((D_MODEL, EXPERT_CHUNK), lambda t, c: (0, c)),
                  pl.BlockSpec((1, D_MODEL), lambda t, c: (0, 0)),
                  pl.BlockSpec((1, D_MODEL), lambda t, c: (0, 0))],
        out_specs=pl.BlockSpec((tm, D_MODEL), lambda t, c: (t, 0)),
        out_shape=jax.ShapeDtypeStruct((rows, D_MODEL), F32),
        scratch_shapes=[pltpu.VMEM((D_MODEL, tm), BF16),
                        pltpu.VMEM((SUBLANES, 2 * PEER_HEADS, tm // LANES, 2 * SUBLANES, LANES), BF16),
                        pltpu.VMEM((tm // LANES, EXPERT_CHUNK, LANES), F32),
                        pltpu.VMEM((EXPERT_CHUNK, tm), BF16),
                        pltpu.VMEM((D_MODEL, tm), F32)],
        compiler_params=_cparams(("parallel", "arbitrary")),
        name="peer_dense",
    )(x, cnt, e1, rank, e2, u, vt, ln_g, ln_b)


def _rel_bucket_np(dist):
    n = np.maximum(dist, 0)
    max_exact = NUM_BUCKETS // 2
    large = max_exact + (np.log(np.maximum(n, max_exact).astype(np.float32) / max_exact)
                         / math.log(MAX_DISTANCE / max_exact) * (NUM_BUCKETS - max_exact)).astype(np.int32)
    return np.where(n < max_exact, n, np.minimum(large, NUM_BUCKETS - 1))


def _bias_kernel(table_ref, bucket_ref, out_ref, *, head_stride):
    head = pl.program_id(0) * head_stride + pl.program_id(1)
    bucket = bucket_ref[0]
    acc = jnp.full(bucket.shape, NEG, F32)
    for b in range(NUM_BUCKETS):
        acc = jnp.where(bucket == b, table_ref[b, head], acc)
    out_ref[0, 0] = acc


def _bias_tiles(table, bucket, n_outer, n_inner, head_stride):
    r, c = bucket.shape[1:]
    bmap = (lambda a, b: (a, 0, 0)) if bucket.shape[0] > 1 else (lambda a, b: (0, 0, 0))
    return pl.pallas_call(
        functools.partial(_bias_kernel, head_stride=head_stride),
        grid=(n_outer, n_inner),
        in_specs=[pl.BlockSpec(memory_space=pltpu.SMEM), pl.BlockSpec((1, r, c), bmap)],
        out_specs=pl.BlockSpec((1, 1, r, c), lambda a, b: (a, b, 0, 0)),
        out_shape=jax.ShapeDtypeStruct((n_outer, n_inner, r, c), F32),
        compiler_params=_cparams(("parallel", "parallel")),
        name="bias_tiles",
    )(table, jnp.asarray(bucket, jnp.int32))


def _prompt_bias(table):
    qi = np.arange(ATTN_BLOCK)[:, None]
    kj = np.arange(2 * ATTN_BLOCK)[None, :]
    dist = ATTN_BLOCK + qi - kj
    ok = (dist >= 0) & (dist < WINDOW)
    ok = np.stack([ok, ok & (kj >= ATTN_BLOCK - N_META)])
    bucket = np.where(ok, _rel_bucket_np(dist)[None], -1)
    return _bias_tiles(table, bucket, 2, N_HEADS, 0)


def _sample_bias(table, s_new):
    s, t = np.meshgrid(np.arange(SAMPLE_SEQS), np.arange(s_new), indexing="ij")
    s, t = s.reshape(-1, 1), t.reshape(-1, 1)
    out = []
    for cols, offset in ((WINDOW, WINDOW), (s_new, 0)):
        s2, c = np.meshgrid(np.arange(SAMPLE_SEQS), np.arange(cols), indexing="ij")
        s2, c = s2.reshape(1, -1), c.reshape(1, -1)
        dist = offset + t - c + 0 * s2
        ok = (s == s2) & (dist >= 0) & (dist < WINDOW)
        bucket = np.where(ok, _rel_bucket_np(dist), -1)[None]
        tiles = _bias_tiles(table, bucket, N_KV_HEADS, GROUP, GROUP)
        out.append(tiles.reshape(N_KV_HEADS, GROUP * bucket.shape[1], bucket.shape[2]))
    return out


def kernel(x_prompt, x_sample, cache_k, cache_v, state_pool, meta_tokens, rel_bias_table, w_in,
           w_pool_mix, pool_scale, attn_sinks, w_up_pool, w_up_attn, w_out, ln1_g, ln1_b,
           peer_w_query, peer_sub_keys, peer_u, peer_v, ln2_g, ln2_b):
    batch, seq, d = x_prompt.shape
    nb, s_new, _ = x_sample.shape
    assert w_in.shape[0] == DEPTH and d == D_MODEL and seq % ATTN_TILE == 0
    assert N_META >= max(POOL_WINDOWS) - 1 and cache_k.shape[2] == WINDOW

    w_in0 = w_in[0].astype(BF16)
    w_pqkv, w_gates = w_in0[:, :OFF_GA], w_in0[:, OFF_GA:]
    wmix = w_pool_mix[0].astype(BF16)
    scale = pool_scale[0].reshape(1, POOL_WIDTH)
    wup, wua, wout = w_up_pool[0].astype(BF16), w_up_attn[0].astype(BF16), w_out[0].astype(BF16)
    g1, b1 = ln1_g[0].reshape(1, d), ln1_b[0].reshape(1, d)
    g2, b2 = ln2_g[0].reshape(1, d), ln2_b[0].reshape(1, d)
    wq = peer_w_query[0].astype(BF16)
    keys = peer_sub_keys[0].reshape(2 * PEER_HEADS, N_KEYS, PEER_HALF).astype(BF16)
    u = peer_u[0].astype(BF16)
    vt = peer_v[0].astype(BF16).T
    sinks = attn_sinks[0].astype(F32)
    table = rel_bias_table.astype(F32)
    bias_p = _prompt_bias(table)
    bias_c, bias_n = _sample_bias(table, s_new)

    xp = x_prompt.reshape(batch * seq, d)
    xs = x_sample.reshape(nb * s_new, d)
    xm = jnp.concatenate([jnp.zeros((ATTN_BLOCK - N_META, d), F32), meta_tokens.astype(F32)], axis=0)

    p_p, q_p, k_p, v_p = _inproj(xp, w_pqkv)
    p_s, q_s, k_s, v_s = _inproj(xs, w_pqkv)
    p_m, _, k_m, v_m = _inproj(xm, w_pqkv)

    pooled_p, attn_p = _prompt_mix(sinks, q_p, k_p, v_p, p_p, k_m, v_m, p_m[ATTN_BLOCK - N_META:],
                                   bias_p, batch, seq)
    ck = cache_k[0].reshape(nb, WINDOW, KV_DIM)
    cv = cache_v[0].reshape(nb, WINDOW, KV_DIM)
    pooled_s, attn_s, newk, newv, newp = _sample_mix(sinks, q_s, k_s, v_s, p_s, ck, cv,
                                                     state_pool[0], bias_c, bias_n, s_new)

    outs = []
    for x, pooled, attn in ((xp, pooled_p, attn_p), (xs, pooled_s, attn_s)):
        x1 = _merge(x, pooled, attn, w_gates, wmix, scale, wup, wua, wout, g1, b1)
        cnt, e1, rank, e2 = _peer_scores(x1, wq, keys)
        outs.append(_peer_dense(x1, cnt, e1, rank, e2, u, vt, g2, b2))

    w_keep = min(WINDOW, seq + N_META)
    kv_shape = (batch, seq, N_KV_HEADS, HEAD_DIM)
    return (outs[0].reshape(batch, seq, d),
            outs[1].reshape(nb, s_new, d),
            k_p.reshape(kv_shape)[None, :, seq - w_keep:],
            v_p.reshape(kv_shape)[None, :, seq - w_keep:],
            p_p.reshape(batch, seq, POOL_WIDTH)[None, :, seq - POOL_STATE:],
            newk.reshape(1, nb, WINDOW, N_KV_HEADS, HEAD_DIM),
            newv.reshape(1, nb, WINDOW, N_KV_HEADS, HEAD_DIM),
            newp[None])
```

```python
import functools
import math

import jax
import jax.numpy as jnp
import numpy as np
from jax import lax
from jax.experimental import pallas as pl
from jax.experimental.pallas import tpu as pltpu

F32 = jnp.float32
BF16 = jnp.bfloat16

D_MODEL = 1024
N_META = 16
POOL_WIDTH = 512
POOL_WINDOWS = (2, 4, 8, 16)
POOL_GROUP_DIM = 128
POOL_STATE = 15
HEAD_DIM = 64
N_HEADS = 8
N_KV_HEADS = 2
GROUP = N_HEADS // N_KV_HEADS
WINDOW = 128
ATTN_BLOCK = 128
ATTN_SCALE = HEAD_DIM ** -0.5
NUM_BUCKETS = 32
MAX_DISTANCE = 128
Q_DIM = N_HEADS * HEAD_DIM
KV_DIM = N_KV_HEADS * HEAD_DIM
OFF_Q = POOL_WIDTH
OFF_K = OFF_Q + Q_DIM
OFF_V = OFF_K + KV_DIM
OFF_GA = OFF_V + KV_DIM
PEER_HEADS = 8
N_KEYS = 128
PEER_TOPK = 16
PEER_HALF = 128
DEPTH = 1
ALPHA = (2 * DEPTH) ** 0.25
LN_EPS = 1e-5
NEG = -1e30

LANES = 128
SUBLANES = 8
VMEM_LIMIT = 56 * 1024 * 1024

TOK_TILE = 512
ATTN_TILE = 512
SAMPLE_SEQS = 8
EXPERT_CHUNK = 1024
PEER_PIPE_STEPS = 4


def _cparams(sem, flags=None):
    return pltpu.CompilerParams(dimension_semantics=sem, vmem_limit_bytes=VMEM_LIMIT, flags=flags)


def _inproj_kernel(x_ref, w_ref, p_ref, q_ref, k_ref, v_ref):
    z = jnp.dot(x_ref[...].astype(BF16), w_ref[...], preferred_element_type=F32)
    p_ref[...] = z[:, :OFF_Q]
    q_ref[...] = (z[:, OFF_Q:OFF_K] * ATTN_SCALE).astype(BF16)
    k_ref[...] = z[:, OFF_K:OFF_V]
    v_ref[...] = z[:, OFF_V:OFF_GA]


def _inproj(x, w_pqkv):
    rows = x.shape[0]
    tm = min(TOK_TILE, rows)
    assert rows % tm == 0
    row = lambda c: pl.BlockSpec((tm, c), lambda i: (i, 0))
    return pl.pallas_call(
        _inproj_kernel,
        grid=(rows // tm,),
        in_specs=[row(D_MODEL), pl.BlockSpec((D_MODEL, OFF_GA), lambda i: (0, 0))],
        out_specs=[row(POOL_WIDTH), row(Q_DIM), row(KV_DIM), row(KV_DIM)],
        out_shape=[jax.ShapeDtypeStruct((rows, POOL_WIDTH), F32),
                   jax.ShapeDtypeStruct((rows, Q_DIM), BF16),
                   jax.ShapeDtypeStruct((rows, KV_DIM), F32),
                   jax.ShapeDtypeStruct((rows, KV_DIM), F32)],
        compiler_params=_cparams(("parallel",)),
        name="inproj",
    )(x, w_pqkv)


def _window_pool(ext_ref, first, rows, out_ref):
    for g, w in enumerate(POOL_WINDOWS):
        cols = slice(g * POOL_GROUP_DIM, (g + 1) * POOL_GROUP_DIM)
        cur = ext_ref[first:first + rows, cols]
        acc = cur
        for r in range(1, w):
            acc = acc + ext_ref[first - r:first - r + rows, cols]
        out_ref[:, cols] = (acc * (1.0 / w) - cur).astype(out_ref.dtype)


def _sink_softmax(s, sink):
    m = jnp.maximum(jnp.max(s, axis=-1, keepdims=True), sink)
    e = jnp.exp(s - m)
    denom = jnp.sum(e, axis=-1, keepdims=True) + jnp.exp(sink - m)
    return e * (1.0 / denom)


_NT = (((1,), (1,)), ((), ()))


def _prompt_mix_kernel(sink_ref, q_ref, k_ref, v_ref, p_ref, kprev_ref, vprev_ref, pprev_ref,
                       mk_ref, mv_ref, mp_ref, bias_ref, pooled_ref, attn_ref,
                       kbuf, vbuf, pbuf):
    first = pl.program_id(1) == 0
    hist = ATTN_BLOCK
    kbuf[0:hist, :] = jnp.where(first, mk_ref[...], kprev_ref[...]).astype(BF16)
    vbuf[0:hist, :] = jnp.where(first, mv_ref[...], vprev_ref[...]).astype(BF16)
    kbuf[hist:, :] = k_ref[...].astype(BF16)
    vbuf[hist:, :] = v_ref[...].astype(BF16)
    pbuf[0:N_META, :] = jnp.where(first, mp_ref[...], pprev_ref[...])
    pbuf[N_META:, :] = p_ref[...]

    _window_pool(pbuf, N_META, ATTN_TILE, pooled_ref)

    first_i = jnp.where(first, 1, 0)
    for j in range(ATTN_TILE // ATTN_BLOCK):
        rows = slice(j * ATTN_BLOCK, (j + 1) * ATTN_BLOCK)
        keys = slice(j * ATTN_BLOCK, j * ATTN_BLOCK + 2 * ATTN_BLOCK)
        for kh in range(N_KV_HEADS):
            kk = kbuf[keys, kh * HEAD_DIM:(kh + 1) * HEAD_DIM]
            vv = vbuf[keys, kh * HEAD_DIM:(kh + 1) * HEAD_DIM]
            for g in range(GROUP):
                h = kh * GROUP + g
                hc = slice(h * HEAD_DIM, (h + 1) * HEAD_DIM)
                s = lax.dot_general(q_ref[rows, hc], kk, _NT, preferred_element_type=F32)
                bias = bias_ref[first_i, h] if j == 0 else bias_ref[0, h]
                prob = _sink_softmax(s + bias, sink_ref[h])
                o = jnp.dot(prob.astype(BF16), vv, preferred_element_type=F32)
                attn_ref[rows, hc] = o.astype(attn_ref.dtype)


def _prompt_mix(sinks, q, k, v, p, mk, mv, mp, bias, batch, seq):
    nt = seq // ATTN_TILE
    blocks_per_tile = ATTN_TILE // ATTN_BLOCK
    cur = lambda c: pl.BlockSpec((ATTN_TILE, c), lambda b, s: (b * nt + s, 0))
    prev_blk = lambda b, s: (jnp.maximum((b * nt + s) * blocks_per_tile - 1, 0), 0)
    prev_p = lambda b, s: (jnp.maximum((b * nt + s) * (ATTN_TILE // N_META) - 1, 0), 0)
    const2 = lambda b, s: (0, 0)
    rows = batch * seq
    return pl.pallas_call(
        _prompt_mix_kernel,
        grid=(batch, nt),
        in_specs=[pl.BlockSpec(memory_space=pltpu.SMEM),
                  cur(Q_DIM), cur(KV_DIM), cur(KV_DIM), cur(POOL_WIDTH),
                  pl.BlockSpec((ATTN_BLOCK, KV_DIM), prev_blk),
                  pl.BlockSpec((ATTN_BLOCK, KV_DIM), prev_blk),
                  pl.BlockSpec((N_META, POOL_WIDTH), prev_p),
                  pl.BlockSpec((ATTN_BLOCK, KV_DIM), const2),
                  pl.BlockSpec((ATTN_BLOCK, KV_DIM), const2),
                  pl.BlockSpec((N_META, POOL_WIDTH), const2),
                  pl.BlockSpec((2, N_HEADS, ATTN_BLOCK, 2 * ATTN_BLOCK), lambda b, s: (0, 0, 0, 0))],
        out_specs=[cur(POOL_WIDTH), cur(Q_DIM)],
        out_shape=[jax.ShapeDtypeStruct((rows, POOL_WIDTH), BF16),
                   jax.ShapeDtypeStruct((rows, Q_DIM), BF16)],
        scratch_shapes=[pltpu.VMEM((ATTN_BLOCK + ATTN_TILE, KV_DIM), BF16),
                        pltpu.VMEM((ATTN_BLOCK + ATTN_TILE, KV_DIM), BF16),
                        pltpu.VMEM((N_META + ATTN_TILE, POOL_WIDTH), F32)],
        compiler_params=_cparams(("parallel", "parallel")),
        name="prompt_mix",
    )(sinks, q, k, v, p, k, v, p, mk, mv, mp, bias)


def _sample_mix_kernel(sink_ref, q_ref, k_ref, v_ref, p_ref, ck_ref, cv_ref, st_ref,
                       biasc_ref, biasn_ref,
                       pooled_ref, attn_ref, newk_ref, newv_ref, newp_ref, ext, pooled_buf):
    nseq, w_cache, s_new = SAMPLE_SEQS, WINDOW, q_ref.shape[0] // SAMPLE_SEQS
    keep = POOL_STATE - s_new
    for i in range(nseq):
        new = slice(i * s_new, (i + 1) * s_new)
        ext[1:1 + POOL_STATE, :] = st_ref[i]
        ext[1 + POOL_STATE:1 + POOL_STATE + s_new, :] = p_ref[new, :]
        _window_pool(ext, 1 + POOL_STATE, s_new, pooled_buf)
        pooled_ref[new, :] = pooled_buf[...].astype(pooled_ref.dtype)
        newp_ref[i, 0:keep, :] = st_ref[i, s_new:POOL_STATE, :]
        newp_ref[i, keep:POOL_STATE, :] = p_ref[new, :]
        newk_ref[i, 0:w_cache - s_new, :] = ck_ref[i, s_new:w_cache, :]
        newk_ref[i, w_cache - s_new:w_cache, :] = k_ref[new, :]
        newv_ref[i, 0:w_cache - s_new, :] = cv_ref[i, s_new:w_cache, :]
        newv_ref[i, w_cache - s_new:w_cache, :] = v_ref[new, :]

    kc = ck_ref[...].reshape(nseq * w_cache, KV_DIM).astype(BF16)
    vc = cv_ref[...].reshape(nseq * w_cache, KV_DIM).astype(BF16)
    kn = k_ref[...].astype(BF16)
    vn = v_ref[...].astype(BF16)
    q = q_ref[...]
    nq = q.shape[0]
    for kh in range(N_KV_HEADS):
        kvc = slice(kh * HEAD_DIM, (kh + 1) * HEAD_DIM)
        qg = jnp.concatenate([q[:, (kh * GROUP + g) * HEAD_DIM:(kh * GROUP + g + 1) * HEAD_DIM]
                              for g in range(GROUP)], axis=0)
        sc = lax.dot_general(qg, kc[:, kvc], _NT, preferred_element_type=F32) + biasc_ref[kh]
        sn = lax.dot_general(qg, kn[:, kvc], _NT, preferred_element_type=F32) + biasn_ref[kh]
        sink = jnp.concatenate([jnp.full((nq, 1), sink_ref[kh * GROUP + g], F32) for g in range(GROUP)], axis=0)
        m = jnp.maximum(jnp.maximum(jnp.max(sc, axis=-1, keepdims=True),
                                    jnp.max(sn, axis=-1, keepdims=True)), sink)
        ec = jnp.exp(sc - m)
        en = jnp.exp(sn - m)
        denom = (jnp.sum(ec, axis=-1, keepdims=True) + jnp.sum(en, axis=-1, keepdims=True)
                 + jnp.exp(sink - m))
        inv = 1.0 / denom
        o = (jnp.dot((ec * inv).astype(BF16), vc[:, kvc], preferred_element_type=F32)
             + jnp.dot((en * inv).astype(BF16), vn[:, kvc], preferred_element_type=F32))
        for g in range(GROUP):
            h = kh * GROUP + g
            attn_ref[:, h * HEAD_DIM:(h + 1) * HEAD_DIM] = o[g * nq:(g + 1) * nq].astype(attn_ref.dtype)


def _sample_mix(sinks, q, k, v, p, cache_k, cache_v, state, biasc, biasn, s_new):
    nb = cache_k.shape[0]
    rows = SAMPLE_SEQS * s_new
    tok = lambda c: pl.BlockSpec((rows, c), lambda i: (i, 0))
    seq3 = lambda r, c: pl.BlockSpec((SAMPLE_SEQS, r, c), lambda i: (i, 0, 0))
    const3 = lambda a: pl.BlockSpec(a.shape, lambda i: (0, 0, 0))
    return pl.pallas_call(
        _sample_mix_kernel,
        grid=(nb // SAMPLE_SEQS,),
        in_specs=[pl.BlockSpec(memory_space=pltpu.SMEM),
                  tok(Q_DIM), tok(KV_DIM), tok(KV_DIM), tok(POOL_WIDTH),
                  seq3(WINDOW, KV_DIM), seq3(WINDOW, KV_DIM), seq3(POOL_STATE, POOL_WIDTH),
                  const3(biasc), const3(biasn)],
        out_specs=[tok(POOL_WIDTH), tok(Q_DIM),
                   seq3(WINDOW, KV_DIM), seq3(WINDOW, KV_DIM), seq3(POOL_STATE, POOL_WIDTH)],
        out_shape=[jax.ShapeDtypeStruct((nb * s_new, POOL_WIDTH), F32),
                   jax.ShapeDtypeStruct((nb * s_new, Q_DIM), F32),
                   jax.ShapeDtypeStruct((nb, WINDOW, KV_DIM), F32),
                   jax.ShapeDtypeStruct((nb, WINDOW, KV_DIM), F32),
                   jax.ShapeDtypeStruct((nb, POOL_STATE, POOL_WIDTH), F32)],
        scratch_shapes=[pltpu.VMEM((1 + POOL_STATE + SUBLANES, POOL_WIDTH), F32),
                        pltpu.VMEM((s_new, POOL_WIDTH), F32)],
        compiler_params=_cparams(("parallel",)),
        name="sample_mix",
    )(sinks, q, k, v, p, cache_k, cache_v, state, biasc, biasn)


def _layer_norm(x, g, b):
    mu = jnp.mean(x, axis=-1, keepdims=True)
    xc = x - mu
    var = jnp.mean(xc * xc, axis=-1, keepdims=True)
    return xc * lax.rsqrt(var + LN_EPS) * g + b


def _merge_kernel(x_ref, pooled_ref, attn_ref, wg_ref, wmix_ref, scale_ref, wup_ref, wua_ref,
                  wout_ref, g_ref, b_ref, y_ref):
    x = x_ref[...]
    glog = jnp.dot(x.astype(BF16), wg_ref[...], preferred_element_type=F32)
    g_pool = jax.nn.sigmoid(glog[:, :D_MODEL])
    g_attn = jax.nn.sigmoid(glog[:, D_MODEL:])
    pooled = pooled_ref[...].astype(BF16)
    mixed = jnp.concatenate(
        [jnp.dot(pooled[:, g * POOL_GROUP_DIM:(g + 1) * POOL_GROUP_DIM], wmix_ref[g],
                 preferred_element_type=F32) for g in range(len(POOL_WINDOWS))], axis=1)
    pool_out = (mixed * scale_ref[...]).astype(BF16)
    a = jnp.dot(pool_out, wup_ref[...], preferred_element_type=F32)
    b = jnp.dot(attn_ref[...].astype(BF16), wua_ref[...], preferred_element_type=F32)
    m = (g_pool * a + g_attn * b).astype(BF16)
    r = jnp.dot(m, wout_ref[...], preferred_element_type=F32)
    y_ref[...] = _layer_norm(ALPHA * x + r, g_ref[...], b_ref[...])


def _merge(x, pooled, attn, wg, wmix, scale, wup, wua, wout, ln_g, ln_b):
    rows = x.shape[0]
    tm = min(TOK_TILE, rows)
    assert rows % tm == 0
    row = lambda c: pl.BlockSpec((tm, c), lambda i: (i, 0))
    full = lambda a: pl.BlockSpec(a.shape, lambda i: (0,) * a.ndim)
    weights = (wg, wmix, scale, wup, wua, wout, ln_g, ln_b)
    return pl.pallas_call(
        _merge_kernel,
        grid=(rows // tm,),
        in_specs=[row(D_MODEL), row(POOL_WIDTH), row(Q_DIM)] + [full(a) for a in weights],
        out_specs=row(D_MODEL),
        out_shape=jax.ShapeDtypeStruct((rows, D_MODEL), F32),
        compiler_params=_cparams(("parallel",)),
        name="merge",
    )(x, pooled, attn, *weights)


def _oddeven_merge_sort_pairs(n):
    pairs = []
    p = 1
    while p < n:
        k = p
        while k >= 1:
            for j in range(k % p, n - k, 2 * k):
                for i in range(min(k, n - j - k)):
                    if (i + j) // (2 * p) == (i + j + k) // (2 * p):
                        pairs.append((i + j, i + j + k))
            k //= 2
        p *= 2
    return pairs


def _bitonic_merge_pairs(n):
    pairs = []
    k = n // 2
    while k >= 1:
        pairs += [(i, i + k) for i in range(n) if not i & k]
        k //= 2
    return pairs


def _apply_network(vals, pairs):
    vals = list(vals)
    for i, j in pairs:
        a, b = vals[i], vals[j]
        if b is None:
            continue
        if a is None:
            vals[i], vals[j] = b, None
        else:
            vals[i], vals[j] = jnp.maximum(a, b), jnp.minimum(a, b)
    return vals


_SORT16 = _oddeven_merge_sort_pairs(PEER_TOPK)
_MERGE16 = _bitonic_merge_pairs(PEER_TOPK)
_SORT64 = _oddeven_merge_sort_pairs(64)
_CANDIDATES = [(a, b) for a in range(PEER_TOPK) for b in range(PEER_TOPK) if (a + 1) * (b + 1) <= PEER_TOPK]


def _top16_rows(sc):
    n = sc.shape[0] // SUBLANES
    assert n == PEER_TOPK
    x = _apply_network([sc[k * SUBLANES:(k + 1) * SUBLANES, :] for k in range(n)], _SORT16)
    for shift in (4, 2, 1):
        y = [jnp.maximum(x[k], pltpu.roll(x[n - 1 - k], shift, 0)) for k in range(n)]
        x = _apply_network(y, _MERGE16)
    return x


def _peer_scores_kernel(x_ref, wq_ref, keys_ref, cnt_ref, e1_ref, rank_ref, e2_ref):
    q = jnp.dot(x_ref[...].astype(BF16), wq_ref[...], preferred_element_type=F32).astype(BF16)
    for h in range(PEER_HEADS):
        sc, tops = [], []
        for c in range(2):
            col = (h * 2 + c) * PEER_HALF
            sc.append(lax.dot_general(keys_ref[h * 2 + c], q[:, col:col + PEER_HALF], _NT,
                                      preferred_element_type=F32))
            tops.append(_top16_rows(sc[c]))
        v1, v2 = tops
        cand = {(a, b): v1[a] + v2[b] for a, b in _CANDIDATES}
        best = _apply_network(list(cand.values()) + [None] * (64 - len(cand)), _SORT64)[:PEER_TOPK]
        tau = best[PEER_TOPK - 1]
        z = jnp.ones_like(tau)
        for r in range(1, PEER_TOPK):
            z = z + jnp.exp(best[r] - best[0])
        inv_z = 1.0 / z
        cnt_of_rank = []
        for a in range(PEER_TOPK):
            n = jnp.zeros_like(tau)
            for b in range(PEER_TOPK):
                if (a, b) in cand:
                    n = n + jnp.where(cand[(a, b)] >= tau, 1.0, 0.0)
            cnt_of_rank.append(n)
        for t in range(tau.shape[1] // LANES):
            lanes = slice(t * LANES, (t + 1) * LANES)
            row = lambda r: r[0:1, lanes]
            s1, s2 = sc[0][:, lanes], sc[1][:, lanes]
            cnt = jnp.zeros_like(s1)
            rank = jnp.zeros_like(s2)
            for a in range(PEER_TOPK):
                cnt = jnp.where(s1 == row(v1[a]), row(cnt_of_rank[a]), cnt)
                rank = jnp.where(row(v2[a]) > s2, float(a + 1), rank)
            cnt_ref[h, t] = cnt
            e1_ref[h, t] = jnp.exp(s1 - row(v1[0])) * row(inv_z)
            rank_ref[h, t] = pltpu.bitcast(rank.astype(BF16), jnp.uint32)
            e2_ref[h, t] = pltpu.bitcast(jnp.exp(s2 - row(v2[0])).astype(BF16), jnp.uint32)


def _peer_scores(x, wq, keys):
    rows = x.shape[0]
    tm = min(TOK_TILE, rows)
    assert rows % tm == 0
    return pl.pallas_call(
        _peer_scores_kernel,
        grid=(rows // tm,),
        in_specs=[pl.BlockSpec((tm, D_MODEL), lambda i: (i, 0)),
                  pl.BlockSpec(wq.shape, lambda i: (0, 0)),
                  pl.BlockSpec(keys.shape, lambda i: (0, 0, 0))],
        out_specs=[pl.BlockSpec((PEER_HEADS, tm // LANES, n, LANES), lambda i: (0, i, 0, 0))
                   for n in (N_KEYS, N_KEYS, N_KEYS // 2, N_KEYS // 2)],
        out_shape=[jax.ShapeDtypeStruct((PEER_HEADS, rows // LANES, n, LANES), dt)
                   for n, dt in ((N_KEYS, F32), (N_KEYS, F32), (N_KEYS // 2, jnp.uint32), (N_KEYS // 2, jnp.uint32))],
        compiler_params=_cparams(("parallel",)),
        name="peer_scores",
    )(x, wq, keys)


def _gelu(x):
    return 0.5 * x * (1.0 + lax.erf(x * math.sqrt(0.5)))


def _peer_dense_kernel(x_ref, cnt_ref, e1_ref, rank_ref, e2_ref, u_ref, vt_ref, g_ref, b_ref, y_ref,
                       xt, rowb, hid_a, hid_b, wt_a, wt_b, acc_a, acc_b):
    c = pl.program_id(1)
    rows_per_chunk = EXPERT_CHUNK // N_KEYS
    half = xt.shape[1] // 2
    ncol = half // LANES
    packed = 2 * SUBLANES
    slab = D_MODEL // PEER_PIPE_STEPS
    rows_per_trip = rows_per_chunk // PEER_PIPE_STEPS
    assert EXPERT_CHUNK == D_MODEL and rows_per_chunk == SUBLANES

    @pl.when(c == 0)
    def _():
        xt[...] = x_ref[...].T.astype(BF16)
        acc_a[...] = jnp.zeros_like(acc_a)
        acc_b[...] = jnp.zeros_like(acc_b)

    def store_hid(hid, rows, res):
        for t in range(ncol):
            hid[t, rows, :] = res[:, t * LANES:(t + 1) * LANES]

    def mask_rows(trip, block, hid, wt):
        for r in range(rows_per_trip):
            il = trip * rows_per_trip + r
            base = pl.multiple_of(il * N_KEYS, N_KEYS)
            for t in range(ncol):
                col = block * ncol + t
                gate = [None] * (N_KEYS // packed)
                for h in range(PEER_HEADS):
                    cnt = rowb[il, h, col]
                    g1 = rowb[il, PEER_HEADS + h, col]
                    for k in range(N_KEYS // packed):
                        words = slice(k * SUBLANES, (k + 1) * SUBLANES)
                        rank = pltpu.bitcast(rank_ref[h, col, words, :], BF16)
                        g2 = pltpu.bitcast(e2_ref[h, col, words, :], BF16)
                        g = jnp.where(rank < cnt, g2 * g1, jnp.zeros((), BF16))
                        gate[k] = g if gate[k] is None else gate[k] + g
                for k in range(N_KEYS // packed):
                    act = _gelu(hid[t, pl.ds(base + k * packed, packed), :]).astype(BF16)
                    wt[pl.ds(base + k * packed, packed), t * LANES:(t + 1) * LANES] = act * gate[k]

    store_hid(hid_a, slice(None), jnp.dot(u_ref[...], xt[:, :half], preferred_element_type=F32))
    group = pl.ds(pl.multiple_of(c * SUBLANES, SUBLANES), SUBLANES)
    for h in range(PEER_HEADS):
        for col in range(2 * ncol):
            cnt_rows, g1_rows = cnt_ref[h, col, group, :], e1_ref[h, col, group, :]
            for r in range(SUBLANES):
                rowb[r, h, col] = jnp.broadcast_to(cnt_rows[r:r + 1, :], (packed, LANES)).astype(BF16)
                rowb[r, PEER_HEADS + h, col] = jnp.broadcast_to(g1_rows[r:r + 1, :], (packed, LANES)).astype(BF16)

    def phase1(trip, carry):
        rows = pl.ds(pl.multiple_of(trip * slab, slab), slab)
        store_hid(hid_b, rows, jnp.dot(u_ref[rows, :], xt[:, half:], preferred_element_type=F32))
        mask_rows(trip, 0, hid_a, wt_a)
        return carry

    def phase2(trip, carry):
        rows = pl.ds(pl.multiple_of(trip * slab, slab), slab)
        acc_a[rows, :] += jnp.dot(vt_ref[rows, :], wt_a[...], preferred_element_type=F32)
        mask_rows(trip, 1, hid_b, wt_b)
        return carry

    lax.fori_loop(0, PEER_PIPE_STEPS, phase1, 0, unroll=True)
    lax.fori_loop(0, PEER_PIPE_STEPS, phase2, 0, unroll=True)
    acc_b[...] += jnp.dot(vt_ref[...], wt_b[...], preferred_element_type=F32)

    @pl.when(c == pl.num_programs(1) - 1)
    def _():
        g, b = g_ref[...], b_ref[...]
        y_ref[:half, :] = _layer_norm(ALPHA * x_ref[:half, :] + acc_a[...].T, g, b)
        y_ref[half:, :] = _layer_norm(ALPHA * x_ref[half:, :] + acc_b[...].T, g, b)


def _peer_dense(x, cnt, e1, rank, e2, u, vt, ln_g, ln_b):
    rows = x.shape[0]
    tm = min(TOK_TILE, rows)
    assert rows % tm == 0
    n_exp = u.shape[0]
    assert n_exp % EXPERT_CHUNK == 0
    return pl.pallas_call(
        _peer_dense_kernel,
        grid=(rows // tm, n_exp // EXPERT_CHUNK),
        in_specs=[pl.BlockSpec((tm, D_MODEL), lambda t, c: (t, 0))]
                 + [pl.BlockSpec((PEER_HEADS, tm // LANES, n, LANES), lambda t, c: (0, t, 0, 0))
                    for n in (N_KEYS, N_KEYS, N_KEYS // 2, N_KEYS // 2)]
                 + [pl.BlockSpec((EXPERT_CHUNK, D_MODEL), lambda t, c: (c, 0)),
                  pl.BlockSpec((D_MODEL, EXPERT_CHUNK), lambda t, c: (0, c)),
                  pl.BlockSpec((1, D_MODEL), lambda t, c: (0, 0)),
                  pl.BlockSpec((1, D_MODEL), lambda t, c: (0, 0))],
        out_specs=pl.BlockSpec((tm, D_MODEL), lambda t, c: (t, 0)),
        out_shape=jax.ShapeDtypeStruct((rows, D_MODEL), F32),
        scratch_shapes=[pltpu.VMEM((D_MODEL, tm), BF16),
                        pltpu.VMEM((SUBLANES, 2 * PEER_HEADS, tm // LANES, 2 * SUBLANES, LANES), BF16)]
                       + [pltpu.VMEM((tm // 2 // LANES, EXPERT_CHUNK, LANES), F32)] * 2
                       + [pltpu.VMEM((EXPERT_CHUNK, tm // 2), BF16)] * 2
                       + [pltpu.VMEM((D_MODEL, tm // 2), F32)] * 2,
        compiler_params=_cparams(("parallel", "arbitrary")),
        name="peer_dense",
    )(x, cnt, e1, rank, e2, u, vt, ln_g, ln_b)


def _rel_bucket_np(dist):
    n = np.maximum(dist, 0)
    max_exact = NUM_BUCKETS // 2
    large = max_exact + (np.log(np.maximum(n, max_exact).astype(np.float32) / max_exact)
                         / math.log(MAX_DISTANCE / max_exact) * (NUM_BUCKETS - max_exact)).astype(np.int32)
    return np.where(n < max_exact, n, np.minimum(large, NUM_BUCKETS - 1))


def _bias_kernel(table_ref, bucket_ref, out_ref, *, head_stride):
    head = pl.program_id(0) * head_stride + pl.program_id(1)
    bucket = bucket_ref[0]
    acc = jnp.full(bucket.shape, NEG, F32)
    for b in range(NUM_BUCKETS):
        acc = jnp.where(bucket == b, table_ref[b, head], acc)
    out_ref[0, 0] = acc


def _bias_tiles(table, bucket, n_outer, n_inner, head_stride):
    r, c = bucket.shape[1:]
    bmap = (lambda a, b: (a, 0, 0)) if bucket.shape[0] > 1 else (lambda a, b: (0, 0, 0))
    return pl.pallas_call(
        functools.partial(_bias_kernel, head_stride=head_stride),
        grid=(n_outer, n_inner),
        in_specs=[pl.BlockSpec(memory_space=pltpu.SMEM), pl.BlockSpec((1, r, c), bmap)],
        out_specs=pl.BlockSpec((1, 1, r, c), lambda a, b: (a, b, 0, 0)),
        out_shape=jax.ShapeDtypeStruct((n_outer, n_inner, r, c), F32),
        compiler_params=_cparams(("parallel", "parallel")),
        name="bias_tiles",
    )(table, jnp.asarray(bucket, jnp.int32))


def _prompt_bias(table):
    qi = np.arange(ATTN_BLOCK)[:, None]
    kj = np.arange(2 * ATTN_BLOCK)[None, :]
    dist = ATTN_BLOCK + qi - kj
    ok = (dist >= 0) & (dist < WINDOW)
    ok = np.stack([ok, ok & (kj >= ATTN_BLOCK - N_META)])
    bucket = np.where(ok, _rel_bucket_np(dist)[None], -1)
    return _bias_tiles(table, bucket, 2, N_HEADS, 0)


def _sample_bias(table, s_new):
    s, t = np.meshgrid(np.arange(SAMPLE_SEQS), np.arange(s_new), indexing="ij")
    s, t = s.reshape(-1, 1), t.reshape(-1, 1)
    out = []
    for cols, offset in ((WINDOW, WINDOW), (s_new, 0)):
        s2, c = np.meshgrid(np.arange(SAMPLE_SEQS), np.arange(cols), indexing="ij")
        s2, c = s2.reshape(1, -1), c.reshape(1, -1)
        dist = offset + t - c + 0 * s2
        ok = (s == s2) & (dist >= 0) & (dist < WINDOW)
        bucket = np.where(ok, _rel_bucket_np(dist), -1)[None]
        tiles = _bias_tiles(table, bucket, N_KV_HEADS, GROUP, GROUP)
        out.append(tiles.reshape(N_KV_HEADS, GROUP * bucket.shape[1], bucket.shape[2]))
    return out


def kernel(x_prompt, x_sample, cache_k, cache_v, state_pool, meta_tokens, rel_bias_table, w_in,
           w_pool_mix, pool_scale, attn_sinks, w_up_pool, w_up_attn, w_out, ln1_g, ln1_b,
           peer_w_query, peer_sub_keys, peer_u, peer_v, ln2_g, ln2_b):
    batch, seq, d = x_prompt.shape
    nb, s_new, _ = x_sample.shape
    assert w_in.shape[0] == DEPTH and d == D_MODEL and seq % ATTN_TILE == 0
    assert N_META >= max(POOL_WINDOWS) - 1 and cache_k.shape[2] == WINDOW

    w_in0 = w_in[0].astype(BF16)
    w_pqkv, w_gates = w_in0[:, :OFF_GA], w_in0[:, OFF_GA:]
    wmix = w_pool_mix[0].astype(BF16)
    scale = pool_scale[0].reshape(1, POOL_WIDTH)
    wup, wua, wout = w_up_pool[0].astype(BF16), w_up_attn[0].astype(BF16), w_out[0].astype(BF16)
    g1, b1 = ln1_g[0].reshape(1, d), ln1_b[0].reshape(1, d)
    g2, b2 = ln2_g[0].reshape(1, d), ln2_b[0].reshape(1, d)
    wq = peer_w_query[0].astype(BF16)
    keys = peer_sub_keys[0].reshape(2 * PEER_HEADS, N_KEYS, PEER_HALF).astype(BF16)
    u = peer_u[0].astype(BF16)
    vt = peer_v[0].astype(BF16).T
    sinks = attn_sinks[0].astype(F32)
    table = rel_bias_table.astype(F32)
    bias_p = _prompt_bias(table)
    bias_c, bias_n = _sample_bias(table, s_new)

    xp = x_prompt.reshape(batch * seq, d)
    xs = x_sample.reshape(nb * s_new, d)
    xm = jnp.concatenate([jnp.zeros((ATTN_BLOCK - N_META, d), F32), meta_tokens.astype(F32)], axis=0)

    p_p, q_p, k_p, v_p = _inproj(xp, w_pqkv)
    p_s, q_s, k_s, v_s = _inproj(xs, w_pqkv)
    p_m, _, k_m, v_m = _inproj(xm, w_pqkv)

    pooled_p, attn_p = _prompt_mix(sinks, q_p, k_p, v_p, p_p, k_m, v_m, p_m[ATTN_BLOCK - N_META:],
                                   bias_p, batch, seq)
    ck = cache_k[0].reshape(nb, WINDOW, KV_DIM)
    cv = cache_v[0].reshape(nb, WINDOW, KV_DIM)
    pooled_s, attn_s, newk, newv, newp = _sample_mix(sinks, q_s, k_s, v_s, p_s, ck, cv,
                                                     state_pool[0], bias_c, bias_n, s_new)

    outs = []
    for x, pooled, attn in ((xp, pooled_p, attn_p), (xs, pooled_s, attn_s)):
        x1 = _merge(x, pooled, attn, w_gates, wmix, scale, wup, wua, wout, g1, b1)
        cnt, e1, rank, e2 = _peer_scores(x1, wq, keys)
        outs.append(_peer_dense(x1, cnt, e1, rank, e2, u, vt, g2, b2))

    w_keep = min(WINDOW, seq + N_META)
    kv_shape = (batch, seq, N_KV_HEADS, HEAD_DIM)
    return (outs[0].reshape(batch, seq, d),
            outs[1].reshape(nb, s_new, d),
            k_p.reshape(kv_shape)[None, :, seq - w_keep:],
            v_p.reshape(kv_shape)[None, :, seq - w_keep:],
            p_p.reshape(batch, seq, POOL_WIDTH)[None, :, seq - POOL_STATE:],
            newk.reshape(1, nb, WINDOW, N_KV_HEADS, HEAD_DIM),
            newv.reshape(1, nb, WINDOW, N_KV_HEADS, HEAD_DIM),
            newp[None])
```

```python
import functools
import math

import jax
import jax.numpy as jnp
import numpy as np
from jax import lax
from jax.experimental import pallas as pl
from jax.experimental.pallas import tpu as pltpu

F32 = jnp.float32
BF16 = jnp.bfloat16

D_MODEL = 1024
N_META = 16
POOL_WIDTH = 512
POOL_WINDOWS = (2, 4, 8, 16)
POOL_GROUP_DIM = 128
POOL_STATE = 15
HEAD_DIM = 64
N_HEADS = 8
N_KV_HEADS = 2
GROUP = N_HEADS // N_KV_HEADS
WINDOW = 128
ATTN_BLOCK = 128
ATTN_SCALE = HEAD_DIM ** -0.5
NUM_BUCKETS = 32
MAX_DISTANCE = 128
Q_DIM = N_HEADS * HEAD_DIM
KV_DIM = N_KV_HEADS * HEAD_DIM
OFF_Q = POOL_WIDTH
OFF_K = OFF_Q + Q_DIM
OFF_V = OFF_K + KV_DIM
OFF_GA = OFF_V + KV_DIM
PEER_HEADS = 8
N_KEYS = 128
PEER_TOPK = 16
PEER_HALF = 128
DEPTH = 1
ALPHA = (2 * DEPTH) ** 0.25
LN_EPS = 1e-5
NEG = -1e30

LANES = 128
SUBLANES = 8
VMEM_LIMIT = 56 * 1024 * 1024

TOK_TILE = 512
ATTN_TILE = 512
SAMPLE_SEQS = 8
EXPERT_CHUNK = 1024


def _cparams(sem, flags=None):
    return pltpu.CompilerParams(dimension_semantics=sem, vmem_limit_bytes=VMEM_LIMIT, flags=flags)


def _inproj_kernel(x_ref, w_ref, p_ref, q_ref, k_ref, v_ref):
    z = jnp.dot(x_ref[...].astype(BF16), w_ref[...], preferred_element_type=F32)
    p_ref[...] = z[:, :OFF_Q]
    q_ref[...] = (z[:, OFF_Q:OFF_K] * ATTN_SCALE).astype(BF16)
    k_ref[...] = z[:, OFF_K:OFF_V]
    v_ref[...] = z[:, OFF_V:OFF_GA]


def _inproj(x, w_pqkv):
    rows = x.shape[0]
    tm = min(TOK_TILE, rows)
    assert rows % tm == 0
    row = lambda c: pl.BlockSpec((tm, c), lambda i: (i, 0))
    return pl.pallas_call(
        _inproj_kernel,
        grid=(rows // tm,),
        in_specs=[row(D_MODEL), pl.BlockSpec((D_MODEL, OFF_GA), lambda i: (0, 0))],
        out_specs=[row(POOL_WIDTH), row(Q_DIM), row(KV_DIM), row(KV_DIM)],
        out_shape=[jax.ShapeDtypeStruct((rows, POOL_WIDTH), F32),
                   jax.ShapeDtypeStruct((rows, Q_DIM), BF16),
                   jax.ShapeDtypeStruct((rows, KV_DIM), F32),
                   jax.ShapeDtypeStruct((rows, KV_DIM), F32)],
        compiler_params=_cparams(("parallel",)),
        name="inproj",
    )(x, w_pqkv)


def _window_pool(ext_ref, first, rows, out_ref):
    for g, w in enumerate(POOL_WINDOWS):
        cols = slice(g * POOL_GROUP_DIM, (g + 1) * POOL_GROUP_DIM)
        cur = ext_ref[first:first + rows, cols]
        acc = cur
        for r in range(1, w):
            acc = acc + ext_ref[first - r:first - r + rows, cols]
        out_ref[:, cols] = (acc * (1.0 / w) - cur).astype(out_ref.dtype)


def _sink_softmax(s, sink):
    m = jnp.maximum(jnp.max(s, axis=-1, keepdims=True), sink)
    e = jnp.exp(s - m)
    denom = jnp.sum(e, axis=-1, keepdims=True) + jnp.exp(sink - m)
    return e * (1.0 / denom)


_NT = (((1,), (1,)), ((), ()))


def _prompt_mix_kernel(sink_ref, q_ref, k_ref, v_ref, p_ref, kprev_ref, vprev_ref, pprev_ref,
                       mk_ref, mv_ref, mp_ref, bias_ref, pooled_ref, attn_ref,
                       kbuf, vbuf, pbuf):
    first = pl.program_id(1) == 0
    hist = ATTN_BLOCK
    kbuf[0:hist, :] = jnp.where(first, mk_ref[...], kprev_ref[...]).astype(BF16)
    vbuf[0:hist, :] = jnp.where(first, mv_ref[...], vprev_ref[...]).astype(BF16)
    kbuf[hist:, :] = k_ref[...].astype(BF16)
    vbuf[hist:, :] = v_ref[...].astype(BF16)
    pbuf[0:N_META, :] = jnp.where(first, mp_ref[...], pprev_ref[...])
    pbuf[N_META:, :] = p_ref[...]

    _window_pool(pbuf, N_META, ATTN_TILE, pooled_ref)

    first_i = jnp.where(first, 1, 0)
    for j in range(ATTN_TILE // ATTN_BLOCK):
        rows = slice(j * ATTN_BLOCK, (j + 1) * ATTN_BLOCK)
        keys = slice(j * ATTN_BLOCK, j * ATTN_BLOCK + 2 * ATTN_BLOCK)
        for kh in range(N_KV_HEADS):
            kk = kbuf[keys, kh * HEAD_DIM:(kh + 1) * HEAD_DIM]
            vv = vbuf[keys, kh * HEAD_DIM:(kh + 1) * HEAD_DIM]
            for g in range(GROUP):
                h = kh * GROUP + g
                hc = slice(h * HEAD_DIM, (h + 1) * HEAD_DIM)
                s = lax.dot_general(q_ref[rows, hc], kk, _NT, preferred_element_type=F32)
                bias = bias_ref[first_i, h] if j == 0 else bias_ref[0, h]
                prob = _sink_softmax(s + bias, sink_ref[h])
                o = jnp.dot(prob.astype(BF16), vv, preferred_element_type=F32)
                attn_ref[rows, hc] = o.astype(attn_ref.dtype)


def _prompt_mix(sinks, q, k, v, p, mk, mv, mp, bias, batch, seq):
    nt = seq // ATTN_TILE
    blocks_per_tile = ATTN_TILE // ATTN_BLOCK
    cur = lambda c: pl.BlockSpec((ATTN_TILE, c), lambda b, s: (b * nt + s, 0))
    prev_blk = lambda b, s: (jnp.maximum((b * nt + s) * blocks_per_tile - 1, 0), 0)
    prev_p = lambda b, s: (jnp.maximum((b * nt + s) * (ATTN_TILE // N_META) - 1, 0), 0)
    const2 = lambda b, s: (0, 0)
    rows = batch * seq
    return pl.pallas_call(
        _prompt_mix_kernel,
        grid=(batch, nt),
        in_specs=[pl.BlockSpec(memory_space=pltpu.SMEM),
                  cur(Q_DIM), cur(KV_DIM), cur(KV_DIM), cur(POOL_WIDTH),
                  pl.BlockSpec((ATTN_BLOCK, KV_DIM), prev_blk),
                  pl.BlockSpec((ATTN_BLOCK, KV_DIM), prev_blk),
                  pl.BlockSpec((N_META, POOL_WIDTH), prev_p),
                  pl.BlockSpec((ATTN_BLOCK, KV_DIM), const2),
                  pl.BlockSpec((ATTN_BLOCK, KV_DIM), const2),
                  pl.BlockSpec((N_META, POOL_WIDTH), const2),
                  pl.BlockSpec((2, N_HEADS, ATTN_BLOCK, 2 * ATTN_BLOCK), lambda b, s: (0, 0, 0, 0))],
        out_specs=[cur(POOL_WIDTH), cur(Q_DIM)],
        out_shape=[jax.ShapeDtypeStruct((rows, POOL_WIDTH), BF16),
                   jax.ShapeDtypeStruct((rows, Q_DIM), BF16)],
        scratch_shapes=[pltpu.VMEM((ATTN_BLOCK + ATTN_TILE, KV_DIM), BF16),
                        pltpu.VMEM((ATTN_BLOCK + ATTN_TILE, KV_DIM), BF16),
                        pltpu.VMEM((N_META + ATTN_TILE, POOL_WIDTH), F32)],
        compiler_params=_cparams(("parallel", "parallel")),
        name="prompt_mix",
    )(sinks, q, k, v, p, k, v, p, mk, mv, mp, bias)


def _sample_mix_kernel(sink_ref, q_ref, k_ref, v_ref, p_ref, ck_ref, cv_ref, st_ref,
                       biasc_ref, biasn_ref,
                       pooled_ref, attn_ref, newk_ref, newv_ref, newp_ref, ext, pooled_buf):
    nseq, w_cache, s_new = SAMPLE_SEQS, WINDOW, q_ref.shape[0] // SAMPLE_SEQS
    keep = POOL_STATE - s_new
    for i in range(nseq):
        new = slice(i * s_new, (i + 1) * s_new)
        ext[1:1 + POOL_STATE, :] = st_ref[i]
        ext[1 + POOL_STATE:1 + POOL_STATE + s_new, :] = p_ref[new, :]
        _window_pool(ext, 1 + POOL_STATE, s_new, pooled_buf)
        pooled_ref[new, :] = pooled_buf[...].astype(pooled_ref.dtype)
        newp_ref[i, 0:keep, :] = st_ref[i, s_new:POOL_STATE, :]
        newp_ref[i, keep:POOL_STATE, :] = p_ref[new, :]
        newk_ref[i, 0:w_cache - s_new, :] = ck_ref[i, s_new:w_cache, :]
        newk_ref[i, w_cache - s_new:w_cache, :] = k_ref[new, :]
        newv_ref[i, 0:w_cache - s_new, :] = cv_ref[i, s_new:w_cache, :]
        newv_ref[i, w_cache - s_new:w_cache, :] = v_ref[new, :]

    kc = ck_ref[...].reshape(nseq * w_cache, KV_DIM).astype(BF16)
    vc = cv_ref[...].reshape(nseq * w_cache, KV_DIM).astype(BF16)
    kn = k_ref[...].astype(BF16)
    vn = v_ref[...].astype(BF16)
    q = q_ref[...]
    nq = q.shape[0]
    for kh in range(N_KV_HEADS):
        kvc = slice(kh * HEAD_DIM, (kh + 1) * HEAD_DIM)
        qg = jnp.concatenate([q[:, (kh * GROUP + g) * HEAD_DIM:(kh * GROUP + g + 1) * HEAD_DIM]
                              for g in range(GROUP)], axis=0)
        sc = lax.dot_general(qg, kc[:, kvc], _NT, preferred_element_type=F32) + biasc_ref[kh]
        sn = lax.dot_general(qg, kn[:, kvc], _NT, preferred_element_type=F32) + biasn_ref[kh]
        sink = jnp.concatenate([jnp.full((nq, 1), sink_ref[kh * GROUP + g], F32) for g in range(GROUP)], axis=0)
        m = jnp.maximum(jnp.maximum(jnp.max(sc, axis=-1, keepdims=True),
                                    jnp.max(sn, axis=-1, keepdims=True)), sink)
        ec = jnp.exp(sc - m)
        en = jnp.exp(sn - m)
        denom = (jnp.sum(ec, axis=-1, keepdims=True) + jnp.sum(en, axis=-1, keepdims=True)
                 + jnp.exp(sink - m))
        inv = 1.0 / denom
        o = (jnp.dot((ec * inv).astype(BF16), vc[:, kvc], preferred_element_type=F32)
             + jnp.dot((en * inv).astype(BF16), vn[:, kvc], preferred_element_type=F32))
        for g in range(GROUP):
            h = kh * GROUP + g
            attn_ref[:, h * HEAD_DIM:(h + 1) * HEAD_DIM] = o[g * nq:(g + 1) * nq].astype(attn_ref.dtype)


def _sample_mix(sinks, q, k, v, p, cache_k, cache_v, state, biasc, biasn, s_new):
    nb = cache_k.shape[0]
    rows = SAMPLE_SEQS * s_new
    tok = lambda c: pl.BlockSpec((rows, c), lambda i: (i, 0))
    seq3 = lambda r, c: pl.BlockSpec((SAMPLE_SEQS, r, c), lambda i: (i, 0, 0))
    const3 = lambda a: pl.BlockSpec(a.shape, lambda i: (0, 0, 0))
    return pl.pallas_call(
        _sample_mix_kernel,
        grid=(nb // SAMPLE_SEQS,),
        in_specs=[pl.BlockSpec(memory_space=pltpu.SMEM),
                  tok(Q_DIM), tok(KV_DIM), tok(KV_DIM), tok(POOL_WIDTH),
                  seq3(WINDOW, KV_DIM), seq3(WINDOW, KV_DIM), seq3(POOL_STATE, POOL_WIDTH),
                  const3(biasc), const3(biasn)],
        out_specs=[tok(POOL_WIDTH), tok(Q_DIM),
                   seq3(WINDOW, KV_DIM), seq3(WINDOW, KV_DIM), seq3(POOL_STATE, POOL_WIDTH)],
        out_shape=[jax.ShapeDtypeStruct((nb * s_new, POOL_WIDTH), F32),
                   jax.ShapeDtypeStruct((nb * s_new, Q_DIM), F32),
                   jax.ShapeDtypeStruct((nb, WINDOW, KV_DIM), F32),
                   jax.ShapeDtypeStruct((nb, WINDOW, KV_DIM), F32),
                   jax.ShapeDtypeStruct((nb, POOL_STATE, POOL_WIDTH), F32)],
        scratch_shapes=[pltpu.VMEM((1 + POOL_STATE + SUBLANES, POOL_WIDTH), F32),
                        pltpu.VMEM((s_new, POOL_WIDTH), F32)],
        compiler_params=_cparams(("parallel",)),
        name="sample_mix",
    )(sinks, q, k, v, p, cache_k, cache_v, state, biasc, biasn)


def _layer_norm(x, g, b):
    mu = jnp.mean(x, axis=-1, keepdims=True)
    xc = x - mu
    var = jnp.mean(xc * xc, axis=-1, keepdims=True)
    return xc * lax.rsqrt(var + LN_EPS) * g + b


def _merge_kernel(x_ref, pooled_ref, attn_ref, wg_ref, wmix_ref, scale_ref, wup_ref, wua_ref,
                  wout_ref, g_ref, b_ref, y_ref):
    x = x_ref[...]
    glog = jnp.dot(x.astype(BF16), wg_ref[...], preferred_element_type=F32)
    g_pool = jax.nn.sigmoid(glog[:, :D_MODEL])
    g_attn = jax.nn.sigmoid(glog[:, D_MODEL:])
    pooled = pooled_ref[...].astype(BF16)
    mixed = jnp.concatenate(
        [jnp.dot(pooled[:, g * POOL_GROUP_DIM:(g + 1) * POOL_GROUP_DIM], wmix_ref[g],
                 preferred_element_type=F32) for g in range(len(POOL_WINDOWS))], axis=1)
    pool_out = (mixed * scale_ref[...]).astype(BF16)
    a = jnp.dot(pool_out, wup_ref[...], preferred_element_type=F32)
    b = jnp.dot(attn_ref[...].astype(BF16), wua_ref[...], preferred_element_type=F32)
    m = (g_pool * a + g_attn * b).astype(BF16)
    r = jnp.dot(m, wout_ref[...], preferred_element_type=F32)
    y_ref[...] = _layer_norm(ALPHA * x + r, g_ref[...], b_ref[...])


def _merge(x, pooled, attn, wg, wmix, scale, wup, wua, wout, ln_g, ln_b):
    rows = x.shape[0]
    tm = min(TOK_TILE, rows)
    assert rows % tm == 0
    row = lambda c: pl.BlockSpec((tm, c), lambda i: (i, 0))
    full = lambda a: pl.BlockSpec(a.shape, lambda i: (0,) * a.ndim)
    weights = (wg, wmix, scale, wup, wua, wout, ln_g, ln_b)
    return pl.pallas_call(
        _merge_kernel,
        grid=(rows // tm,),
        in_specs=[row(D_MODEL), row(POOL_WIDTH), row(Q_DIM)] + [full(a) for a in weights],
        out_specs=row(D_MODEL),
        out_shape=jax.ShapeDtypeStruct((rows, D_MODEL), F32),
        compiler_params=_cparams(("parallel",)),
        name="merge",
    )(x, pooled, attn, *weights)


def _oddeven_merge_sort_pairs(n):
    pairs = []
    p = 1
    while p < n:
        k = p
        while k >= 1:
            for j in range(k % p, n - k, 2 * k):
                for i in range(min(k, n - j - k)):
                    if (i + j) // (2 * p) == (i + j + k) // (2 * p):
                        pairs.append((i + j, i + j + k))
            k //= 2
        p *= 2
    return pairs


def _bitonic_merge_pairs(n):
    pairs = []
    k = n // 2
    while k >= 1:
        pairs += [(i, i + k) for i in range(n) if not i & k]
        k //= 2
    return pairs


def _apply_network(vals, pairs):
    vals = list(vals)
    for i, j in pairs:
        a, b = vals[i], vals[j]
        if b is None:
            continue
        if a is None:
            vals[i], vals[j] = b, None
        else:
            vals[i], vals[j] = jnp.maximum(a, b), jnp.minimum(a, b)
    return vals


_SORT16 = _oddeven_merge_sort_pairs(PEER_TOPK)
_MERGE16 = _bitonic_merge_pairs(PEER_TOPK)


def _top16_rows(sc):
    n = sc.shape[0] // SUBLANES
    assert n == PEER_TOPK
    x = _apply_network([sc[k * SUBLANES:(k + 1) * SUBLANES, :] for k in range(n)], _SORT16)
    for shift in (4, 2, 1):
        y = [jnp.maximum(x[k], pltpu.roll(x[n - 1 - k], shift, 0)) for k in range(n)]
        x = _apply_network(y, _MERGE16)
    return x


def _best_sums(v1, v2):
    row = lax.broadcasted_iota(jnp.int32, v1[0].shape, 0)

    def one_per_sublane(vals):
        x = vals[SUBLANES - 1]
        for s in range(SUBLANES - 2, -1, -1):
            x = jnp.where(row == s, vals[s], x)
        return x

    lo, hi = one_per_sublane(v1[:SUBLANES]), one_per_sublane(v1[SUBLANES:])
    sums_lo = []
    for b in range(PEER_TOPK):
        n_valid = sum((a + 1) * (b + 1) <= PEER_TOPK for a in range(SUBLANES))
        s = lo + v2[b]
        sums_lo.append(s if n_valid == SUBLANES else jnp.where(row < n_valid, s, NEG))
    sums_hi = hi + v2[0]
    x, best = sums_hi, []
    for k in range(PEER_TOPK):
        best.append(jnp.maximum(sums_lo[k], x))
        x = jnp.minimum(sums_lo[k], x)
    for shift in (4, 2, 1):
        y = [jnp.maximum(best[k], pltpu.roll(best[PEER_TOPK - 1 - k], shift, 0)) for k in range(PEER_TOPK)]
        best = _apply_network(y, _MERGE16)
    return best, sums_lo, sums_hi


def _peer_scores_kernel(x_ref, wq_ref, keys_ref, cnt_ref, e1_ref, rank_ref, e2_ref):
    q = jnp.dot(x_ref[...].astype(BF16), wq_ref[...], preferred_element_type=F32).astype(BF16)
    for h in range(PEER_HEADS):
        sc, tops = [], []
        for c in range(2):
            col = (h * 2 + c) * PEER_HALF
            sc.append(lax.dot_general(keys_ref[h * 2 + c], q[:, col:col + PEER_HALF], _NT,
                                      preferred_element_type=F32))
            tops.append(_top16_rows(sc[c]))
        v1, v2 = tops
        best, sums_lo, sums_hi = _best_sums(v1, v2)
        tau = best[PEER_TOPK - 1]
        z = jnp.ones_like(tau)
        for r in range(1, PEER_TOPK):
            z = z + jnp.exp(best[r] - best[0])
        inv_z = 1.0 / z
        cnt_lo = jnp.zeros_like(tau)
        for s in sums_lo:
            cnt_lo = cnt_lo + jnp.where(s >= tau, 1.0, 0.0)
        cnt_hi = jnp.where(sums_hi >= tau, 1.0, 0.0)
        for t in range(tau.shape[1] // LANES):
            lanes = slice(t * LANES, (t + 1) * LANES)
            row = lambda r: r[0:1, lanes]
            s1, s2 = sc[0][:, lanes], sc[1][:, lanes]
            cnt = jnp.zeros_like(s1)
            rank = jnp.zeros_like(s2)
            for a in range(PEER_TOPK):
                cnt_a = (cnt_lo if a < SUBLANES else cnt_hi)[a % SUBLANES:a % SUBLANES + 1, lanes]
                cnt = jnp.where(s1 == row(v1[a]), cnt_a, cnt)
                rank = jnp.where(row(v2[a]) > s2, float(a + 1), rank)
            cnt_ref[h, t] = cnt
            e1_ref[h, t] = jnp.exp(s1 - row(v1[0])) * row(inv_z)
            rank_ref[h, t] = pltpu.bitcast(rank.astype(BF16), jnp.uint32)
            e2_ref[h, t] = pltpu.bitcast(jnp.exp(s2 - row(v2[0])).astype(BF16), jnp.uint32)


def _peer_scores(x, wq, keys):
    rows = x.shape[0]
    tm = min(TOK_TILE, rows)
    assert rows % tm == 0
    return pl.pallas_call(
        _peer_scores_kernel,
        grid=(rows // tm,),
        in_specs=[pl.BlockSpec((tm, D_MODEL), lambda i: (i, 0)),
                  pl.BlockSpec(wq.shape, lambda i: (0, 0)),
                  pl.BlockSpec(keys.shape, lambda i: (0, 0, 0))],
        out_specs=[pl.BlockSpec((PEER_HEADS, tm // LANES, n, LANES), lambda i: (0, i, 0, 0))
                   for n in (N_KEYS, N_KEYS, N_KEYS // 2, N_KEYS // 2)],
        out_shape=[jax.ShapeDtypeStruct((PEER_HEADS, rows // LANES, n, LANES), dt)
                   for n, dt in ((N_KEYS, F32), (N_KEYS, F32), (N_KEYS // 2, jnp.uint32), (N_KEYS // 2, jnp.uint32))],
        compiler_params=_cparams(("parallel",)),
        name="peer_scores",
    )(x, wq, keys)


def _gelu(x):
    return 0.5 * x * (1.0 + lax.erf(x * math.sqrt(0.5)))


def _peer_dense_kernel(x_ref, cnt_ref, e1_ref, rank_ref, e2_ref, u_ref, v_ref, g_ref, b_ref, y_ref,
                       xt, rowb, hid, wt, acc):
    c = pl.program_id(1)
    rows_per_chunk = EXPERT_CHUNK // N_KEYS
    ncol = xt.shape[1] // LANES
    packed = 2 * SUBLANES
    assert rows_per_chunk == SUBLANES

    @pl.when(c == 0)
    def _():
        xt[...] = x_ref[...].T.astype(BF16)
        acc[...] = jnp.zeros_like(acc)

    def build_rows(chunk):
        for h in range(PEER_HEADS):
            for col in range(ncol):
                for r in range(SUBLANES):
                    row = pl.ds(chunk * SUBLANES + r, packed, stride=0)
                    rowb[r, h, col] = cnt_ref[h, col, row, :].astype(BF16)
                    rowb[r, PEER_HEADS + h, col] = e1_ref[h, col, row, :].astype(BF16)

    def mask_row(il, carry):
        base = pl.multiple_of(il * N_KEYS, N_KEYS)
        for col in range(ncol):
            gate = [None] * (N_KEYS // packed)
            for h in range(PEER_HEADS):
                cnt = rowb[il, h, col]
                g1 = rowb[il, PEER_HEADS + h, col]
                for k in range(N_KEYS // packed):
                    words = slice(k * SUBLANES, (k + 1) * SUBLANES)
                    rank = pltpu.bitcast(rank_ref[h, col, words, :], BF16)
                    g2 = pltpu.bitcast(e2_ref[h, col, words, :], BF16)
                    g = jnp.where(rank < cnt, g2 * g1, jnp.zeros((), BF16))
                    gate[k] = g if gate[k] is None else gate[k] + g
            for k in range(N_KEYS // packed):
                act = _gelu(hid[col, pl.ds(base + k * packed, packed), :]).astype(BF16)
                wt[pl.ds(base + k * packed, packed), col * LANES:(col + 1) * LANES] = act * gate[k]
        return carry

    res = jnp.dot(u_ref[...], xt[...], preferred_element_type=F32)
    for col in range(ncol):
        hid[col] = res[:, col * LANES:(col + 1) * LANES]
    build_rows(c)
    lax.fori_loop(0, rows_per_chunk, mask_row, 0)
    acc[...] += lax.dot_general(v_ref[...], wt[...], (((0,), (0,)), ((), ())), preferred_element_type=F32)

    @pl.when(c == pl.num_programs(1) - 1)
    def _():
        y_ref[...] = _layer_norm(ALPHA * x_ref[...] + acc[...].T, g_ref[...], b_ref[...])


def _peer_dense(x, cnt, e1, rank, e2, u, vt, ln_g, ln_b):
    rows = x.shape[0]
    tm = min(TOK_TILE, rows)
    assert rows % tm == 0
    n_exp = u.shape[0]
    assert n_exp % EXPERT_CHUNK == 0
    return pl.pallas_call(
        _peer_dense_kernel,
        grid=(rows // tm, n_exp // EXPERT_CHUNK),
        in_specs=[pl.BlockSpec((tm, D_MODEL), lambda t, c: (t, 0))]
                 + [pl.BlockSpec((PEER_HEADS, tm // LANES, n, LANES), lambda t, c: (0, t, 0, 0))
                    for n in (N_KEYS, N_KEYS, N_KEYS // 2, N_KEYS // 2)]
                 + [pl.BlockSpec((EXPERT_CHUNK, D_MODEL), lambda t, c: (c, 0)),
                    pl.BlockSpec((EXPERT_CHUNK, D_MODEL), lambda t, c: (c, 0)),
                    pl.BlockSpec((1, D_MODEL), lambda t, c: (0, 0)),
                    pl.BlockSpec((1, D_MODEL), lambda t, c: (0, 0))],
        out_specs=pl.BlockSpec((tm, D_MODEL), lambda t, c: (t, 0)),
        out_shape=jax.ShapeDtypeStruct((rows, D_MODEL), F32),
        scratch_shapes=[pltpu.VMEM((D_MODEL, tm), BF16),
                        pltpu.VMEM((SUBLANES, 2 * PEER_HEADS, tm // LANES, 2 * SUBLANES, LANES), BF16),
                        pltpu.VMEM((tm // LANES, EXPERT_CHUNK, LANES), F32),
                        pltpu.VMEM((EXPERT_CHUNK, tm), BF16),
                        pltpu.VMEM((D_MODEL, tm), F32)],
        compiler_params=_cparams(("parallel", "arbitrary")),
        name="peer_dense",
    )(x, cnt, e1, rank, e2, u, vt, ln_g, ln_b)


def _rel_bucket_np(dist):
    n = np.maximum(dist, 0)
    max_exact = NUM_BUCKETS // 2
    large = max_exact + (np.log(np.maximum(n, max_exact).astype(np.float32) / max_exact)
                         / math.log(MAX_DISTANCE / max_exact) * (NUM_BUCKETS - max_exact)).astype(np.int32)
    return np.where(n < max_exact, n, np.minimum(large, NUM_BUCKETS - 1))


def _bias_kernel(table_ref, bucket_ref, out_ref, *, head_stride):
    head = pl.program_id(0) * head_stride + pl.program_id(1)
    bucket = bucket_ref[0]
    acc = jnp.full(bucket.shape, NEG, F32)
    for b in range(NUM_BUCKETS):
        acc = jnp.where(bucket == b, table_ref[b, head], acc)
    out_ref[0, 0] = acc


def _bias_tiles(table, bucket, n_outer, n_inner, head_stride):
    r, c = bucket.shape[1:]
    bmap = (lambda a, b: (a, 0, 0)) if bucket.shape[0] > 1 else (lambda a, b: (0, 0, 0))
    return pl.pallas_call(
        functools.partial(_bias_kernel, head_stride=head_stride),
        grid=(n_outer, n_inner),
        in_specs=[pl.BlockSpec(memory_space=pltpu.SMEM), pl.BlockSpec((1, r, c), bmap)],
        out_specs=pl.BlockSpec((1, 1, r, c), lambda a, b: (a, b, 0, 0)),
        out_shape=jax.ShapeDtypeStruct((n_outer, n_inner, r, c), F32),
        compiler_params=_cparams(("parallel", "parallel")),
        name="bias_tiles",
    )(table, jnp.asarray(bucket, jnp.int32))


def _prompt_bias(table):
    qi = np.arange(ATTN_BLOCK)[:, None]
    kj = np.arange(2 * ATTN_BLOCK)[None, :]
    dist = ATTN_BLOCK + qi - kj
    ok = (dist >= 0) & (dist < WINDOW)
    ok = np.stack([ok, ok & (kj >= ATTN_BLOCK - N_META)])
    bucket = np.where(ok, _rel_bucket_np(dist)[None], -1)
    return _bias_tiles(table, bucket, 2, N_HEADS, 0)


def _sample_bias(table, s_new):
    s, t = np.meshgrid(np.arange(SAMPLE_SEQS), np.arange(s_new), indexing="ij")
    s, t = s.reshape(-1, 1), t.reshape(-1, 1)
    out = []
    for cols, offset in ((WINDOW, WINDOW), (s_new, 0)):
        s2, c = np.meshgrid(np.arange(SAMPLE_SEQS), np.arange(cols), indexing="ij")
        s2, c = s2.reshape(1, -1), c.reshape(1, -1)
        dist = offset + t - c + 0 * s2
        ok = (s == s2) & (dist >= 0) & (dist < WINDOW)
        bucket = np.where(ok, _rel_bucket_np(dist), -1)[None]
        tiles = _bias_tiles(table, bucket, N_KV_HEADS, GROUP, GROUP)
        out.append(tiles.reshape(N_KV_HEADS, GROUP * bucket.shape[1], bucket.shape[2]))
    return out


def kernel(x_prompt, x_sample, cache_k, cache_v, state_pool, meta_tokens, rel_bias_table, w_in,
           w_pool_mix, pool_scale, attn_sinks, w_up_pool, w_up_attn, w_out, ln1_g, ln1_b,
           peer_w_query, peer_sub_keys, peer_u, peer_v, ln2_g, ln2_b):
    batch, seq, d = x_prompt.shape
    nb, s_new, _ = x_sample.shape
    assert w_in.shape[0] == DEPTH and d == D_MODEL and seq % ATTN_TILE == 0
    assert N_META >= max(POOL_WINDOWS) - 1 and cache_k.shape[2] == WINDOW

    w_in0 = w_in[0].astype(BF16)
    w_pqkv, w_gates = w_in0[:, :OFF_GA], w_in0[:, OFF_GA:]
    wmix = w_pool_mix[0].astype(BF16)
    scale = pool_scale[0].reshape(1, POOL_WIDTH)
    wup, wua, wout = w_up_pool[0].astype(BF16), w_up_attn[0].astype(BF16), w_out[0].astype(BF16)
    g1, b1 = ln1_g[0].reshape(1, d), ln1_b[0].reshape(1, d)
    g2, b2 = ln2_g[0].reshape(1, d), ln2_b[0].reshape(1, d)
    wq = peer_w_query[0].astype(BF16)
    keys = peer_sub_keys[0].reshape(2 * PEER_HEADS, N_KEYS, PEER_HALF).astype(BF16)
    u = peer_u[0].astype(BF16)
    vt = peer_v[0].astype(BF16)
    sinks = attn_sinks[0].astype(F32)
    table = rel_bias_table.astype(F32)
    bias_p = _prompt_bias(table)
    bias_c, bias_n = _sample_bias(table, s_new)

    xp = x_prompt.reshape(batch * seq, d)
    xs = x_sample.reshape(nb * s_new, d)
    xm = jnp.concatenate([jnp.zeros((ATTN_BLOCK - N_META, d), F32), meta_tokens.astype(F32)], axis=0)

    p_p, q_p, k_p, v_p = _inproj(xp, w_pqkv)
    p_s, q_s, k_s, v_s = _inproj(xs, w_pqkv)
    p_m, _, k_m, v_m = _inproj(xm, w_pqkv)

    pooled_p, attn_p = _prompt_mix(sinks, q_p, k_p, v_p, p_p, k_m, v_m, p_m[ATTN_BLOCK - N_META:],
                                   bias_p, batch, seq)
    ck = cache_k[0].reshape(nb, WINDOW, KV_DIM)
    cv = cache_v[0].reshape(nb, WINDOW, KV_DIM)
    pooled_s, attn_s, newk, newv, newp = _sample_mix(sinks, q_s, k_s, v_s, p_s, ck, cv,
                                                     state_pool[0], bias_c, bias_n, s_new)

    outs = []
    for x, pooled, attn in ((xp, pooled_p, attn_p), (xs, pooled_s, attn_s)):
        x1 = _merge(x, pooled, attn, w_gates, wmix, scale, wup, wua, wout, g1, b1)
        cnt, e1, rank, e2 = _peer_scores(x1, wq, keys)
        outs.append(_peer_dense(x1, cnt, e1, rank, e2, u, vt, g2, b2))

    w_keep = min(WINDOW, seq + N_META)
    kv_shape = (batch, seq, N_KV_HEADS, HEAD_DIM)
    return (outs[0].reshape(batch, seq, d),
            outs[1].reshape(nb, s_new, d),
            k_p.reshape(kv_shape)[None, :, seq - w_keep:],
            v_p.reshape(kv_shape)[None, :, seq - w_keep:],
            p_p.reshape(batch, seq, POOL_WIDTH)[None, :, seq - POOL_STATE:],
            newk.reshape(1, nb, WINDOW, N_KV_HEADS, HEAD_DIM),
            newv.reshape(1, nb, WINDOW, N_KV_HEADS, HEAD_DIM),
            newp[None])
```

```python
import functools
import math

import jax
import jax.numpy as jnp
import numpy as np
from jax import lax
from jax.experimental import pallas as pl
from jax.experimental.pallas import tpu as pltpu

F32 = jnp.float32
BF16 = jnp.bfloat16

D_MODEL = 1024
N_META = 16
POOL_WIDTH = 512
POOL_WINDOWS = (2, 4, 8, 16)
POOL_GROUP_DIM = 128
POOL_STATE = 15
HEAD_DIM = 64
N_HEADS = 8
N_KV_HEADS = 2
GROUP = N_HEADS // N_KV_HEADS
WINDOW = 128
ATTN_BLOCK = 128
ATTN_SCALE = HEAD_DIM ** -0.5
NUM_BUCKETS = 32
MAX_DISTANCE = 128
Q_DIM = N_HEADS * HEAD_DIM
KV_DIM = N_KV_HEADS * HEAD_DIM
OFF_Q = POOL_WIDTH
OFF_K = OFF_Q + Q_DIM
OFF_V = OFF_K + KV_DIM
OFF_GA = OFF_V + KV_DIM
PEER_HEADS = 8
N_KEYS = 128
PEER_TOPK = 16
PEER_HALF = 128
DEPTH = 1
ALPHA = (2 * DEPTH) ** 0.25
LN_EPS = 1e-5
NEG = -1e30

LANES = 128
SUBLANES = 8
VMEM_LIMIT = 56 * 1024 * 1024

TOK_TILE = 512
ATTN_TILE = 512
SAMPLE_SEQS = 8
EXPERT_CHUNK = 1024


def _cparams(sem, flags=None):
    return pltpu.CompilerParams(dimension_semantics=sem, vmem_limit_bytes=VMEM_LIMIT, flags=flags)


def _inproj_kernel(x_ref, w_ref, p_ref, q_ref, k_ref, v_ref):
    z = jnp.dot(x_ref[...].astype(BF16), w_ref[...], preferred_element_type=F32)
    p_ref[...] = z[:, :OFF_Q]
    q_ref[...] = (z[:, OFF_Q:OFF_K] * ATTN_SCALE).astype(BF16)
    k_ref[...] = z[:, OFF_K:OFF_V]
    v_ref[...] = z[:, OFF_V:OFF_GA]


def _inproj(x, w_pqkv):
    rows = x.shape[0]
    tm = min(TOK_TILE, rows)
    assert rows % tm == 0
    row = lambda c: pl.BlockSpec((tm, c), lambda i: (i, 0))
    return pl.pallas_call(
        _inproj_kernel,
        grid=(rows // tm,),
        in_specs=[row(D_MODEL), pl.BlockSpec((D_MODEL, OFF_GA), lambda i: (0, 0))],
        out_specs=[row(POOL_WIDTH), row(Q_DIM), row(KV_DIM), row(KV_DIM)],
        out_shape=[jax.ShapeDtypeStruct((rows, POOL_WIDTH), F32),
                   jax.ShapeDtypeStruct((rows, Q_DIM), BF16),
                   jax.ShapeDtypeStruct((rows, KV_DIM), F32),
                   jax.ShapeDtypeStruct((rows, KV_DIM), F32)],
        compiler_params=_cparams(("parallel",)),
        name="inproj",
    )(x, w_pqkv)


def _window_pool(ext_ref, first, rows, out_ref):
    for g, w in enumerate(POOL_WINDOWS):
        cols = slice(g * POOL_GROUP_DIM, (g + 1) * POOL_GROUP_DIM)
        cur = ext_ref[first:first + rows, cols]
        acc = cur
        for r in range(1, w):
            acc = acc + ext_ref[first - r:first - r + rows, cols]
        out_ref[:, cols] = (acc * (1.0 / w) - cur).astype(out_ref.dtype)


def _sink_softmax(s, sink):
    m = jnp.maximum(jnp.max(s, axis=-1, keepdims=True), sink)
    e = jnp.exp(s - m)
    denom = jnp.sum(e, axis=-1, keepdims=True) + jnp.exp(sink - m)
    return e * (1.0 / denom)


_NT = (((1,), (1,)), ((), ()))


def _prompt_mix_kernel(sink_ref, q_ref, k_ref, v_ref, p_ref, kprev_ref, vprev_ref, pprev_ref,
                       mk_ref, mv_ref, mp_ref, bias_ref, pooled_ref, attn_ref,
                       kbuf, vbuf, pbuf):
    first = pl.program_id(1) == 0
    hist = ATTN_BLOCK
    kbuf[0:hist, :] = jnp.where(first, mk_ref[...], kprev_ref[...]).astype(BF16)
    vbuf[0:hist, :] = jnp.where(first, mv_ref[...], vprev_ref[...]).astype(BF16)
    kbuf[hist:, :] = k_ref[...].astype(BF16)
    vbuf[hist:, :] = v_ref[...].astype(BF16)
    pbuf[0:N_META, :] = jnp.where(first, mp_ref[...], pprev_ref[...])
    pbuf[N_META:, :] = p_ref[...]

    _window_pool(pbuf, N_META, ATTN_TILE, pooled_ref)

    first_i = jnp.where(first, 1, 0)
    for j in range(ATTN_TILE // ATTN_BLOCK):
        rows = slice(j * ATTN_BLOCK, (j + 1) * ATTN_BLOCK)
        keys = slice(j * ATTN_BLOCK, j * ATTN_BLOCK + 2 * ATTN_BLOCK)
        for kh in range(N_KV_HEADS):
            kk = kbuf[keys, kh * HEAD_DIM:(kh + 1) * HEAD_DIM]
            vv = vbuf[keys, kh * HEAD_DIM:(kh + 1) * HEAD_DIM]
            for g in range(GROUP):
                h = kh * GROUP + g
                hc = slice(h * HEAD_DIM, (h + 1) * HEAD_DIM)
                s = lax.dot_general(q_ref[rows, hc], kk, _NT, preferred_element_type=F32)
                bias = bias_ref[first_i, h] if j == 0 else bias_ref[0, h]
                prob = _sink_softmax(s + bias, sink_ref[h])
                o = jnp.dot(prob.astype(BF16), vv, preferred_element_type=F32)
                attn_ref[rows, hc] = o.astype(attn_ref.dtype)


def _prompt_mix(sinks, q, k, v, p, mk, mv, mp, bias, batch, seq):
    nt = seq // ATTN_TILE
    blocks_per_tile = ATTN_TILE // ATTN_BLOCK
    cur = lambda c: pl.BlockSpec((ATTN_TILE, c), lambda b, s: (b * nt + s, 0))
    prev_blk = lambda b, s: (jnp.maximum((b * nt + s) * blocks_per_tile - 1, 0), 0)
    prev_p = lambda b, s: (jnp.maximum((b * nt + s) * (ATTN_TILE // N_META) - 1, 0), 0)
    const2 = lambda b, s: (0, 0)
    rows = batch * seq
    return pl.pallas_call(
        _prompt_mix_kernel,
        grid=(batch, nt),
        in_specs=[pl.BlockSpec(memory_space=pltpu.SMEM),
                  cur(Q_DIM), cur(KV_DIM), cur(KV_DIM), cur(POOL_WIDTH),
                  pl.BlockSpec((ATTN_BLOCK, KV_DIM), prev_blk),
                  pl.BlockSpec((ATTN_BLOCK, KV_DIM), prev_blk),
                  pl.BlockSpec((N_META, POOL_WIDTH), prev_p),
                  pl.BlockSpec((ATTN_BLOCK, KV_DIM), const2),
                  pl.BlockSpec((ATTN_BLOCK, KV_DIM), const2),
                  pl.BlockSpec((N_META, POOL_WIDTH), const2),
                  pl.BlockSpec((2, N_HEADS, ATTN_BLOCK, 2 * ATTN_BLOCK), lambda b, s: (0, 0, 0, 0))],
        out_specs=[cur(POOL_WIDTH), cur(Q_DIM)],
        out_shape=[jax.ShapeDtypeStruct((rows, POOL_WIDTH), BF16),
                   jax.ShapeDtypeStruct((rows, Q_DIM), BF16)],
        scratch_shapes=[pltpu.VMEM((ATTN_BLOCK + ATTN_TILE, KV_DIM), BF16),
                        pltpu.VMEM((ATTN_BLOCK + ATTN_TILE, KV_DIM), BF16),
                        pltpu.VMEM((N_META + ATTN_TILE, POOL_WIDTH), F32)],
        compiler_params=_cparams(("parallel", "parallel")),
        name="prompt_mix",
    )(sinks, q, k, v, p, k, v, p, mk, mv, mp, bias)


def _sample_mix_kernel(sink_ref, q_ref, k_ref, v_ref, p_ref, ck_ref, cv_ref, st_ref,
                       biasc_ref, biasn_ref,
                       pooled_ref, attn_ref, newk_ref, newv_ref, newp_ref, ext, pooled_buf):
    nseq, w_cache, s_new = SAMPLE_SEQS, WINDOW, q_ref.shape[0] // SAMPLE_SEQS
    keep = POOL_STATE - s_new
    for i in range(nseq):
        new = slice(i * s_new, (i + 1) * s_new)
        ext[1:1 + POOL_STATE, :] = st_ref[i]
        ext[1 + POOL_STATE:1 + POOL_STATE + s_new, :] = p_ref[new, :]
        _window_pool(ext, 1 + POOL_STATE, s_new, pooled_buf)
        pooled_ref[new, :] = pooled_buf[...].astype(pooled_ref.dtype)
        newp_ref[i, 0:keep, :] = st_ref[i, s_new:POOL_STATE, :]
        newp_ref[i, keep:POOL_STATE, :] = p_ref[new, :]
        newk_ref[i, 0:w_cache - s_new, :] = ck_ref[i, s_new:w_cache, :]
        newk_ref[i, w_cache - s_new:w_cache, :] = k_ref[new, :]
        newv_ref[i, 0:w_cache - s_new, :] = cv_ref[i, s_new:w_cache, :]
        newv_ref[i, w_cache - s_new:w_cache, :] = v_ref[new, :]

    kc = ck_ref[...].reshape(nseq * w_cache, KV_DIM).astype(BF16)
    vc = cv_ref[...].reshape(nseq * w_cache, KV_DIM).astype(BF16)
    kn = k_ref[...].astype(BF16)
    vn = v_ref[...].astype(BF16)
    q = q_ref[...]
    nq = q.shape[0]
    for kh in range(N_KV_HEADS):
        kvc = slice(kh * HEAD_DIM, (kh + 1) * HEAD_DIM)
        qg = jnp.concatenate([q[:, (kh * GROUP + g) * HEAD_DIM:(kh * GROUP + g + 1) * HEAD_DIM]
                              for g in range(GROUP)], axis=0)
        sc = lax.dot_general(qg, kc[:, kvc], _NT, preferred_element_type=F32) + biasc_ref[kh]
        sn = lax.dot_general(qg, kn[:, kvc], _NT, preferred_element_type=F32) + biasn_ref[kh]
        sink = jnp.concatenate([jnp.full((nq, 1), sink_ref[kh * GROUP + g], F32) for g in range(GROUP)], axis=0)
        m = jnp.maximum(jnp.maximum(jnp.max(sc, axis=-1, keepdims=True),
                                    jnp.max(sn, axis=-1, keepdims=True)), sink)
        ec = jnp.exp(sc - m)
        en = jnp.exp(sn - m)
        denom = (jnp.sum(ec, axis=-1, keepdims=True) + jnp.sum(en, axis=-1, keepdims=True)
                 + jnp.exp(sink - m))
        inv = 1.0 / denom
        o = (jnp.dot((ec * inv).astype(BF16), vc[:, kvc], preferred_element_type=F32)
             + jnp.dot((en * inv).astype(BF16), vn[:, kvc], preferred_element_type=F32))
        for g in range(GROUP):
            h = kh * GROUP + g
            attn_ref[:, h * HEAD_DIM:(h + 1) * HEAD_DIM] = o[g * nq:(g + 1) * nq].astype(attn_ref.dtype)


def _sample_mix(sinks, q, k, v, p, cache_k, cache_v, state, biasc, biasn, s_new):
    nb = cache_k.shape[0]
    rows = SAMPLE_SEQS * s_new
    tok = lambda c: pl.BlockSpec((rows, c), lambda i: (i, 0))
    seq3 = lambda r, c: pl.BlockSpec((SAMPLE_SEQS, r, c), lambda i: (i, 0, 0))
    const3 = lambda a: pl.BlockSpec(a.shape, lambda i: (0, 0, 0))
    return pl.pallas_call(
        _sample_mix_kernel,
        grid=(nb // SAMPLE_SEQS,),
        in_specs=[pl.BlockSpec(memory_space=pltpu.SMEM),
                  tok(Q_DIM), tok(KV_DIM), tok(KV_DIM), tok(POOL_WIDTH),
                  seq3(WINDOW, KV_DIM), seq3(WINDOW, KV_DIM), seq3(POOL_STATE, POOL_WIDTH),
                  const3(biasc), const3(biasn)],
        out_specs=[tok(POOL_WIDTH), tok(Q_DIM),
                   seq3(WINDOW, KV_DIM), seq3(WINDOW, KV_DIM), seq3(POOL_STATE, POOL_WIDTH)],
        out_shape=[jax.ShapeDtypeStruct((nb * s_new, POOL_WIDTH), F32),
                   jax.ShapeDtypeStruct((nb * s_new, Q_DIM), F32),
                   jax.ShapeDtypeStruct((nb, WINDOW, KV_DIM), F32),
                   jax.ShapeDtypeStruct((nb, WINDOW, KV_DIM), F32),
                   jax.ShapeDtypeStruct((nb, POOL_STATE, POOL_WIDTH), F32)],
        scratch_shapes=[pltpu.VMEM((1 + POOL_STATE + SUBLANES, POOL_WIDTH), F32),
                        pltpu.VMEM((s_new, POOL_WIDTH), F32)],
        compiler_params=_cparams(("parallel",)),
        name="sample_mix",
    )(sinks, q, k, v, p, cache_k, cache_v, state, biasc, biasn)


def _layer_norm(x, g, b):
    mu = jnp.mean(x, axis=-1, keepdims=True)
    xc = x - mu
    var = jnp.mean(xc * xc, axis=-1, keepdims=True)
    return xc * lax.rsqrt(var + LN_EPS) * g + b


def _merge_kernel(x_ref, pooled_ref, attn_ref, wg_ref, wmix_ref, scale_ref, wup_ref, wua_ref,
                  wout_ref, g_ref, b_ref, y_ref):
    x = x_ref[...]
    glog = jnp.dot(x.astype(BF16), wg_ref[...], preferred_element_type=F32)
    g_pool = jax.nn.sigmoid(glog[:, :D_MODEL])
    g_attn = jax.nn.sigmoid(glog[:, D_MODEL:])
    pooled = pooled_ref[...].astype(BF16)
    mixed = jnp.concatenate(
        [jnp.dot(pooled[:, g * POOL_GROUP_DIM:(g + 1) * POOL_GROUP_DIM], wmix_ref[g],
                 preferred_element_type=F32) for g in range(len(POOL_WINDOWS))], axis=1)
    pool_out = (mixed * scale_ref[...]).astype(BF16)
    a = jnp.dot(pool_out, wup_ref[...], preferred_element_type=F32)
    b = jnp.dot(attn_ref[...].astype(BF16), wua_ref[...], preferred_element_type=F32)
    m = (g_pool * a + g_attn * b).astype(BF16)
    r = jnp.dot(m, wout_ref[...], preferred_element_type=F32)
    y_ref[...] = _layer_norm(ALPHA * x + r, g_ref[...], b_ref[...])


def _merge(x, pooled, attn, wg, wmix, scale, wup, wua, wout, ln_g, ln_b):
    rows = x.shape[0]
    tm = min(TOK_TILE, rows)
    assert rows % tm == 0
    row = lambda c: pl.BlockSpec((tm, c), lambda i: (i, 0))
    full = lambda a: pl.BlockSpec(a.shape, lambda i: (0,) * a.ndim)
    weights = (wg, wmix, scale, wup, wua, wout, ln_g, ln_b)
    return pl.pallas_call(
        _merge_kernel,
        grid=(rows // tm,),
        in_specs=[row(D_MODEL), row(POOL_WIDTH), row(Q_DIM)] + [full(a) for a in weights],
        out_specs=row(D_MODEL),
        out_shape=jax.ShapeDtypeStruct((rows, D_MODEL), F32),
        compiler_params=_cparams(("parallel",)),
        name="merge",
    )(x, pooled, attn, *weights)


def _oddeven_merge_sort_pairs(n):
    pairs = []
    p = 1
    while p < n:
        k = p
        while k >= 1:
            for j in range(k % p, n - k, 2 * k):
                for i in range(min(k, n - j - k)):
                    if (i + j) // (2 * p) == (i + j + k) // (2 * p):
                        pairs.append((i + j, i + j + k))
            k //= 2
        p *= 2
    return pairs


def _bitonic_merge_pairs(n):
    pairs = []
    k = n // 2
    while k >= 1:
        pairs += [(i, i + k) for i in range(n) if not i & k]
        k //= 2
    return pairs


def _apply_network(vals, pairs):
    vals = list(vals)
    for i, j in pairs:
        a, b = vals[i], vals[j]
        if b is None:
            continue
        if a is None:
            vals[i], vals[j] = b, None
        else:
            vals[i], vals[j] = jnp.maximum(a, b), jnp.minimum(a, b)
    return vals


_SORT16 = _oddeven_merge_sort_pairs(PEER_TOPK)
_MERGE16 = _bitonic_merge_pairs(PEER_TOPK)


def _top16_rows(sc):
    n = sc.shape[0] // SUBLANES
    assert n == PEER_TOPK
    x = _apply_network([sc[k * SUBLANES:(k + 1) * SUBLANES, :] for k in range(n)], _SORT16)
    for shift in (4, 2, 1):
        y = [jnp.maximum(x[k], pltpu.roll(x[n - 1 - k], shift, 0)) for k in range(n)]
        x = _apply_network(y, _MERGE16)
    return x


def _best_sums(v1, v2):
    row = lax.broadcasted_iota(jnp.int32, v1[0].shape, 0)

    def one_per_sublane(vals):
        x = vals[SUBLANES - 1]
        for s in range(SUBLANES - 2, -1, -1):
            x = jnp.where(row == s, vals[s], x)
        return x

    lo, hi = one_per_sublane(v1[:SUBLANES]), one_per_sublane(v1[SUBLANES:])
    sums_lo = []
    for b in range(PEER_TOPK):
        n_valid = sum((a + 1) * (b + 1) <= PEER_TOPK for a in range(SUBLANES))
        s = lo + v2[b]
        sums_lo.append(s if n_valid == SUBLANES else jnp.where(row < n_valid, s, NEG))
    sums_hi = hi + v2[0]
    x, best = sums_hi, []
    for k in range(PEER_TOPK):
        best.append(jnp.maximum(sums_lo[k], x))
        x = jnp.minimum(sums_lo[k], x)
    for shift in (4, 2, 1):
        y = [jnp.maximum(best[k], pltpu.roll(best[PEER_TOPK - 1 - k], shift, 0)) for k in range(PEER_TOPK)]
        best = _apply_network(y, _MERGE16)
    return best, sums_lo, sums_hi


def _peer_scores_kernel(x_ref, wq_ref, keys_ref, cnt_ref, e1_ref, rank_ref, e2_ref):
    q = jnp.dot(x_ref[...].astype(BF16), wq_ref[...], preferred_element_type=F32).astype(BF16)
    for h in range(PEER_HEADS):
        sc, tops = [], []
        for c in range(2):
            col = (h * 2 + c) * PEER_HALF
            sc.append(lax.dot_general(keys_ref[h * 2 + c], q[:, col:col + PEER_HALF], _NT,
                                      preferred_element_type=F32))
            tops.append(_top16_rows(sc[c]))
        v1, v2 = tops
        best, sums_lo, sums_hi = _best_sums(v1, v2)
        tau = best[PEER_TOPK - 1]
        z = jnp.ones_like(tau)
        for r in range(1, PEER_TOPK):
            z = z + jnp.exp(best[r] - best[0])
        inv_z = 1.0 / z
        cnt_lo = jnp.zeros_like(tau)
        for s in sums_lo:
            cnt_lo = cnt_lo + jnp.where(s >= tau, 1.0, 0.0)
        cnt_hi = jnp.where(sums_hi >= tau, 1.0, 0.0)
        for t in range(tau.shape[1] // LANES):
            lanes = slice(t * LANES, (t + 1) * LANES)
            row = lambda r: r[0:1, lanes]
            s1, s2 = sc[0][:, lanes], sc[1][:, lanes]
            cnt = jnp.zeros_like(s1)
            rank = jnp.zeros_like(s2)
            for a in range(PEER_TOPK):
                cnt_a = (cnt_lo if a < SUBLANES else cnt_hi)[a % SUBLANES:a % SUBLANES + 1, lanes]
                cnt = jnp.where(s1 == row(v1[a]), cnt_a, cnt)
                rank = jnp.where(row(v2[a]) > s2, float(a + 1), rank)
            cnt_ref[h, t] = cnt
            e1_ref[h, t] = jnp.exp(s1 - row(v1[0])) * row(inv_z)
            rank_ref[h, t] = pltpu.bitcast(rank.astype(BF16), jnp.uint32)
            e2_ref[h, t] = pltpu.bitcast(jnp.exp(s2 - row(v2[0])).astype(BF16), jnp.uint32)


def _peer_scores(x, wq, keys):
    rows = x.shape[0]
    tm = min(TOK_TILE, rows)
    assert rows % tm == 0
    return pl.pallas_call(
        _peer_scores_kernel,
        grid=(rows // tm,),
        in_specs=[pl.BlockSpec((tm, D_MODEL), lambda i: (i, 0)),
                  pl.BlockSpec(wq.shape, lambda i: (0, 0)),
                  pl.BlockSpec(keys.shape, lambda i: (0, 0, 0))],
        out_specs=[pl.BlockSpec((PEER_HEADS, tm // LANES, n, LANES), lambda i: (0, i, 0, 0))
                   for n in (N_KEYS, N_KEYS, N_KEYS // 2, N_KEYS // 2)],
        out_shape=[jax.ShapeDtypeStruct((PEER_HEADS, rows // LANES, n, LANES), dt)
                   for n, dt in ((N_KEYS, F32), (N_KEYS, F32), (N_KEYS // 2, jnp.uint32), (N_KEYS // 2, jnp.uint32))],
        compiler_params=_cparams(("parallel",)),
        name="peer_scores",
    )(x, wq, keys)


def _gelu(x):
    return 0.5 * x * (1.0 + lax.erf(x * math.sqrt(0.5)))


def _peer_dense_kernel(x_ref, cnt_ref, e1_ref, rank_ref, e2_ref, u_ref, v_ref, g_ref, b_ref, y_ref,
                       xt, rowb, hid, wt, acc):
    c = pl.program_id(1)
    rows_per_chunk = EXPERT_CHUNK // N_KEYS
    ncol = xt.shape[1] // LANES
    packed = 2 * SUBLANES
    assert rows_per_chunk == SUBLANES

    @pl.when(c == 0)
    def _():
        xt[...] = x_ref[...].T.astype(BF16)
        acc[...] = jnp.zeros_like(acc)

    def build_rows(chunk):
        for h in range(PEER_HEADS):
            for col in range(ncol):
                for r in range(SUBLANES):
                    row = pl.ds(chunk * SUBLANES + r, packed, stride=0)
                    rowb[r, h, col] = cnt_ref[h, col, row, :].astype(BF16)
                    rowb[r, PEER_HEADS + h, col] = e1_ref[h, col, row, :].astype(BF16)

    def mask_row(il, carry):
        base = pl.multiple_of(il * N_KEYS, N_KEYS)
        for col in range(ncol):
            gate = [None] * (N_KEYS // packed)
            for h in range(PEER_HEADS):
                cnt = rowb[il, h, col]
                g1 = rowb[il, PEER_HEADS + h, col]
                for k in range(N_KEYS // packed):
                    words = slice(k * SUBLANES, (k + 1) * SUBLANES)
                    rank = pltpu.bitcast(rank_ref[h, col, words, :], BF16)
                    g2 = pltpu.bitcast(e2_ref[h, col, words, :], BF16)
                    g = jnp.where(rank < cnt, g2 * g1, jnp.zeros((), BF16))
                    gate[k] = g if gate[k] is None else gate[k] + g
            for k in range(N_KEYS // packed):
                act = _gelu(hid[col, pl.ds(base + k * packed, packed), :]).astype(BF16)
                wt[pl.ds(base + k * packed, packed), col * LANES:(col + 1) * LANES] = act * gate[k]
        return carry

    res = jnp.dot(u_ref[...].astype(BF16), xt[...], preferred_element_type=F32)
    for col in range(ncol):
        hid[col] = res[:, col * LANES:(col + 1) * LANES]
    build_rows(c)
    lax.fori_loop(0, rows_per_chunk, mask_row, 0)
    acc[...] += lax.dot_general(v_ref[...].astype(BF16), wt[...], (((0,), (0,)), ((), ())),
                                preferred_element_type=F32)

    @pl.when(c == pl.num_programs(1) - 1)
    def _():
        y_ref[...] = _layer_norm(ALPHA * x_ref[...] + acc[...].T, g_ref[...], b_ref[...])


def _peer_dense(x, cnt, e1, rank, e2, u, vt, ln_g, ln_b):
    rows = x.shape[0]
    tm = min(TOK_TILE, rows)
    assert rows % tm == 0
    n_exp = u.shape[0]
    assert n_exp % EXPERT_CHUNK == 0
    return pl.pallas_call(
        _peer_dense_kernel,
        grid=(rows // tm, n_exp // EXPERT_CHUNK),
        in_specs=[pl.BlockSpec((tm, D_MODEL), lambda t, c: (t, 0))]
                 + [pl.BlockSpec((PEER_HEADS, tm // LANES, n, LANES), lambda t, c: (0, t, 0, 0))
                    for n in (N_KEYS, N_KEYS, N_KEYS // 2, N_KEYS // 2)]
                 + [pl.BlockSpec((EXPERT_CHUNK, D_MODEL), lambda t, c: (c, 0)),
                    pl.BlockSpec((EXPERT_CHUNK, D_MODEL), lambda t, c: (c, 0)),
                    pl.BlockSpec((1, D_MODEL), lambda t, c: (0, 0)),
                    pl.BlockSpec((1, D_MODEL), lambda t, c: (0, 0))],
        out_specs=pl.BlockSpec((tm, D_MODEL), lambda t, c: (t, 0)),
        out_shape=jax.ShapeDtypeStruct((rows, D_MODEL), F32),
        scratch_shapes=[pltpu.VMEM((D_MODEL, tm), BF16),
                        pltpu.VMEM((SUBLANES, 2 * PEER_HEADS, tm // LANES, 2 * SUBLANES, LANES), BF16),
                        pltpu.VMEM((tm // LANES, EXPERT_CHUNK, LANES), F32),
                        pltpu.VMEM((EXPERT_CHUNK, tm), BF16),
                        pltpu.VMEM((D_MODEL, tm), F32)],
        compiler_params=_cparams(("parallel", "arbitrary")),
        name="peer_dense",
    )(x, cnt, e1, rank, e2, u, vt, ln_g, ln_b)


def _rel_bucket_np(dist):
    n = np.maximum(dist, 0)
    max_exact = NUM_BUCKETS // 2
    large = max_exact + (np.log(np.maximum(n, max_exact).astype(np.float32) / max_exact)
                         / math.log(MAX_DISTANCE / max_exact) * (NUM_BUCKETS - max_exact)).astype(np.int32)
    return np.where(n < max_exact, n, np.minimum(large, NUM_BUCKETS - 1))


def _bias_kernel(table_ref, bucket_ref, out_ref, *, head_stride):
    head = pl.program_id(0) * head_stride + pl.program_id(1)
    bucket = bucket_ref[0]
    acc = jnp.full(bucket.shape, NEG, F32)
    for b in range(NUM_BUCKETS):
        acc = jnp.where(bucket == b, table_ref[b, head], acc)
    out_ref[0, 0] = acc


def _bias_tiles(table, bucket, n_outer, n_inner, head_stride):
    r, c = bucket.shape[1:]
    bmap = (lambda a, b: (a, 0, 0)) if bucket.shape[0] > 1 else (lambda a, b: (0, 0, 0))
    return pl.pallas_call(
        functools.partial(_bias_kernel, head_stride=head_stride),
        grid=(n_outer, n_inner),
        in_specs=[pl.BlockSpec(memory_space=pltpu.SMEM), pl.BlockSpec((1, r, c), bmap)],
        out_specs=pl.BlockSpec((1, 1, r, c), lambda a, b: (a, b, 0, 0)),
        out_shape=jax.ShapeDtypeStruct((n_outer, n_inner, r, c), F32),
        compiler_params=_cparams(("parallel", "parallel")),
        name="bias_tiles",
    )(table, jnp.asarray(bucket, jnp.int32))


def _prompt_bias(table):
    qi = np.arange(ATTN_BLOCK)[:, None]
    kj = np.arange(2 * ATTN_BLOCK)[None, :]
    dist = ATTN_BLOCK + qi - kj
    ok = (dist >= 0) & (dist < WINDOW)
    ok = np.stack([ok, ok & (kj >= ATTN_BLOCK - N_META)])
    bucket = np.where(ok, _rel_bucket_np(dist)[None], -1)
    return _bias_tiles(table, bucket, 2, N_HEADS, 0)


def _sample_bias(table, s_new):
    s, t = np.meshgrid(np.arange(SAMPLE_SEQS), np.arange(s_new), indexing="ij")
    s, t = s.reshape(-1, 1), t.reshape(-1, 1)
    out = []
    for cols, offset in ((WINDOW, WINDOW), (s_new, 0)):
        s2, c = np.meshgrid(np.arange(SAMPLE_SEQS), np.arange(cols), indexing="ij")
        s2, c = s2.reshape(1, -1), c.reshape(1, -1)
        dist = offset + t - c + 0 * s2
        ok = (s == s2) & (dist >= 0) & (dist < WINDOW)
        bucket = np.where(ok, _rel_bucket_np(dist), -1)[None]
        tiles = _bias_tiles(table, bucket, N_KV_HEADS, GROUP, GROUP)
        out.append(tiles.reshape(N_KV_HEADS, GROUP * bucket.shape[1], bucket.shape[2]))
    return out


def kernel(x_prompt, x_sample, cache_k, cache_v, state_pool, meta_tokens, rel_bias_table, w_in,
           w_pool_mix, pool_scale, attn_sinks, w_up_pool, w_up_attn, w_out, ln1_g, ln1_b,
           peer_w_query, peer_sub_keys, peer_u, peer_v, ln2_g, ln2_b):
    batch, seq, d = x_prompt.shape
    nb, s_new, _ = x_sample.shape
    assert w_in.shape[0] == DEPTH and d == D_MODEL and seq % ATTN_TILE == 0
    assert N_META >= max(POOL_WINDOWS) - 1 and cache_k.shape[2] == WINDOW

    w_in0 = w_in[0].astype(BF16)
    w_pqkv, w_gates = w_in0[:, :OFF_GA], w_in0[:, OFF_GA:]
    wmix = w_pool_mix[0].astype(BF16)
    scale = pool_scale[0].reshape(1, POOL_WIDTH)
    wup, wua, wout = w_up_pool[0].astype(BF16), w_up_attn[0].astype(BF16), w_out[0].astype(BF16)
    g1, b1 = ln1_g[0].reshape(1, d), ln1_b[0].reshape(1, d)
    g2, b2 = ln2_g[0].reshape(1, d), ln2_b[0].reshape(1, d)
    wq = peer_w_query[0].astype(BF16)
    keys = peer_sub_keys[0].reshape(2 * PEER_HEADS, N_KEYS, PEER_HALF).astype(BF16)
    u = peer_u[0]
    vt = peer_v[0]
    sinks = attn_sinks[0].astype(F32)
    table = rel_bias_table.astype(F32)
    bias_p = _prompt_bias(table)
    bias_c, bias_n = _sample_bias(table, s_new)

    xp = x_prompt.reshape(batch * seq, d)
    xs = x_sample.reshape(nb * s_new, d)
    xm = jnp.concatenate([jnp.zeros((ATTN_BLOCK - N_META, d), F32), meta_tokens.astype(F32)], axis=0)

    p_p, q_p, k_p, v_p = _inproj(xp, w_pqkv)
    p_s, q_s, k_s, v_s = _inproj(xs, w_pqkv)
    p_m, _, k_m, v_m = _inproj(xm, w_pqkv)

    pooled_p, attn_p = _prompt_mix(sinks, q_p, k_p, v_p, p_p, k_m, v_m, p_m[ATTN_BLOCK - N_META:],
                                   bias_p, batch, seq)
    ck = cache_k[0].reshape(nb, WINDOW, KV_DIM)
    cv = cache_v[0].reshape(nb, WINDOW, KV_DIM)
    pooled_s, attn_s, newk, newv, newp = _sample_mix(sinks, q_s, k_s, v_s, p_s, ck, cv,
                                                     state_pool[0], bias_c, bias_n, s_new)

    outs = []
    for x, pooled, attn in ((xp, pooled_p, attn_p), (xs, pooled_s, attn_s)):
        x1 = _merge(x, pooled, attn, w_gates, wmix, scale, wup, wua, wout, g1, b1)
        cnt, e1, rank, e2 = _peer_scores(x1, wq, keys)
        outs.append(_peer_dense(x1, cnt, e1, rank, e2, u, vt, g2, b2))

    w_keep = min(WINDOW, seq + N_META)
    kv_shape = (batch, seq, N_KV_HEADS, HEAD_DIM)
    return (outs[0].reshape(batch, seq, d),
            outs[1].reshape(nb, s_new, d),
            k_p.reshape(kv_shape)[None, :, seq - w_keep:],
            v_p.reshape(kv_shape)[None, :, seq - w_keep:],
            p_p.reshape(batch, seq, POOL_WIDTH)[None, :, seq - POOL_STATE:],
            newk.reshape(1, nb, WINDOW, N_KV_HEADS, HEAD_DIM),
            newv.reshape(1, nb, WINDOW, N_KV_HEADS, HEAD_DIM),
            newp[None])
```

```python
import functools
import math

import jax
import jax.numpy as jnp
import numpy as np
from jax import lax
from jax.experimental import pallas as pl
from jax.experimental.pallas import tpu as pltpu

F32 = jnp.float32
BF16 = jnp.bfloat16

D_MODEL = 1024
N_META = 16
POOL_WIDTH = 512
POOL_WINDOWS = (2, 4, 8, 16)
POOL_GROUP_DIM = 128
POOL_STATE = 15
HEAD_DIM = 64
N_HEADS = 8
N_KV_HEADS = 2
GROUP = N_HEADS // N_KV_HEADS
WINDOW = 128
ATTN_BLOCK = 128
ATTN_SCALE = HEAD_DIM ** -0.5
NUM_BUCKETS = 32
MAX_DISTANCE = 128
Q_DIM = N_HEADS * HEAD_DIM
KV_DIM = N_KV_HEADS * HEAD_DIM
OFF_Q = POOL_WIDTH
OFF_K = OFF_Q + Q_DIM
OFF_V = OFF_K + KV_DIM
OFF_GA = OFF_V + KV_DIM
PEER_HEADS = 8
N_KEYS = 128
PEER_TOPK = 16
PEER_HALF = 128
DEPTH = 1
ALPHA = (2 * DEPTH) ** 0.25
LN_EPS = 1e-5
NEG = -1e30

LANES = 128
SUBLANES = 8
VMEM_LIMIT = 56 * 1024 * 1024

TOK_TILE = 512
ATTN_TILE = 512
SAMPLE_SEQS = 8
EXPERT_CHUNK = 1024


def _cparams(sem, flags=None):
    return pltpu.CompilerParams(dimension_semantics=sem, vmem_limit_bytes=VMEM_LIMIT, flags=flags)


def _inproj_kernel(x_ref, w_ref, p_ref, q_ref, k_ref, v_ref):
    z = jnp.dot(x_ref[...].astype(BF16), w_ref[...], preferred_element_type=F32)
    p_ref[...] = z[:, :OFF_Q]
    q_ref[...] = (z[:, OFF_Q:OFF_K] * ATTN_SCALE).astype(BF16)
    k_ref[...] = z[:, OFF_K:OFF_V]
    v_ref[...] = z[:, OFF_V:OFF_GA]


def _inproj(x, w_pqkv):
    rows = x.shape[0]
    tm = min(TOK_TILE, rows)
    assert rows % tm == 0
    row = lambda c: pl.BlockSpec((tm, c), lambda i: (i, 0))
    return pl.pallas_call(
        _inproj_kernel,
        grid=(rows // tm,),
        in_specs=[row(D_MODEL), pl.BlockSpec((D_MODEL, OFF_GA), lambda i: (0, 0))],
        out_specs=[row(POOL_WIDTH), row(Q_DIM), row(KV_DIM), row(KV_DIM)],
        out_shape=[jax.ShapeDtypeStruct((rows, POOL_WIDTH), F32),
                   jax.ShapeDtypeStruct((rows, Q_DIM), BF16),
                   jax.ShapeDtypeStruct((rows, KV_DIM), F32),
                   jax.ShapeDtypeStruct((rows, KV_DIM), F32)],
        compiler_params=_cparams(("parallel",)),
        name="inproj",
    )(x, w_pqkv)


def _window_pool(ext_ref, first, rows, out_ref):
    for g, w in enumerate(POOL_WINDOWS):
        cols = slice(g * POOL_GROUP_DIM, (g + 1) * POOL_GROUP_DIM)
        cur = ext_ref[first:first + rows, cols]
        acc = cur
        for r in range(1, w):
            acc = acc + ext_ref[first - r:first - r + rows, cols]
        out_ref[:, cols] = (acc * (1.0 / w) - cur).astype(out_ref.dtype)


def _window_pool_tall(ext_ref, first, rows, out_ref):
    for g, w in enumerate(POOL_WINDOWS):
        cols = slice(g * POOL_GROUP_DIM, (g + 1) * POOL_GROUP_DIM)
        x = ext_ref[:, cols]
        s, span = x, 1
        while span < w:
            s = s + pltpu.roll(s, span, 0)
            span *= 2
        out_ref[:, cols] = (s[first:first + rows] * (1.0 / w) - x[first:first + rows]).astype(out_ref.dtype)


def _sink_softmax(s, sink):
    m = jnp.maximum(jnp.max(s, axis=-1, keepdims=True), sink)
    e = jnp.exp(s - m)
    denom = jnp.sum(e, axis=-1, keepdims=True) + jnp.exp(sink - m)
    return e * (1.0 / denom)


_NT = (((1,), (1,)), ((), ()))


def _prompt_mix_kernel(sink_ref, q_ref, k_ref, v_ref, p_ref, kprev_ref, vprev_ref, pprev_ref,
                       mk_ref, mv_ref, mp_ref, bias_ref, pooled_ref, attn_ref,
                       kbuf, vbuf, pbuf, sbuf, prob_buf):
    first = pl.program_id(1) == 0
    hist = ATTN_BLOCK
    kbuf[0:hist, :] = jnp.where(first, mk_ref[...], kprev_ref[...]).astype(BF16)
    vbuf[0:hist, :] = jnp.where(first, mv_ref[...], vprev_ref[...]).astype(BF16)
    kbuf[hist:, :] = k_ref[...].astype(BF16)
    vbuf[hist:, :] = v_ref[...].astype(BF16)
    pbuf[0:N_META, :] = jnp.where(first, mp_ref[...], pprev_ref[...])
    pbuf[N_META:, :] = p_ref[...]

    _window_pool_tall(pbuf, N_META, ATTN_TILE, pooled_ref)

    first_i = jnp.where(first, 1, 0)
    blocks = [(j, kh) for j in range(ATTN_TILE // ATTN_BLOCK) for kh in range(N_KV_HEADS)]
    rows = lambda j: slice(j * ATTN_BLOCK, (j + 1) * ATTN_BLOCK)
    keys = lambda j: slice(j * ATTN_BLOCK, j * ATTN_BLOCK + 2 * ATTN_BLOCK)
    head = lambda h: slice(h * HEAD_DIM, (h + 1) * HEAD_DIM)
    for i, (j, kh) in enumerate(blocks):
        qg = jnp.concatenate([q_ref[rows(j), head(kh * GROUP + g)] for g in range(GROUP)], axis=0)
        s = lax.dot_general(qg, kbuf[keys(j), head(kh)], _NT, preferred_element_type=F32)
        sbuf[i] = s + (bias_ref[first_i, kh] if j == 0 else bias_ref[0, kh])
    for i, (j, kh) in enumerate(blocks):
        sink = jnp.concatenate([jnp.full((ATTN_BLOCK, 1), sink_ref[kh * GROUP + g], F32) for g in range(GROUP)], axis=0)
        prob_buf[i] = _sink_softmax(sbuf[i], sink).astype(BF16)
    for i, (j, kh) in enumerate(blocks):
        o = jnp.dot(prob_buf[i], vbuf[keys(j), head(kh)], preferred_element_type=F32)
        for g in range(GROUP):
            attn_ref[rows(j), head(kh * GROUP + g)] = o[g * ATTN_BLOCK:(g + 1) * ATTN_BLOCK].astype(attn_ref.dtype)


def _prompt_mix(sinks, q, k, v, p, mk, mv, mp, bias, batch, seq):
    nt = seq // ATTN_TILE
    blocks_per_tile = ATTN_TILE // ATTN_BLOCK
    cur = lambda c: pl.BlockSpec((ATTN_TILE, c), lambda b, s: (b * nt + s, 0))
    prev_blk = lambda b, s: (jnp.maximum((b * nt + s) * blocks_per_tile - 1, 0), 0)
    prev_p = lambda b, s: (jnp.maximum((b * nt + s) * (ATTN_TILE // N_META) - 1, 0), 0)
    const2 = lambda b, s: (0, 0)
    rows = batch * seq
    return pl.pallas_call(
        _prompt_mix_kernel,
        grid=(batch, nt),
        in_specs=[pl.BlockSpec(memory_space=pltpu.SMEM),
                  cur(Q_DIM), cur(KV_DIM), cur(KV_DIM), cur(POOL_WIDTH),
                  pl.BlockSpec((ATTN_BLOCK, KV_DIM), prev_blk),
                  pl.BlockSpec((ATTN_BLOCK, KV_DIM), prev_blk),
                  pl.BlockSpec((N_META, POOL_WIDTH), prev_p),
                  pl.BlockSpec((ATTN_BLOCK, KV_DIM), const2),
                  pl.BlockSpec((ATTN_BLOCK, KV_DIM), const2),
                  pl.BlockSpec((N_META, POOL_WIDTH), const2),
                  pl.BlockSpec((2, N_KV_HEADS, GROUP * ATTN_BLOCK, 2 * ATTN_BLOCK), lambda b, s: (0, 0, 0, 0))],
        out_specs=[cur(POOL_WIDTH), cur(Q_DIM)],
        out_shape=[jax.ShapeDtypeStruct((rows, POOL_WIDTH), BF16),
                   jax.ShapeDtypeStruct((rows, Q_DIM), BF16)],
        scratch_shapes=[pltpu.VMEM((ATTN_BLOCK + ATTN_TILE, KV_DIM), BF16),
                        pltpu.VMEM((ATTN_BLOCK + ATTN_TILE, KV_DIM), BF16),
                        pltpu.VMEM((N_META + ATTN_TILE, POOL_WIDTH), F32),
                        pltpu.VMEM((blocks_per_tile * N_KV_HEADS, GROUP * ATTN_BLOCK, 2 * ATTN_BLOCK), F32),
                        pltpu.VMEM((blocks_per_tile * N_KV_HEADS, GROUP * ATTN_BLOCK, 2 * ATTN_BLOCK), BF16)],
        compiler_params=_cparams(("parallel", "parallel")),
        name="prompt_mix",
    )(sinks, q, k, v, p, k, v, p, mk, mv, mp, bias)


def _sample_mix_kernel(sink_ref, q_ref, k_ref, v_ref, p_ref, ck_ref, cv_ref, st_ref,
                       biasc_ref, biasn_ref,
                       pooled_ref, attn_ref, newk_ref, newv_ref, newp_ref, ext, pooled_buf):
    nseq, w_cache, s_new = SAMPLE_SEQS, WINDOW, q_ref.shape[0] // SAMPLE_SEQS
    keep = POOL_STATE - s_new
    for i in range(nseq):
        new = slice(i * s_new, (i + 1) * s_new)
        ext[1:1 + POOL_STATE, :] = st_ref[i]
        ext[1 + POOL_STATE:1 + POOL_STATE + s_new, :] = p_ref[new, :]
        _window_pool(ext, 1 + POOL_STATE, s_new, pooled_buf)
        pooled_ref[new, :] = pooled_buf[...].astype(pooled_ref.dtype)
        newp_ref[i, 0:keep, :] = st_ref[i, s_new:POOL_STATE, :]
        newp_ref[i, keep:POOL_STATE, :] = p_ref[new, :]
        newk_ref[i, 0:w_cache - s_new, :] = ck_ref[i, s_new:w_cache, :]
        newk_ref[i, w_cache - s_new:w_cache, :] = k_ref[new, :]
        newv_ref[i, 0:w_cache - s_new, :] = cv_ref[i, s_new:w_cache, :]
        newv_ref[i, w_cache - s_new:w_cache, :] = v_ref[new, :]

    kc = ck_ref[...].reshape(nseq * w_cache, KV_DIM).astype(BF16)
    vc = cv_ref[...].reshape(nseq * w_cache, KV_DIM).astype(BF16)
    kn = k_ref[...].astype(BF16)
    vn = v_ref[...].astype(BF16)
    q = q_ref[...]
    nq = q.shape[0]
    for kh in range(N_KV_HEADS):
        kvc = slice(kh * HEAD_DIM, (kh + 1) * HEAD_DIM)
        qg = jnp.concatenate([q[:, (kh * GROUP + g) * HEAD_DIM:(kh * GROUP + g + 1) * HEAD_DIM]
                              for g in range(GROUP)], axis=0)
        sc = lax.dot_general(qg, kc[:, kvc], _NT, preferred_element_type=F32) + biasc_ref[kh]
        sn = lax.dot_general(qg, kn[:, kvc], _NT, preferred_element_type=F32) + biasn_ref[kh]
        sink = jnp.concatenate([jnp.full((nq, 1), sink_ref[kh * GROUP + g], F32) for g in range(GROUP)], axis=0)
        m = jnp.maximum(jnp.maximum(jnp.max(sc, axis=-1, keepdims=True),
                                    jnp.max(sn, axis=-1, keepdims=True)), sink)
        ec = jnp.exp(sc - m)
        en = jnp.exp(sn - m)
        denom = (jnp.sum(ec, axis=-1, keepdims=True) + jnp.sum(en, axis=-1, keepdims=True)
                 + jnp.exp(sink - m))
        inv = 1.0 / denom
        o = (jnp.dot((ec * inv).astype(BF16), vc[:, kvc], preferred_element_type=F32)
             + jnp.dot((en * inv).astype(BF16), vn[:, kvc], preferred_element_type=F32))
        for g in range(GROUP):
            h = kh * GROUP + g
            attn_ref[:, h * HEAD_DIM:(h + 1) * HEAD_DIM] = o[g * nq:(g + 1) * nq].astype(attn_ref.dtype)


def _sample_mix(sinks, q, k, v, p, cache_k, cache_v, state, biasc, biasn, s_new):
    nb = cache_k.shape[0]
    rows = SAMPLE_SEQS * s_new
    tok = lambda c: pl.BlockSpec((rows, c), lambda i: (i, 0))
    seq3 = lambda r, c: pl.BlockSpec((SAMPLE_SEQS, r, c), lambda i: (i, 0, 0))
    const3 = lambda a: pl.BlockSpec(a.shape, lambda i: (0, 0, 0))
    return pl.pallas_call(
        _sample_mix_kernel,
        grid=(nb // SAMPLE_SEQS,),
        in_specs=[pl.BlockSpec(memory_space=pltpu.SMEM),
                  tok(Q_DIM), tok(KV_DIM), tok(KV_DIM), tok(POOL_WIDTH),
                  seq3(WINDOW, KV_DIM), seq3(WINDOW, KV_DIM), seq3(POOL_STATE, POOL_WIDTH),
                  const3(biasc), const3(biasn)],
        out_specs=[tok(POOL_WIDTH), tok(Q_DIM),
                   seq3(WINDOW, KV_DIM), seq3(WINDOW, KV_DIM), seq3(POOL_STATE, POOL_WIDTH)],
        out_shape=[jax.ShapeDtypeStruct((nb * s_new, POOL_WIDTH), F32),
                   jax.ShapeDtypeStruct((nb * s_new, Q_DIM), F32),
                   jax.ShapeDtypeStruct((nb, WINDOW, KV_DIM), F32),
                   jax.ShapeDtypeStruct((nb, WINDOW, KV_DIM), F32),
                   jax.ShapeDtypeStruct((nb, POOL_STATE, POOL_WIDTH), F32)],
        scratch_shapes=[pltpu.VMEM((1 + POOL_STATE + SUBLANES, POOL_WIDTH), F32),
                        pltpu.VMEM((s_new, POOL_WIDTH), F32)],
        compiler_params=_cparams(("parallel",)),
        name="sample_mix",
    )(sinks, q, k, v, p, cache_k, cache_v, state, biasc, biasn)


def _layer_norm(x, g, b):
    mu = jnp.mean(x, axis=-1, keepdims=True)
    xc = x - mu
    var = jnp.mean(xc * xc, axis=-1, keepdims=True)
    return xc * lax.rsqrt(var + LN_EPS) * g + b


def _merge_kernel(x_ref, pooled_ref, attn_ref, wg_ref, wmix_ref, scale_ref, wup_ref, wua_ref,
                  wout_ref, g_ref, b_ref, y_ref):
    x = x_ref[...]
    glog = jnp.dot(x.astype(BF16), wg_ref[...], preferred_element_type=F32)
    g_pool = jax.nn.sigmoid(glog[:, :D_MODEL])
    g_attn = jax.nn.sigmoid(glog[:, D_MODEL:])
    pooled = pooled_ref[...].astype(BF16)
    mixed = jnp.concatenate(
        [jnp.dot(pooled[:, g * POOL_GROUP_DIM:(g + 1) * POOL_GROUP_DIM], wmix_ref[g],
                 preferred_element_type=F32) for g in range(len(POOL_WINDOWS))], axis=1)
    pool_out = (mixed * scale_ref[...]).astype(BF16)
    a = jnp.dot(pool_out, wup_ref[...], preferred_element_type=F32)
    b = jnp.dot(attn_ref[...].astype(BF16), wua_ref[...], preferred_element_type=F32)
    m = (g_pool * a + g_attn * b).astype(BF16)
    r = jnp.dot(m, wout_ref[...], preferred_element_type=F32)
    y_ref[...] = _layer_norm(ALPHA * x + r, g_ref[...], b_ref[...])


def _merge(x, pooled, attn, wg, wmix, scale, wup, wua, wout, ln_g, ln_b):
    rows = x.shape[0]
    tm = min(TOK_TILE, rows)
    assert rows % tm == 0
    row = lambda c: pl.BlockSpec((tm, c), lambda i: (i, 0))
    full = lambda a: pl.BlockSpec(a.shape, lambda i: (0,) * a.ndim)
    weights = (wg, wmix, scale, wup, wua, wout, ln_g, ln_b)
    return pl.pallas_call(
        _merge_kernel,
        grid=(rows // tm,),
        in_specs=[row(D_MODEL), row(POOL_WIDTH), row(Q_DIM)] + [full(a) for a in weights],
        out_specs=row(D_MODEL),
        out_shape=jax.ShapeDtypeStruct((rows, D_MODEL), F32),
        compiler_params=_cparams(("parallel",)),
        name="merge",
    )(x, pooled, attn, *weights)


def _oddeven_merge_sort_pairs(n):
    pairs = []
    p = 1
    while p < n:
        k = p
        while k >= 1:
            for j in range(k % p, n - k, 2 * k):
                for i in range(min(k, n - j - k)):
                    if (i + j) // (2 * p) == (i + j + k) // (2 * p):
                        pairs.append((i + j, i + j + k))
            k //= 2
        p *= 2
    return pairs


def _bitonic_merge_pairs(n):
    pairs = []
    k = n // 2
    while k >= 1:
        pairs += [(i, i + k) for i in range(n) if not i & k]
        k //= 2
    return pairs


def _apply_network(vals, pairs):
    vals = list(vals)
    for i, j in pairs:
        a, b = vals[i], vals[j]
        if b is None:
            continue
        if a is None:
            vals[i], vals[j] = b, None
        else:
            vals[i], vals[j] = jnp.maximum(a, b), jnp.minimum(a, b)
    return vals


_SORT16 = _oddeven_merge_sort_pairs(PEER_TOPK)
_MERGE16 = _bitonic_merge_pairs(PEER_TOPK)


def _top16_rows(sc):
    n = sc.shape[0] // SUBLANES
    assert n == PEER_TOPK
    x = _apply_network([sc[k * SUBLANES:(k + 1) * SUBLANES, :] for k in range(n)], _SORT16)
    for shift in (4, 2, 1):
        y = [jnp.maximum(x[k], pltpu.roll(x[n - 1 - k], shift, 0)) for k in range(n)]
        x = _apply_network(y, _MERGE16)
    return x


def _best_sums(v1, v2):
    row = lax.broadcasted_iota(jnp.int32, v1[0].shape, 0)

    def one_per_sublane(vals):
        x = vals[SUBLANES - 1]
        for s in range(SUBLANES - 2, -1, -1):
            x = jnp.where(row == s, vals[s], x)
        return x

    lo, hi = one_per_sublane(v1[:SUBLANES]), one_per_sublane(v1[SUBLANES:])
    sums_lo = []
    for b in range(PEER_TOPK):
        n_valid = sum((a + 1) * (b + 1) <= PEER_TOPK for a in range(SUBLANES))
        s = lo + v2[b]
        sums_lo.append(s if n_valid == SUBLANES else jnp.where(row < n_valid, s, NEG))
    sums_hi = hi + v2[0]
    x, best = sums_hi, []
    for k in range(PEER_TOPK):
        best.append(jnp.maximum(sums_lo[k], x))
        x = jnp.minimum(sums_lo[k], x)
    for shift in (4, 2, 1):
        y = [jnp.maximum(best[k], pltpu.roll(best[PEER_TOPK - 1 - k], shift, 0)) for k in range(PEER_TOPK)]
        best = _apply_network(y, _MERGE16)
    return best, sums_lo, sums_hi


def _peer_scores_kernel(x_ref, wq_ref, keys_ref, cnt_ref, e1_ref, rank_ref, e2_ref):
    q = jnp.dot(x_ref[...].astype(BF16), wq_ref[...], preferred_element_type=F32).astype(BF16)
    for h in range(PEER_HEADS):
        sc, tops = [], []
        for c in range(2):
            col = (h * 2 + c) * PEER_HALF
            sc.append(lax.dot_general(keys_ref[h * 2 + c], q[:, col:col + PEER_HALF], _NT,
                                      preferred_element_type=F32))
            tops.append(_top16_rows(sc[c]))
        v1, v2 = tops
        best, sums_lo, sums_hi = _best_sums(v1, v2)
        tau = best[PEER_TOPK - 1]
        z = jnp.ones_like(tau)
        for r in range(1, PEER_TOPK):
            z = z + jnp.exp(best[r] - best[0])
        inv_z = 1.0 / z
        cnt_lo = jnp.zeros_like(tau)
        for s in sums_lo:
            cnt_lo = cnt_lo + jnp.where(s >= tau, 1.0, 0.0)
        cnt_hi = jnp.where(sums_hi >= tau, 1.0, 0.0)
        for t in range(tau.shape[1] // LANES):
            lanes = slice(t * LANES, (t + 1) * LANES)
            row = lambda r: r[0:1, lanes]
            s1, s2 = sc[0][:, lanes], sc[1][:, lanes]
            cnt = jnp.zeros_like(s1)
            rank = jnp.zeros_like(s2)
            for a in range(PEER_TOPK):
                cnt_a = (cnt_lo if a < SUBLANES else cnt_hi)[a % SUBLANES:a % SUBLANES + 1, lanes]
                cnt = jnp.where(s1 == row(v1[a]), cnt_a, cnt)
                rank = jnp.where(row(v2[a]) > s2, float(a + 1), rank)
            cnt_ref[h, t] = cnt
            e1_ref[h, t] = jnp.exp(s1 - row(v1[0])) * row(inv_z)
            rank_ref[h, t] = pltpu.bitcast(rank.astype(BF16), jnp.uint32)
            e2_ref[h, t] = pltpu.bitcast(jnp.exp(s2 - row(v2[0])).astype(BF16), jnp.uint32)


def _peer_scores(x, wq, keys):
    rows = x.shape[0]
    tm = min(TOK_TILE, rows)
    assert rows % tm == 0
    return pl.pallas_call(
        _peer_scores_kernel,
        grid=(rows // tm,),
        in_specs=[pl.BlockSpec((tm, D_MODEL), lambda i: (i, 0)),
                  pl.BlockSpec(wq.shape, lambda i: (0, 0)),
                  pl.BlockSpec(keys.shape, lambda i: (0, 0, 0))],
        out_specs=[pl.BlockSpec((PEER_HEADS, tm // LANES, n, LANES), lambda i: (0, i, 0, 0))
                   for n in (N_KEYS, N_KEYS, N_KEYS // 2, N_KEYS // 2)],
        out_shape=[jax.ShapeDtypeStruct((PEER_HEADS, rows // LANES, n, LANES), dt)
                   for n, dt in ((N_KEYS, F32), (N_KEYS, F32), (N_KEYS // 2, jnp.uint32), (N_KEYS // 2, jnp.uint32))],
        compiler_params=_cparams(("parallel",)),
        name="peer_scores",
    )(x, wq, keys)


def _gelu(x):
    return 0.5 * x * (1.0 + lax.erf(x * math.sqrt(0.5)))


def _peer_dense_kernel(x_ref, cnt_ref, e1_ref, rank_ref, e2_ref, u_ref, v_ref, g_ref, b_ref, y_ref,
                       xt, rowb, hid, wt, acc):
    c = pl.program_id(1)
    rows_per_chunk = EXPERT_CHUNK // N_KEYS
    ncol = xt.shape[1] // LANES
    packed = 2 * SUBLANES
    assert rows_per_chunk == SUBLANES

    @pl.when(c == 0)
    def _():
        xt[...] = x_ref[...].T.astype(BF16)
        acc[...] = jnp.zeros_like(acc)

    def build_rows(chunk, heads):
        for h in heads:
            for col in range(ncol):
                for r in range(SUBLANES):
                    row = pl.ds(chunk * SUBLANES + r, packed, stride=0)
                    rowb[r, h, col] = cnt_ref[h, col, row, :].astype(BF16)
                    rowb[r, PEER_HEADS + h, col] = e1_ref[h, col, row, :].astype(BF16)

    def mask_row(il, carry):
        base = pl.multiple_of(il * N_KEYS, N_KEYS)
        for col in range(ncol):
            gate = [None] * (N_KEYS // packed)
            for h in range(PEER_HEADS):
                cnt = rowb[il, h, col]
                g1 = rowb[il, PEER_HEADS + h, col]
                for k in range(N_KEYS // packed):
                    words = slice(k * SUBLANES, (k + 1) * SUBLANES)
                    rank = pltpu.bitcast(rank_ref[h, col, words, :], BF16)
                    g2 = pltpu.bitcast(e2_ref[h, col, words, :], BF16)
                    g = jnp.where(rank < cnt, g2 * g1, jnp.zeros((), BF16))
                    gate[k] = g if gate[k] is None else gate[k] + g
            for k in range(N_KEYS // packed):
                act = _gelu(hid[col, pl.ds(base + k * packed, packed), :]).astype(BF16)
                wt[pl.ds(base + k * packed, packed), col * LANES:(col + 1) * LANES] = act * gate[k]
        return carry

    n_slabs = 4
    slab = EXPERT_CHUNK // n_slabs

    for i in range(n_slabs):
        rows = slice(i * slab, (i + 1) * slab)
        res = jnp.dot(u_ref[rows, :].astype(BF16), xt[...], preferred_element_type=F32)
        for col in range(ncol):
            hid[col, rows, :] = res[:, col * LANES:(col + 1) * LANES]
        build_rows(c, range(i * PEER_HEADS // n_slabs, (i + 1) * PEER_HEADS // n_slabs))
    lax.fori_loop(0, rows_per_chunk, mask_row, 0)
    acc[...] += lax.dot_general(v_ref[...].astype(BF16), wt[...], (((0,), (0,)), ((), ())),
                                preferred_element_type=F32)

    @pl.when(c == pl.num_programs(1) - 1)
    def _():
        y_ref[...] = _layer_norm(ALPHA * x_ref[...] + acc[...].T, g_ref[...], b_ref[...])


def _peer_dense(x, cnt, e1, rank, e2, u, vt, ln_g, ln_b):
    rows = x.shape[0]
    tm = min(TOK_TILE, rows)
    assert rows % tm == 0
    n_exp = u.shape[0]
    assert n_exp % EXPERT_CHUNK == 0
    return pl.pallas_call(
        _peer_dense_kernel,
        grid=(rows // tm, n_exp // EXPERT_CHUNK),
        in_specs=[pl.BlockSpec((tm, D_MODEL), lambda t, c: (t, 0))]
                 + [pl.BlockSpec((PEER_HEADS, tm // LANES, n, LANES), lambda t, c: (0, t, 0, 0))
                    for n in (N_KEYS, N_KEYS, N_KEYS // 2, N_KEYS // 2)]
                 + [pl.BlockSpec((EXPERT_CHUNK, D_MODEL), lambda t, c: (c, 0)),
                    pl.BlockSpec((EXPERT_CHUNK, D_MODEL), lambda t, c: (c, 0)),
                    pl.BlockSpec((1, D_MODEL), lambda t, c: (0, 0)),
                    pl.BlockSpec((1, D_MODEL), lambda t, c: (0, 0))],
        out_specs=pl.BlockSpec((tm, D_MODEL), lambda t, c: (t, 0)),
        out_shape=jax.ShapeDtypeStruct((rows, D_MODEL), F32),
        scratch_shapes=[pltpu.VMEM((D_MODEL, tm), BF16),
                        pltpu.VMEM((SUBLANES, 2 * PEER_HEADS, tm // LANES, 2 * SUBLANES, LANES), BF16),
                        pltpu.VMEM((tm // LANES, EXPERT_CHUNK, LANES), F32),
                        pltpu.VMEM((EXPERT_CHUNK, tm), BF16),
                        pltpu.VMEM((D_MODEL, tm), F32)],
        compiler_params=_cparams(("parallel", "arbitrary")),
        name="peer_dense",
    )(x, cnt, e1, rank, e2, u, vt, ln_g, ln_b)


def _rel_bucket_np(dist):
    n = np.maximum(dist, 0)
    max_exact = NUM_BUCKETS // 2
    large = max_exact + (np.log(np.maximum(n, max_exact).astype(np.float32) / max_exact)
                         / math.log(MAX_DISTANCE / max_exact) * (NUM_BUCKETS - max_exact)).astype(np.int32)
    return np.where(n < max_exact, n, np.minimum(large, NUM_BUCKETS - 1))


def _bias_kernel(table_ref, bucket_ref, out_ref, *, head_stride):
    head = pl.program_id(0) * head_stride + pl.program_id(1)
    bucket = bucket_ref[0]
    acc = jnp.full(bucket.shape, NEG, F32)
    for b in range(NUM_BUCKETS):
        acc = jnp.where(bucket == b, table_ref[b, head], acc)
    out_ref[0, 0] = acc


def _bias_tiles(table, bucket, n_outer, n_inner, head_stride):
    r, c = bucket.shape[1:]
    bmap = (lambda a, b: (a, 0, 0)) if bucket.shape[0] > 1 else (lambda a, b: (0, 0, 0))
    return pl.pallas_call(
        functools.partial(_bias_kernel, head_stride=head_stride),
        grid=(n_outer, n_inner),
        in_specs=[pl.BlockSpec(memory_space=pltpu.SMEM), pl.BlockSpec((1, r, c), bmap)],
        out_specs=pl.BlockSpec((1, 1, r, c), lambda a, b: (a, b, 0, 0)),
        out_shape=jax.ShapeDtypeStruct((n_outer, n_inner, r, c), F32),
        compiler_params=_cparams(("parallel", "parallel")),
        name="bias_tiles",
    )(table, jnp.asarray(bucket, jnp.int32))


def _prompt_bias(table):
    qi = np.arange(ATTN_BLOCK)[:, None]
    kj = np.arange(2 * ATTN_BLOCK)[None, :]
    dist = ATTN_BLOCK + qi - kj
    ok = (dist >= 0) & (dist < WINDOW)
    ok = np.stack([ok, ok & (kj >= ATTN_BLOCK - N_META)])
    bucket = np.where(ok, _rel_bucket_np(dist)[None], -1)
    tiles = _bias_tiles(table, bucket, 2, N_HEADS, 0)
    return tiles.reshape(2, N_KV_HEADS, GROUP * ATTN_BLOCK, 2 * ATTN_BLOCK)


def _sample_bias(table, s_new):
    s, t = np.meshgrid(np.arange(SAMPLE_SEQS), np.arange(s_new), indexing="ij")
    s, t = s.reshape(-1, 1), t.reshape(-1, 1)
    out = []
    for cols, offset in ((WINDOW, WINDOW), (s_new, 0)):
        s2, c = np.meshgrid(np.arange(SAMPLE_SEQS), np.arange(cols), indexing="ij")
        s2, c = s2.reshape(1, -1), c.reshape(1, -1)
        dist = offset + t - c + 0 * s2
        ok = (s == s2) & (dist >= 0) & (dist < WINDOW)
        bucket = np.where(ok, _rel_bucket_np(dist), -1)[None]
        tiles = _bias_tiles(table, bucket, N_KV_HEADS, GROUP, GROUP)
        out.append(tiles.reshape(N_KV_HEADS, GROUP * bucket.shape[1], bucket.shape[2]))
    return out


def kernel(x_prompt, x_sample, cache_k, cache_v, state_pool, meta_tokens, rel_bias_table, w_in,
           w_pool_mix, pool_scale, attn_sinks, w_up_pool, w_up_attn, w_out, ln1_g, ln1_b,
           peer_w_query, peer_sub_keys, peer_u, peer_v, ln2_g, ln2_b):
    batch, seq, d = x_prompt.shape
    nb, s_new, _ = x_sample.shape
    assert w_in.shape[0] == DEPTH and d == D_MODEL and seq % ATTN_TILE == 0
    assert N_META >= max(POOL_WINDOWS) - 1 and cache_k.shape[2] == WINDOW

    w_in0 = w_in[0].astype(BF16)
    w_pqkv, w_gates = w_in0[:, :OFF_GA], w_in0[:, OFF_GA:]
    wmix = w_pool_mix[0].astype(BF16)
    scale = pool_scale[0].reshape(1, POOL_WIDTH)
    wup, wua, wout = w_up_pool[0].astype(BF16), w_up_attn[0].astype(BF16), w_out[0].astype(BF16)
    g1, b1 = ln1_g[0].reshape(1, d), ln1_b[0].reshape(1, d)
    g2, b2 = ln2_g[0].reshape(1, d), ln2_b[0].reshape(1, d)
    wq = peer_w_query[0].astype(BF16)
    keys = peer_sub_keys[0].reshape(2 * PEER_HEADS, N_KEYS, PEER_HALF).astype(BF16)
    u = peer_u[0]
    vt = peer_v[0]
    sinks = attn_sinks[0].astype(F32)
    table = rel_bias_table.astype(F32)
    bias_p = _prompt_bias(table)
    bias_c, bias_n = _sample_bias(table, s_new)

    xp = x_prompt.reshape(batch * seq, d)
    xs = x_sample.reshape(nb * s_new, d)
    xm = jnp.concatenate([jnp.zeros((ATTN_BLOCK - N_META, d), F32), meta_tokens.astype(F32)], axis=0)

    p_p, q_p, k_p, v_p = _inproj(xp, w_pqkv)
    p_s, q_s, k_s, v_s = _inproj(xs, w_pqkv)
    p_m, _, k_m, v_m = _inproj(xm, w_pqkv)

    pooled_p, attn_p = _prompt_mix(sinks, q_p, k_p, v_p, p_p, k_m, v_m, p_m[ATTN_BLOCK - N_META:],
                                   bias_p, batch, seq)
    ck = cache_k[0].reshape(nb, WINDOW, KV_DIM)
    cv = cache_v[0].reshape(nb, WINDOW, KV_DIM)
    pooled_s, attn_s, newk, newv, newp = _sample_mix(sinks, q_s, k_s, v_s, p_s, ck, cv,
                                                     state_pool[0], bias_c, bias_n, s_new)

    outs = []
    for x, pooled, attn in ((xp, pooled_p, attn_p), (xs, pooled_s, attn_s)):
        x1 = _merge(x, pooled, attn, w_gates, wmix, scale, wup, wua, wout, g1, b1)
        cnt, e1, rank, e2 = _peer_scores(x1, wq, keys)
        outs.append(_peer_dense(x1, cnt, e1, rank, e2, u, vt, g2, b2))

    w_keep = min(WINDOW, seq + N_META)
    kv_shape = (batch, seq, N_KV_HEADS, HEAD_DIM)
    return (outs[0].reshape(batch, seq, d),
            outs[1].reshape(nb, s_new, d),
            k_p.reshape(kv_shape)[None, :, seq - w_keep:],
            v_p.reshape(kv_shape)[None, :, seq - w_keep:],
            p_p.reshape(batch, seq, POOL_WIDTH)[None, :, seq - POOL_STATE:],
            newk.reshape(1, nb, WINDOW, N_KV_HEADS, HEAD_DIM),
            newv.reshape(1, nb, WINDOW, N_KV_HEADS, HEAD_DIM),
            newp[None])
```

```python
import functools
import math

import jax
import jax.numpy as jnp
import numpy as np
from jax import lax
from jax.experimental import pallas as pl
from jax.experimental.pallas import tpu as pltpu

F32 = jnp.float32
BF16 = jnp.bfloat16

D_MODEL = 1024
N_META = 16
POOL_WIDTH = 512
POOL_WINDOWS = (2, 4, 8, 16)
POOL_GROUP_DIM = 128
POOL_STATE = 15
HEAD_DIM = 64
N_HEADS = 8
N_KV_HEADS = 2
GROUP = N_HEADS // N_KV_HEADS
WINDOW = 128
ATTN_BLOCK = 128
ATTN_SCALE = HEAD_DIM ** -0.5
NUM_BUCKETS = 32
MAX_DISTANCE = 128
Q_DIM = N_HEADS * HEAD_DIM
KV_DIM = N_KV_HEADS * HEAD_DIM
OFF_Q = POOL_WIDTH
OFF_K = OFF_Q + Q_DIM
OFF_V = OFF_K + KV_DIM
OFF_GA = OFF_V + KV_DIM
PEER_HEADS = 8
N_KEYS = 128
PEER_TOPK = 16
PEER_HALF = 128
DEPTH = 1
ALPHA = (2 * DEPTH) ** 0.25
LN_EPS = 1e-5
NEG = -1e30

LANES = 128
SUBLANES = 8
VMEM_LIMIT = 56 * 1024 * 1024

TOK_TILE = 512
ATTN_TILE = 512
SAMPLE_SEQS = 8
EXPERT_CHUNK = 1024


def _cparams(sem, flags=None):
    return pltpu.CompilerParams(dimension_semantics=sem, vmem_limit_bytes=VMEM_LIMIT, flags=flags)


def _inproj_kernel(x_ref, w_ref, p_ref, q_ref, k_ref, v_ref):
    z = jnp.dot(x_ref[...].astype(BF16), w_ref[...], preferred_element_type=F32)
    p_ref[...] = z[:, :OFF_Q]
    q_ref[...] = (z[:, OFF_Q:OFF_K] * ATTN_SCALE).astype(BF16)
    k_ref[...] = z[:, OFF_K:OFF_V]
    v_ref[...] = z[:, OFF_V:OFF_GA]


def _inproj(x, w_pqkv):
    rows = x.shape[0]
    tm = min(TOK_TILE, rows)
    assert rows % tm == 0
    row = lambda c: pl.BlockSpec((tm, c), lambda i: (i, 0))
    return pl.pallas_call(
        _inproj_kernel,
        grid=(rows // tm,),
        in_specs=[row(D_MODEL), pl.BlockSpec((D_MODEL, OFF_GA), lambda i: (0, 0))],
        out_specs=[row(POOL_WIDTH), row(Q_DIM), row(KV_DIM), row(KV_DIM)],
        out_shape=[jax.ShapeDtypeStruct((rows, POOL_WIDTH), F32),
                   jax.ShapeDtypeStruct((rows, Q_DIM), BF16),
                   jax.ShapeDtypeStruct((rows, KV_DIM), F32),
                   jax.ShapeDtypeStruct((rows, KV_DIM), F32)],
        compiler_params=_cparams(("parallel",)),
        name="inproj",
    )(x, w_pqkv)


def _window_pool(ext_ref, first, rows, out_ref):
    for g, w in enumerate(POOL_WINDOWS):
        cols = slice(g * POOL_GROUP_DIM, (g + 1) * POOL_GROUP_DIM)
        cur = ext_ref[first:first + rows, cols]
        acc = cur
        for r in range(1, w):
            acc = acc + ext_ref[first - r:first - r + rows, cols]
        out_ref[:, cols] = (acc * (1.0 / w) - cur).astype(out_ref.dtype)


def _window_pool_tall(ext_ref, first, rows, out_ref):
    for g, w in enumerate(POOL_WINDOWS):
        cols = slice(g * POOL_GROUP_DIM, (g + 1) * POOL_GROUP_DIM)
        x = ext_ref[:, cols]
        s, span = x, 1
        while span < w:
            s = s + pltpu.roll(s, span, 0)
            span *= 2
        out_ref[:, cols] = (s[first:first + rows] * (1.0 / w) - x[first:first + rows]).astype(out_ref.dtype)


def _sink_softmax(s, sink):
    m = jnp.maximum(jnp.max(s, axis=-1, keepdims=True), sink)
    e = jnp.exp(s - m)
    denom = jnp.sum(e, axis=-1, keepdims=True) + jnp.exp(sink - m)
    return e * (1.0 / denom)


_NT = (((1,), (1,)), ((), ()))


def _prompt_mix_kernel(sink_ref, q_ref, k_ref, v_ref, p_ref, kprev_ref, vprev_ref, pprev_ref,
                       mk_ref, mv_ref, mp_ref, bias_ref, pooled_ref, attn_ref,
                       kbuf, vbuf, pbuf, sbuf, prob_buf):
    first = pl.program_id(1) == 0
    hist = ATTN_BLOCK
    kbuf[0:hist, :] = jnp.where(first, mk_ref[...], kprev_ref[...]).astype(BF16)
    vbuf[0:hist, :] = jnp.where(first, mv_ref[...], vprev_ref[...]).astype(BF16)
    kbuf[hist:, :] = k_ref[...].astype(BF16)
    vbuf[hist:, :] = v_ref[...].astype(BF16)
    pbuf[0:N_META, :] = jnp.where(first, mp_ref[...], pprev_ref[...])
    pbuf[N_META:, :] = p_ref[...]

    _window_pool_tall(pbuf, N_META, ATTN_TILE, pooled_ref)

    first_i = jnp.where(first, 1, 0)
    blocks = [(j, kh) for j in range(ATTN_TILE // ATTN_BLOCK) for kh in range(N_KV_HEADS)]
    rows = lambda j: slice(j * ATTN_BLOCK, (j + 1) * ATTN_BLOCK)
    keys = lambda j: slice(j * ATTN_BLOCK, j * ATTN_BLOCK + 2 * ATTN_BLOCK)
    head = lambda h: slice(h * HEAD_DIM, (h + 1) * HEAD_DIM)
    for i, (j, kh) in enumerate(blocks):
        qg = jnp.concatenate([q_ref[rows(j), head(kh * GROUP + g)] for g in range(GROUP)], axis=0)
        s = lax.dot_general(qg, kbuf[keys(j), head(kh)], _NT, preferred_element_type=F32)
        sbuf[i] = s + (bias_ref[first_i, kh] if j == 0 else bias_ref[0, kh])
    for i, (j, kh) in enumerate(blocks):
        sink = jnp.concatenate([jnp.full((ATTN_BLOCK, 1), sink_ref[kh * GROUP + g], F32) for g in range(GROUP)], axis=0)
        prob_buf[i] = _sink_softmax(sbuf[i], sink).astype(BF16)
    for i, (j, kh) in enumerate(blocks):
        o = jnp.dot(prob_buf[i], vbuf[keys(j), head(kh)], preferred_element_type=F32)
        for g in range(GROUP):
            attn_ref[rows(j), head(kh * GROUP + g)] = o[g * ATTN_BLOCK:(g + 1) * ATTN_BLOCK].astype(attn_ref.dtype)


def _prompt_mix(sinks, q, k, v, p, mk, mv, mp, bias, batch, seq):
    nt = seq // ATTN_TILE
    blocks_per_tile = ATTN_TILE // ATTN_BLOCK
    cur = lambda c: pl.BlockSpec((ATTN_TILE, c), lambda b, s: (b * nt + s, 0))
    prev_blk = lambda b, s: (jnp.maximum((b * nt + s) * blocks_per_tile - 1, 0), 0)
    prev_p = lambda b, s: (jnp.maximum((b * nt + s) * (ATTN_TILE // N_META) - 1, 0), 0)
    const2 = lambda b, s: (0, 0)
    rows = batch * seq
    return pl.pallas_call(
        _prompt_mix_kernel,
        grid=(batch, nt),
        in_specs=[pl.BlockSpec(memory_space=pltpu.SMEM),
                  cur(Q_DIM), cur(KV_DIM), cur(KV_DIM), cur(POOL_WIDTH),
                  pl.BlockSpec((ATTN_BLOCK, KV_DIM), prev_blk),
                  pl.BlockSpec((ATTN_BLOCK, KV_DIM), prev_blk),
                  pl.BlockSpec((N_META, POOL_WIDTH), prev_p),
                  pl.BlockSpec((ATTN_BLOCK, KV_DIM), const2),
                  pl.BlockSpec((ATTN_BLOCK, KV_DIM), const2),
                  pl.BlockSpec((N_META, POOL_WIDTH), const2),
                  pl.BlockSpec((2, N_KV_HEADS, GROUP * ATTN_BLOCK, 2 * ATTN_BLOCK), lambda b, s: (0, 0, 0, 0))],
        out_specs=[cur(POOL_WIDTH), cur(Q_DIM)],
        out_shape=[jax.ShapeDtypeStruct((rows, POOL_WIDTH), BF16),
                   jax.ShapeDtypeStruct((rows, Q_DIM), BF16)],
        scratch_shapes=[pltpu.VMEM((ATTN_BLOCK + ATTN_TILE, KV_DIM), BF16),
                        pltpu.VMEM((ATTN_BLOCK + ATTN_TILE, KV_DIM), BF16),
                        pltpu.VMEM((N_META + ATTN_TILE, POOL_WIDTH), F32),
                        pltpu.VMEM((blocks_per_tile * N_KV_HEADS, GROUP * ATTN_BLOCK, 2 * ATTN_BLOCK), F32),
                        pltpu.VMEM((blocks_per_tile * N_KV_HEADS, GROUP * ATTN_BLOCK, 2 * ATTN_BLOCK), BF16)],
        compiler_params=_cparams(("parallel", "parallel")),
        name="prompt_mix",
    )(sinks, q, k, v, p, k, v, p, mk, mv, mp, bias)


def _sample_mix_kernel(sink_ref, q_ref, k_ref, v_ref, p_ref, ck_ref, cv_ref, st_ref,
                       biasc_ref, biasn_ref,
                       pooled_ref, attn_ref, newk_ref, newv_ref, newp_ref, ext, pooled_buf):
    nseq, w_cache, s_new = SAMPLE_SEQS, WINDOW, q_ref.shape[0] // SAMPLE_SEQS
    keep = POOL_STATE - s_new
    for i in range(nseq):
        new = slice(i * s_new, (i + 1) * s_new)
        ext[1:1 + POOL_STATE, :] = st_ref[i]
        ext[1 + POOL_STATE:1 + POOL_STATE + s_new, :] = p_ref[new, :]
        _window_pool(ext, 1 + POOL_STATE, s_new, pooled_buf)
        pooled_ref[new, :] = pooled_buf[...].astype(pooled_ref.dtype)
        newp_ref[i, 0:keep, :] = st_ref[i, s_new:POOL_STATE, :]
        newp_ref[i, keep:POOL_STATE, :] = p_ref[new, :]
        newk_ref[i, 0:w_cache - s_new, :] = ck_ref[i, s_new:w_cache, :]
        newk_ref[i, w_cache - s_new:w_cache, :] = k_ref[new, :]
        newv_ref[i, 0:w_cache - s_new, :] = cv_ref[i, s_new:w_cache, :]
        newv_ref[i, w_cache - s_new:w_cache, :] = v_ref[new, :]

    kc = ck_ref[...].reshape(nseq * w_cache, KV_DIM).astype(BF16)
    vc = cv_ref[...].reshape(nseq * w_cache, KV_DIM).astype(BF16)
    kn = k_ref[...].astype(BF16)
    vn = v_ref[...].astype(BF16)
    q = q_ref[...]
    nq = q.shape[0]
    for kh in range(N_KV_HEADS):
        kvc = slice(kh * HEAD_DIM, (kh + 1) * HEAD_DIM)
        qg = jnp.concatenate([q[:, (kh * GROUP + g) * HEAD_DIM:(kh * GROUP + g + 1) * HEAD_DIM]
                              for g in range(GROUP)], axis=0)
        sc = lax.dot_general(qg, kc[:, kvc], _NT, preferred_element_type=F32) + biasc_ref[kh]
        sn = lax.dot_general(qg, kn[:, kvc], _NT, preferred_element_type=F32) + biasn_ref[kh]
        sink = jnp.concatenate([jnp.full((nq, 1), sink_ref[kh * GROUP + g], F32) for g in range(GROUP)], axis=0)
        m = jnp.maximum(jnp.maximum(jnp.max(sc, axis=-1, keepdims=True),
                                    jnp.max(sn, axis=-1, keepdims=True)), sink)
        ec = jnp.exp(sc - m)
        en = jnp.exp(sn - m)
        denom = (jnp.sum(ec, axis=-1, keepdims=True) + jnp.sum(en, axis=-1, keepdims=True)
                 + jnp.exp(sink - m))
        inv = 1.0 / denom
        o = (jnp.dot((ec * inv).astype(BF16), vc[:, kvc], preferred_element_type=F32)
             + jnp.dot((en * inv).astype(BF16), vn[:, kvc], preferred_element_type=F32))
        for g in range(GROUP):
            h = kh * GROUP + g
            attn_ref[:, h * HEAD_DIM:(h + 1) * HEAD_DIM] = o[g * nq:(g + 1) * nq].astype(attn_ref.dtype)


def _sample_mix(sinks, q, k, v, p, cache_k, cache_v, state, biasc, biasn, s_new):
    nb = cache_k.shape[0]
    rows = SAMPLE_SEQS * s_new
    tok = lambda c: pl.BlockSpec((rows, c), lambda i: (i, 0))
    seq3 = lambda r, c: pl.BlockSpec((SAMPLE_SEQS, r, c), lambda i: (i, 0, 0))
    const3 = lambda a: pl.BlockSpec(a.shape, lambda i: (0, 0, 0))
    return pl.pallas_call(
        _sample_mix_kernel,
        grid=(nb // SAMPLE_SEQS,),
        in_specs=[pl.BlockSpec(memory_space=pltpu.SMEM),
                  tok(Q_DIM), tok(KV_DIM), tok(KV_DIM), tok(POOL_WIDTH),
                  seq3(WINDOW, KV_DIM), seq3(WINDOW, KV_DIM), seq3(POOL_STATE, POOL_WIDTH),
                  const3(biasc), const3(biasn)],
        out_specs=[tok(POOL_WIDTH), tok(Q_DIM),
                   seq3(WINDOW, KV_DIM), seq3(WINDOW, KV_DIM), seq3(POOL_STATE, POOL_WIDTH)],
        out_shape=[jax.ShapeDtypeStruct((nb * s_new, POOL_WIDTH), F32),
                   jax.ShapeDtypeStruct((nb * s_new, Q_DIM), F32),
                   jax.ShapeDtypeStruct((nb, WINDOW, KV_DIM), F32),
                   jax.ShapeDtypeStruct((nb, WINDOW, KV_DIM), F32),
                   jax.ShapeDtypeStruct((nb, POOL_STATE, POOL_WIDTH), F32)],
        scratch_shapes=[pltpu.VMEM((1 + POOL_STATE + SUBLANES, POOL_WIDTH), F32),
                        pltpu.VMEM((s_new, POOL_WIDTH), F32)],
        compiler_params=_cparams(("parallel",)),
        name="sample_mix",
    )(sinks, q, k, v, p, cache_k, cache_v, state, biasc, biasn)


def _layer_norm(x, g, b):
    mu = jnp.mean(x, axis=-1, keepdims=True)
    xc = x - mu
    var = jnp.mean(xc * xc, axis=-1, keepdims=True)
    return xc * lax.rsqrt(var + LN_EPS) * g + b


def _merge_kernel(x_ref, pooled_ref, attn_ref, wg_ref, wmix_ref, scale_ref, wup_ref, wua_ref,
                  wout_ref, g_ref, b_ref, y_ref):
    x = x_ref[...]
    glog = jnp.dot(x.astype(BF16), wg_ref[...], preferred_element_type=F32)
    g_pool = jax.nn.sigmoid(glog[:, :D_MODEL])
    g_attn = jax.nn.sigmoid(glog[:, D_MODEL:])
    pooled = pooled_ref[...].astype(BF16)
    mixed = jnp.concatenate(
        [jnp.dot(pooled[:, g * POOL_GROUP_DIM:(g + 1) * POOL_GROUP_DIM], wmix_ref[g],
                 preferred_element_type=F32) for g in range(len(POOL_WINDOWS))], axis=1)
    pool_out = (mixed * scale_ref[...]).astype(BF16)
    a = jnp.dot(pool_out, wup_ref[...], preferred_element_type=F32)
    b = jnp.dot(attn_ref[...].astype(BF16), wua_ref[...], preferred_element_type=F32)
    m = (g_pool * a + g_attn * b).astype(BF16)
    r = jnp.dot(m, wout_ref[...], preferred_element_type=F32)
    y_ref[...] = _layer_norm(ALPHA * x + r, g_ref[...], b_ref[...])


def _merge(x, pooled, attn, wg, wmix, scale, wup, wua, wout, ln_g, ln_b):
    rows = x.shape[0]
    tm = min(TOK_TILE, rows)
    assert rows % tm == 0
    row = lambda c: pl.BlockSpec((tm, c), lambda i: (i, 0))
    full = lambda a: pl.BlockSpec(a.shape, lambda i: (0,) * a.ndim)
    weights = (wg, wmix, scale, wup, wua, wout, ln_g, ln_b)
    return pl.pallas_call(
        _merge_kernel,
        grid=(rows // tm,),
        in_specs=[row(D_MODEL), row(POOL_WIDTH), row(Q_DIM)] + [full(a) for a in weights],
        out_specs=row(D_MODEL),
        out_shape=jax.ShapeDtypeStruct((rows, D_MODEL), F32),
        compiler_params=_cparams(("parallel",)),
        name="merge",
    )(x, pooled, attn, *weights)


def _oddeven_merge_sort_pairs(n):
    pairs = []
    p = 1
    while p < n:
        k = p
        while k >= 1:
            for j in range(k % p, n - k, 2 * k):
                for i in range(min(k, n - j - k)):
                    if (i + j) // (2 * p) == (i + j + k) // (2 * p):
                        pairs.append((i + j, i + j + k))
            k //= 2
        p *= 2
    return pairs


def _bitonic_merge_pairs(n):
    pairs = []
    k = n // 2
    while k >= 1:
        pairs += [(i, i + k) for i in range(n) if not i & k]
        k //= 2
    return pairs


def _apply_network(vals, pairs):
    vals = list(vals)
    for i, j in pairs:
        a, b = vals[i], vals[j]
        if b is None:
            continue
        if a is None:
            vals[i], vals[j] = b, None
        else:
            vals[i], vals[j] = jnp.maximum(a, b), jnp.minimum(a, b)
    return vals


_SORT16 = _oddeven_merge_sort_pairs(PEER_TOPK)
_MERGE16 = _bitonic_merge_pairs(PEER_TOPK)


def _top16_rows(sc):
    n = sc.shape[0] // SUBLANES
    assert n == PEER_TOPK
    x = _apply_network([sc[k * SUBLANES:(k + 1) * SUBLANES, :] for k in range(n)], _SORT16)
    for shift in (4, 2, 1):
        y = [jnp.maximum(x[k], pltpu.roll(x[n - 1 - k], shift, 0)) for k in range(n)]
        x = _apply_network(y, _MERGE16)
    return x


def _best_sums(v1, v2):
    row = lax.broadcasted_iota(jnp.int32, v1[0].shape, 0)

    def one_per_sublane(vals):
        x = vals[SUBLANES - 1]
        for s in range(SUBLANES - 2, -1, -1):
            x = jnp.where(row == s, vals[s], x)
        return x

    lo, hi = one_per_sublane(v1[:SUBLANES]), one_per_sublane(v1[SUBLANES:])
    sums_lo = []
    for b in range(PEER_TOPK):
        n_valid = sum((a + 1) * (b + 1) <= PEER_TOPK for a in range(SUBLANES))
        s = lo + v2[b]
        sums_lo.append(s if n_valid == SUBLANES else jnp.where(row < n_valid, s, NEG))
    sums_hi = hi + v2[0]
    x, best = sums_hi, []
    for k in range(PEER_TOPK):
        best.append(jnp.maximum(sums_lo[k], x))
        x = jnp.minimum(sums_lo[k], x)
    for shift in (4, 2, 1):
        y = [jnp.maximum(best[k], pltpu.roll(best[PEER_TOPK - 1 - k], shift, 0)) for k in range(PEER_TOPK)]
        best = _apply_network(y, _MERGE16)
    return best, sums_lo, sums_hi


def _peer_scores_kernel(x_ref, wq_ref, keys_ref, cnt_ref, e1_ref, sel_ref):
    q = jnp.dot(x_ref[...].astype(BF16), wq_ref[...], preferred_element_type=F32).astype(BF16)
    for h in range(PEER_HEADS):
        sc, tops = [], []
        for c in range(2):
            col = (h * 2 + c) * PEER_HALF
            sc.append(lax.dot_general(keys_ref[h * 2 + c], q[:, col:col + PEER_HALF], _NT,
                                      preferred_element_type=F32))
            tops.append(_top16_rows(sc[c]))
        v1, v2 = tops
        best, sums_lo, sums_hi = _best_sums(v1, v2)
        tau = best[PEER_TOPK - 1]
        z = jnp.ones_like(tau)
        for r in range(1, PEER_TOPK):
            z = z + jnp.exp(best[r] - best[0])
        inv_z = 1.0 / z
        cnt_lo = jnp.zeros_like(tau)
        for s in sums_lo:
            cnt_lo = cnt_lo + jnp.where(s >= tau, 1.0, 0.0)
        cnt_hi = jnp.where(sums_hi >= tau, 1.0, 0.0)
        for t in range(tau.shape[1] // LANES):
            lanes = slice(t * LANES, (t + 1) * LANES)
            row = lambda r: r[0:1, lanes]
            s1, s2 = sc[0][:, lanes], sc[1][:, lanes]
            cnt = jnp.zeros_like(s1)
            rank = jnp.zeros_like(s2)
            for a in range(PEER_TOPK):
                cnt_a = (cnt_lo if a < SUBLANES else cnt_hi)[a % SUBLANES:a % SUBLANES + 1, lanes]
                cnt = jnp.where(s1 == row(v1[a]), cnt_a, cnt)
                rank = jnp.where(row(v2[a]) > s2, float(a + 1), rank)
            cnt_ref[h, t] = cnt
            e1_ref[h, t] = jnp.exp(s1 - row(v1[0])) * row(inv_z)
            rank_words = pltpu.bitcast(rank.astype(BF16), jnp.uint32)
            e2_words = pltpu.bitcast(jnp.exp(s2 - row(v2[0])).astype(BF16), jnp.uint32)
            for k in range(N_KEYS // (2 * SUBLANES)):
                words = slice(k * SUBLANES, (k + 1) * SUBLANES)
                sel_ref[h, t, 2 * k * SUBLANES:(2 * k + 1) * SUBLANES, :] = rank_words[words]
                sel_ref[h, t, (2 * k + 1) * SUBLANES:(2 * k + 2) * SUBLANES, :] = e2_words[words]


def _peer_scores(x, wq, keys):
    rows = x.shape[0]
    tm = min(TOK_TILE, rows)
    assert rows % tm == 0
    return pl.pallas_call(
        _peer_scores_kernel,
        grid=(rows // tm,),
        in_specs=[pl.BlockSpec((tm, D_MODEL), lambda i: (i, 0)),
                  pl.BlockSpec(wq.shape, lambda i: (0, 0)),
                  pl.BlockSpec(keys.shape, lambda i: (0, 0, 0))],
        out_specs=[pl.BlockSpec((PEER_HEADS, tm // LANES, N_KEYS, LANES), lambda i: (0, i, 0, 0))] * 3,
        out_shape=[jax.ShapeDtypeStruct((PEER_HEADS, rows // LANES, N_KEYS, LANES), dt)
                   for dt in (F32, F32, jnp.uint32)],
        compiler_params=_cparams(("parallel",)),
        name="peer_scores",
    )(x, wq, keys)


def _gelu(x):
    return 0.5 * x * (1.0 + lax.erf(x * math.sqrt(0.5)))


def _peer_dense_kernel(x_ref, cnt_ref, e1_ref, sel_ref, u_ref, v_ref, g_ref, b_ref, y_ref,
                       xt, rowb, hid, wt, acc):
    c = pl.program_id(1)
    rows_per_chunk = EXPERT_CHUNK // N_KEYS
    ncol = xt.shape[1] // LANES
    packed = 2 * SUBLANES
    assert rows_per_chunk == SUBLANES

    @pl.when(c == 0)
    def _():
        xt[...] = x_ref[...].T.astype(BF16)
        acc[...] = jnp.zeros_like(acc)

    def build_rows(chunk, heads):
        for h in heads:
            for col in range(ncol):
                for r in range(SUBLANES):
                    row = pl.ds(chunk * SUBLANES + r, packed, stride=0)
                    rowb[r, h, col] = cnt_ref[h, col, row, :].astype(BF16)
                    rowb[r, PEER_HEADS + h, col] = e1_ref[h, col, row, :].astype(BF16)

    def mask_row(il, carry):
        base = pl.multiple_of(il * N_KEYS, N_KEYS)
        for col in range(ncol):
            gate = [None] * (N_KEYS // packed)
            for h in range(PEER_HEADS):
                cnt = rowb[il, h, col]
                g1 = rowb[il, PEER_HEADS + h, col]
                for k in range(N_KEYS // packed):
                    rank = pltpu.bitcast(sel_ref[h, col, 2 * k * SUBLANES:(2 * k + 1) * SUBLANES, :], BF16)
                    g2 = pltpu.bitcast(sel_ref[h, col, (2 * k + 1) * SUBLANES:(2 * k + 2) * SUBLANES, :], BF16)
                    g = jnp.where(rank < cnt, g2 * g1, jnp.zeros((), BF16))
                    gate[k] = g if gate[k] is None else gate[k] + g
            for k in range(N_KEYS // packed):
                act = _gelu(hid[col, pl.ds(base + k * packed, packed), :]).astype(BF16)
                wt[pl.ds(base + k * packed, packed), col * LANES:(col + 1) * LANES] = act * gate[k]
        return carry

    n_slabs = 4
    slab = EXPERT_CHUNK // n_slabs

    for i in range(n_slabs):
        rows = slice(i * slab, (i + 1) * slab)
        res = jnp.dot(u_ref[rows, :].astype(BF16), xt[...], preferred_element_type=F32)
        for col in range(ncol):
            hid[col, rows, :] = res[:, col * LANES:(col + 1) * LANES]
        build_rows(c, range(i * PEER_HEADS // n_slabs, (i + 1) * PEER_HEADS // n_slabs))
    lax.fori_loop(0, rows_per_chunk, mask_row, 0)
    acc[...] += lax.dot_general(v_ref[...].astype(BF16), wt[...], (((0,), (0,)), ((), ())),
                                preferred_element_type=F32)

    @pl.when(c == pl.num_programs(1) - 1)
    def _():
        y_ref[...] = _layer_norm(ALPHA * x_ref[...] + acc[...].T, g_ref[...], b_ref[...])


def _peer_dense(x, cnt, e1, sel, u, vt, ln_g, ln_b):
    rows = x.shape[0]
    tm = min(TOK_TILE, rows)
    assert rows % tm == 0
    n_exp = u.shape[0]
    assert n_exp % EXPERT_CHUNK == 0
    return pl.pallas_call(
        _peer_dense_kernel,
        grid=(rows // tm, n_exp // EXPERT_CHUNK),
        in_specs=[pl.BlockSpec((tm, D_MODEL), lambda t, c: (t, 0))]
                 + [pl.BlockSpec((PEER_HEADS, tm // LANES, N_KEYS, LANES), lambda t, c: (0, t, 0, 0))] * 3
                 + [pl.BlockSpec((EXPERT_CHUNK, D_MODEL), lambda t, c: (c, 0)),
                    pl.BlockSpec((EXPERT_CHUNK, D_MODEL), lambda t, c: (c, 0)),
                    pl.BlockSpec((1, D_MODEL), lambda t, c: (0, 0)),
                    pl.BlockSpec((1, D_MODEL), lambda t, c: (0, 0))],
        out_specs=pl.BlockSpec((tm, D_MODEL), lambda t, c: (t, 0)),
        out_shape=jax.ShapeDtypeStruct((rows, D_MODEL), F32),
        scratch_shapes=[pltpu.VMEM((D_MODEL, tm), BF16),
                        pltpu.VMEM((SUBLANES, 2 * PEER_HEADS, tm // LANES, 2 * SUBLANES, LANES), BF16),
                        pltpu.VMEM((tm // LANES, EXPERT_CHUNK, LANES), F32),
                        pltpu.VMEM((EXPERT_CHUNK, tm), BF16),
                        pltpu.VMEM((D_MODEL, tm), F32)],
        compiler_params=_cparams(("parallel", "arbitrary")),
        name="peer_dense",
    )(x, cnt, e1, sel, u, vt, ln_g, ln_b)


def _rel_bucket_np(dist):
    n = np.maximum(dist, 0)
    max_exact = NUM_BUCKETS // 2
    large = max_exact + (np.log(np.maximum(n, max_exact).astype(np.float32) / max_exact)
                         / math.log(MAX_DISTANCE / max_exact) * (NUM_BUCKETS - max_exact)).astype(np.int32)
    return np.where(n < max_exact, n, np.minimum(large, NUM_BUCKETS - 1))


def _bias_kernel(table_ref, bucket_ref, out_ref, *, head_stride):
    head = pl.program_id(0) * head_stride + pl.program_id(1)
    bucket = bucket_ref[0]
    acc = jnp.full(bucket.shape, NEG, F32)
    for b in range(NUM_BUCKETS):
        acc = jnp.where(bucket == b, table_ref[b, head], acc)
    out_ref[0, 0] = acc


def _bias_tiles(table, bucket, n_outer, n_inner, head_stride):
    r, c = bucket.shape[1:]
    bmap = (lambda a, b: (a, 0, 0)) if bucket.shape[0] > 1 else (lambda a, b: (0, 0, 0))
    return pl.pallas_call(
        functools.partial(_bias_kernel, head_stride=head_stride),
        grid=(n_outer, n_inner),
        in_specs=[pl.BlockSpec(memory_space=pltpu.SMEM), pl.BlockSpec((1, r, c), bmap)],
        out_specs=pl.BlockSpec((1, 1, r, c), lambda a, b: (a, b, 0, 0)),
        out_shape=jax.ShapeDtypeStruct((n_outer, n_inner, r, c), F32),
        compiler_params=_cparams(("parallel", "parallel")),
        name="bias_tiles",
    )(table, jnp.asarray(bucket, jnp.int32))


def _prompt_bias(table):
    qi = np.arange(ATTN_BLOCK)[:, None]
    kj = np.arange(2 * ATTN_BLOCK)[None, :]
    dist = ATTN_BLOCK + qi - kj
    ok = (dist >= 0) & (dist < WINDOW)
    ok = np.stack([ok, ok & (kj >= ATTN_BLOCK - N_META)])
    bucket = np.where(ok, _rel_bucket_np(dist)[None], -1)
    tiles = _bias_tiles(table, bucket, 2, N_HEADS, 0)
    return tiles.reshape(2, N_KV_HEADS, GROUP * ATTN_BLOCK, 2 * ATTN_BLOCK)


def _sample_bias(table, s_new):
    s, t = np.meshgrid(np.arange(SAMPLE_SEQS), np.arange(s_new), indexing="ij")
    s, t = s.reshape(-1, 1), t.reshape(-1, 1)
    out = []
    for cols, offset in ((WINDOW, WINDOW), (s_new, 0)):
        s2, c = np.meshgrid(np.arange(SAMPLE_SEQS), np.arange(cols), indexing="ij")
        s2, c = s2.reshape(1, -1), c.reshape(1, -1)
        dist = offset + t - c + 0 * s2
        ok = (s == s2) & (dist >= 0) & (dist < WINDOW)
        bucket = np.where(ok, _rel_bucket_np(dist), -1)[None]
        tiles = _bias_tiles(table, bucket, N_KV_HEADS, GROUP, GROUP)
        out.append(tiles.reshape(N_KV_HEADS, GROUP * bucket.shape[1], bucket.shape[2]))
    return out


def kernel(x_prompt, x_sample, cache_k, cache_v, state_pool, meta_tokens, rel_bias_table, w_in,
           w_pool_mix, pool_scale, attn_sinks, w_up_pool, w_up_attn, w_out, ln1_g, ln1_b,
           peer_w_query, peer_sub_keys, peer_u, peer_v, ln2_g, ln2_b):
    batch, seq, d = x_prompt.shape
    nb, s_new, _ = x_sample.shape
    assert w_in.shape[0] == DEPTH and d == D_MODEL and seq % ATTN_TILE == 0
    assert N_META >= max(POOL_WINDOWS) - 1 and cache_k.shape[2] == WINDOW

    w_in0 = w_in[0].astype(BF16)
    w_pqkv, w_gates = w_in0[:, :OFF_GA], w_in0[:, OFF_GA:]
    wmix = w_pool_mix[0].astype(BF16)
    scale = pool_scale[0].reshape(1, POOL_WIDTH)
    wup, wua, wout = w_up_pool[0].astype(BF16), w_up_attn[0].astype(BF16), w_out[0].astype(BF16)
    g1, b1 = ln1_g[0].reshape(1, d), ln1_b[0].reshape(1, d)
    g2, b2 = ln2_g[0].reshape(1, d), ln2_b[0].reshape(1, d)
    wq = peer_w_query[0].astype(BF16)
    keys = peer_sub_keys[0].reshape(2 * PEER_HEADS, N_KEYS, PEER_HALF).astype(BF16)
    u = peer_u[0]
    vt = peer_v[0]
    sinks = attn_sinks[0].astype(F32)
    table = rel_bias_table.astype(F32)
    bias_p = _prompt_bias(table)
    bias_c, bias_n = _sample_bias(table, s_new)

    xp = x_prompt.reshape(batch * seq, d)
    xs = x_sample.reshape(nb * s_new, d)
    xm = jnp.concatenate([jnp.zeros((ATTN_BLOCK - N_META, d), F32), meta_tokens.astype(F32)], axis=0)

    p_p, q_p, k_p, v_p = _inproj(xp, w_pqkv)
    p_s, q_s, k_s, v_s = _inproj(xs, w_pqkv)
    p_m, _, k_m, v_m = _inproj(xm, w_pqkv)

    pooled_p, attn_p = _prompt_mix(sinks, q_p, k_p, v_p, p_p, k_m, v_m, p_m[ATTN_BLOCK - N_META:],
                                   bias_p, batch, seq)
    ck = cache_k[0].reshape(nb, WINDOW, KV_DIM)
    cv = cache_v[0].reshape(nb, WINDOW, KV_DIM)
    pooled_s, attn_s, newk, newv, newp = _sample_mix(sinks, q_s, k_s, v_s, p_s, ck, cv,
                                                     state_pool[0], bias_c, bias_n, s_new)

    outs = []
    for x, pooled, attn in ((xp, pooled_p, attn_p), (xs, pooled_s, attn_s)):
        x1 = _merge(x, pooled, attn, w_gates, wmix, scale, wup, wua, wout, g1, b1)
        cnt, e1, sel = _peer_scores(x1, wq, keys)
        outs.append(_peer_dense(x1, cnt, e1, sel, u, vt, g2, b2))

    w_keep = min(WINDOW, seq + N_META)
    kv_shape = (batch, seq, N_KV_HEADS, HEAD_DIM)
    return (outs[0].reshape(batch, seq, d),
            outs[1].reshape(nb, s_new, d),
            k_p.reshape(kv_shape)[None, :, seq - w_keep:],
            v_p.reshape(kv_shape)[None, :, seq - w_keep:],
            p_p.reshape(batch, seq, POOL_WIDTH)[None, :, seq - POOL_STATE:],
            newk.reshape(1, nb, WINDOW, N_KV_HEADS, HEAD_DIM),
            newv.reshape(1, nb, WINDOW, N_KV_HEADS, HEAD_DIM),
            newp[None])
```

```python
import functools
import math

import jax
import jax.numpy as jnp
import numpy as np
from jax import lax
from jax.experimental import pallas as pl
from jax.experimental.pallas import tpu as pltpu

F32 = jnp.float32
BF16 = jnp.bfloat16

D_MODEL = 1024
N_META = 16
POOL_WIDTH = 512
POOL_WINDOWS = (2, 4, 8, 16)
POOL_GROUP_DIM = 128
POOL_STATE = 15
HEAD_DIM = 64
N_HEADS = 8
N_KV_HEADS = 2
GROUP = N_HEADS // N_KV_HEADS
WINDOW = 128
ATTN_BLOCK = 128
ATTN_SCALE = HEAD_DIM ** -0.5
NUM_BUCKETS = 32
MAX_DISTANCE = 128
Q_DIM = N_HEADS * HEAD_DIM
KV_DIM = N_KV_HEADS * HEAD_DIM
OFF_Q = POOL_WIDTH
OFF_K = OFF_Q + Q_DIM
OFF_V = OFF_K + KV_DIM
OFF_GA = OFF_V + KV_DIM
PEER_HEADS = 8
N_KEYS = 128
PEER_TOPK = 16
PEER_HALF = 128
DEPTH = 1
ALPHA = (2 * DEPTH) ** 0.25
LN_EPS = 1e-5
NEG = -1e30

LANES = 128
SUBLANES = 8
VMEM_LIMIT = 56 * 1024 * 1024

TOK_TILE = 512
ATTN_TILE = 512
SAMPLE_SEQS = 8
EXPERT_CHUNK = 1024


def _cparams(sem):
    return pltpu.CompilerParams(dimension_semantics=sem, vmem_limit_bytes=VMEM_LIMIT)


def _inproj_kernel(x_ref, w_ref, p_ref, q_ref, k_ref, v_ref):
    z = jnp.dot(x_ref[...].astype(BF16), w_ref[...], preferred_element_type=F32)
    p_ref[...] = z[:, :OFF_Q]
    q_ref[...] = (z[:, OFF_Q:OFF_K] * ATTN_SCALE).astype(BF16)
    k_ref[...] = z[:, OFF_K:OFF_V]
    v_ref[...] = z[:, OFF_V:OFF_GA]


def _inproj(x, w_pqkv):
    rows = x.shape[0]
    tm = min(TOK_TILE, rows)
    assert rows % tm == 0
    row = lambda c: pl.BlockSpec((tm, c), lambda i: (i, 0))
    return pl.pallas_call(
        _inproj_kernel,
        grid=(rows // tm,),
        in_specs=[row(D_MODEL), pl.BlockSpec((D_MODEL, OFF_GA), lambda i: (0, 0))],
        out_specs=[row(POOL_WIDTH), row(Q_DIM), row(KV_DIM), row(KV_DIM)],
        out_shape=[jax.ShapeDtypeStruct((rows, POOL_WIDTH), F32),
                   jax.ShapeDtypeStruct((rows, Q_DIM), BF16),
                   jax.ShapeDtypeStruct((rows, KV_DIM), F32),
                   jax.ShapeDtypeStruct((rows, KV_DIM), F32)],
        compiler_params=_cparams(("parallel",)),
        name="inproj",
    )(x, w_pqkv)


def _window_pool(ext_ref, first, rows, out_ref):
    for g, w in enumerate(POOL_WINDOWS):
        cols = slice(g * POOL_GROUP_DIM, (g + 1) * POOL_GROUP_DIM)
        cur = ext_ref[first:first + rows, cols]
        acc = cur
        for r in range(1, w):
            acc = acc + ext_ref[first - r:first - r + rows, cols]
        out_ref[:, cols] = (acc * (1.0 / w) - cur).astype(out_ref.dtype)


def _window_pool_tall(ext_ref, first, rows, out_ref):
    for g, w in enumerate(POOL_WINDOWS):
        cols = slice(g * POOL_GROUP_DIM, (g + 1) * POOL_GROUP_DIM)
        x = ext_ref[:, cols]
        s, span = x, 1
        while span < w:
            s = s + pltpu.roll(s, span, 0)
            span *= 2
        out_ref[:, cols] = (s[first:first + rows] * (1.0 / w) - x[first:first + rows]).astype(out_ref.dtype)


def _sink_softmax(s, sink):
    m = jnp.maximum(jnp.max(s, axis=-1, keepdims=True), sink)
    e = jnp.exp(s - m)
    denom = jnp.sum(e, axis=-1, keepdims=True) + jnp.exp(sink - m)
    return e * (1.0 / denom)


_NT = (((1,), (1,)), ((), ()))


def _prompt_mix_kernel(sink_ref, q_ref, k_ref, v_ref, p_ref, kprev_ref, vprev_ref, pprev_ref,
                       mk_ref, mv_ref, mp_ref, bias_ref, pooled_ref, attn_ref,
                       kbuf, vbuf, pbuf, sbuf, prob_buf):
    first = pl.program_id(1) == 0
    hist = ATTN_BLOCK
    kbuf[0:hist, :] = jnp.where(first, mk_ref[...], kprev_ref[...]).astype(BF16)
    vbuf[0:hist, :] = jnp.where(first, mv_ref[...], vprev_ref[...]).astype(BF16)
    kbuf[hist:, :] = k_ref[...].astype(BF16)
    vbuf[hist:, :] = v_ref[...].astype(BF16)
    pbuf[0:N_META, :] = jnp.where(first, mp_ref[...], pprev_ref[...])
    pbuf[N_META:, :] = p_ref[...]

    _window_pool_tall(pbuf, N_META, ATTN_TILE, pooled_ref)

    first_i = jnp.where(first, 1, 0)
    blocks = [(j, kh) for j in range(ATTN_TILE // ATTN_BLOCK) for kh in range(N_KV_HEADS)]
    rows = lambda j: slice(j * ATTN_BLOCK, (j + 1) * ATTN_BLOCK)
    keys = lambda j: slice(j * ATTN_BLOCK, j * ATTN_BLOCK + 2 * ATTN_BLOCK)
    head = lambda h: slice(h * HEAD_DIM, (h + 1) * HEAD_DIM)
    for i, (j, kh) in enumerate(blocks):
        qg = jnp.concatenate([q_ref[rows(j), head(kh * GROUP + g)] for g in range(GROUP)], axis=0)
        s = lax.dot_general(qg, kbuf[keys(j), head(kh)], _NT, preferred_element_type=F32)
        sbuf[i] = s + (bias_ref[first_i, kh] if j == 0 else bias_ref[0, kh])
    for i, (j, kh) in enumerate(blocks):
        sink = jnp.concatenate([jnp.full((ATTN_BLOCK, 1), sink_ref[kh * GROUP + g], F32) for g in range(GROUP)], axis=0)
        prob_buf[i] = _sink_softmax(sbuf[i], sink).astype(BF16)
    for i, (j, kh) in enumerate(blocks):
        o = jnp.dot(prob_buf[i], vbuf[keys(j), head(kh)], preferred_element_type=F32)
        for g in range(GROUP):
            attn_ref[rows(j), head(kh * GROUP + g)] = o[g * ATTN_BLOCK:(g + 1) * ATTN_BLOCK].astype(attn_ref.dtype)


def _prompt_mix(sinks, q, k, v, p, mk, mv, mp, bias, batch, seq):
    nt = seq // ATTN_TILE
    blocks_per_tile = ATTN_TILE // ATTN_BLOCK
    cur = lambda c: pl.BlockSpec((ATTN_TILE, c), lambda b, s: (b * nt + s, 0))
    prev_blk = lambda b, s: (jnp.maximum((b * nt + s) * blocks_per_tile - 1, 0), 0)
    prev_p = lambda b, s: (jnp.maximum((b * nt + s) * (ATTN_TILE // N_META) - 1, 0), 0)
    const2 = lambda b, s: (0, 0)
    rows = batch * seq
    return pl.pallas_call(
        _prompt_mix_kernel,
        grid=(batch, nt),
        in_specs=[pl.BlockSpec(memory_space=pltpu.SMEM),
                  cur(Q_DIM), cur(KV_DIM), cur(KV_DIM), cur(POOL_WIDTH),
                  pl.BlockSpec((ATTN_BLOCK, KV_DIM), prev_blk),
                  pl.BlockSpec((ATTN_BLOCK, KV_DIM), prev_blk),
                  pl.BlockSpec((N_META, POOL_WIDTH), prev_p),
                  pl.BlockSpec((ATTN_BLOCK, KV_DIM), const2),
                  pl.BlockSpec((ATTN_BLOCK, KV_DIM), const2),
                  pl.BlockSpec((N_META, POOL_WIDTH), const2),
                  pl.BlockSpec((2, N_KV_HEADS, GROUP * ATTN_BLOCK, 2 * ATTN_BLOCK), lambda b, s: (0, 0, 0, 0))],
        out_specs=[cur(POOL_WIDTH), cur(Q_DIM)],
        out_shape=[jax.ShapeDtypeStruct((rows, POOL_WIDTH), BF16),
                   jax.ShapeDtypeStruct((rows, Q_DIM), BF16)],
        scratch_shapes=[pltpu.VMEM((ATTN_BLOCK + ATTN_TILE, KV_DIM), BF16),
                        pltpu.VMEM((ATTN_BLOCK + ATTN_TILE, KV_DIM), BF16),
                        pltpu.VMEM((N_META + ATTN_TILE, POOL_WIDTH), F32),
                        pltpu.VMEM((blocks_per_tile * N_KV_HEADS, GROUP * ATTN_BLOCK, 2 * ATTN_BLOCK), F32),
                        pltpu.VMEM((blocks_per_tile * N_KV_HEADS, GROUP * ATTN_BLOCK, 2 * ATTN_BLOCK), BF16)],
        compiler_params=_cparams(("parallel", "parallel")),
        name="prompt_mix",
    )(sinks, q, k, v, p, k, v, p, mk, mv, mp, bias)


def _sample_mix_kernel(sink_ref, q_ref, k_ref, v_ref, p_ref, ck_ref, cv_ref, st_ref,
                       biasc_ref, biasn_ref,
                       pooled_ref, attn_ref, newk_ref, newv_ref, newp_ref, ext, pooled_buf):
    nseq, w_cache, s_new = SAMPLE_SEQS, WINDOW, q_ref.shape[0] // SAMPLE_SEQS
    keep = POOL_STATE - s_new
    for i in range(nseq):
        new = slice(i * s_new, (i + 1) * s_new)
        ext[1:1 + POOL_STATE, :] = st_ref[i]
        ext[1 + POOL_STATE:1 + POOL_STATE + s_new, :] = p_ref[new, :]
        _window_pool(ext, 1 + POOL_STATE, s_new, pooled_buf)
        pooled_ref[new, :] = pooled_buf[...].astype(pooled_ref.dtype)
        newp_ref[i, 0:keep, :] = st_ref[i, s_new:POOL_STATE, :]
        newp_ref[i, keep:POOL_STATE, :] = p_ref[new, :]
        newk_ref[i, 0:w_cache - s_new, :] = ck_ref[i, s_new:w_cache, :]
        newk_ref[i, w_cache - s_new:w_cache, :] = k_ref[new, :]
        newv_ref[i, 0:w_cache - s_new, :] = cv_ref[i, s_new:w_cache, :]
        newv_ref[i, w_cache - s_new:w_cache, :] = v_ref[new, :]

    kc = ck_ref[...].reshape(nseq * w_cache, KV_DIM).astype(BF16)
    vc = cv_ref[...].reshape(nseq * w_cache, KV_DIM).astype(BF16)
    kn = k_ref[...].astype(BF16)
    vn = v_ref[...].astype(BF16)
    q = q_ref[...]
    nq = q.shape[0]
    for kh in range(N_KV_HEADS):
        kvc = slice(kh * HEAD_DIM, (kh + 1) * HEAD_DIM)
        qg = jnp.concatenate([q[:, (kh * GROUP + g) * HEAD_DIM:(kh * GROUP + g + 1) * HEAD_DIM]
                              for g in range(GROUP)], axis=0)
        sc = lax.dot_general(qg, kc[:, kvc], _NT, preferred_element_type=F32) + biasc_ref[kh]
        sn = lax.dot_general(qg, kn[:, kvc], _NT, preferred_element_type=F32) + biasn_ref[kh]
        sink = jnp.concatenate([jnp.full((nq, 1), sink_ref[kh * GROUP + g], F32) for g in range(GROUP)], axis=0)
        m = jnp.maximum(jnp.maximum(jnp.max(sc, axis=-1, keepdims=True),
                                    jnp.max(sn, axis=-1, keepdims=True)), sink)
        ec = jnp.exp(sc - m)
        en = jnp.exp(sn - m)
        denom = (jnp.sum(ec, axis=-1, keepdims=True) + jnp.sum(en, axis=-1, keepdims=True)
                 + jnp.exp(sink - m))
        inv = 1.0 / denom
        o = (jnp.dot((ec * inv).astype(BF16), vc[:, kvc], preferred_element_type=F32)
             + jnp.dot((en * inv).astype(BF16), vn[:, kvc], preferred_element_type=F32))
        for g in range(GROUP):
            h = kh * GROUP + g
            attn_ref[:, h * HEAD_DIM:(h + 1) * HEAD_DIM] = o[g * nq:(g + 1) * nq].astype(attn_ref.dtype)


def _sample_mix(sinks, q, k, v, p, cache_k, cache_v, state, biasc, biasn, s_new):
    nb = cache_k.shape[0]
    rows = SAMPLE_SEQS * s_new
    tok = lambda c: pl.BlockSpec((rows, c), lambda i: (i, 0))
    seq3 = lambda r, c: pl.BlockSpec((SAMPLE_SEQS, r, c), lambda i: (i, 0, 0))
    const3 = lambda a: pl.BlockSpec(a.shape, lambda i: (0, 0, 0))
    return pl.pallas_call(
        _sample_mix_kernel,
        grid=(nb // SAMPLE_SEQS,),
        in_specs=[pl.BlockSpec(memory_space=pltpu.SMEM),
                  tok(Q_DIM), tok(KV_DIM), tok(KV_DIM), tok(POOL_WIDTH),
                  seq3(WINDOW, KV_DIM), seq3(WINDOW, KV_DIM), seq3(POOL_STATE, POOL_WIDTH),
                  const3(biasc), const3(biasn)],
        out_specs=[tok(POOL_WIDTH), tok(Q_DIM),
                   seq3(WINDOW, KV_DIM), seq3(WINDOW, KV_DIM), seq3(POOL_STATE, POOL_WIDTH)],
        out_shape=[jax.ShapeDtypeStruct((nb * s_new, POOL_WIDTH), F32),
                   jax.ShapeDtypeStruct((nb * s_new, Q_DIM), F32),
                   jax.ShapeDtypeStruct((nb, WINDOW, KV_DIM), F32),
                   jax.ShapeDtypeStruct((nb, WINDOW, KV_DIM), F32),
                   jax.ShapeDtypeStruct((nb, POOL_STATE, POOL_WIDTH), F32)],
        scratch_shapes=[pltpu.VMEM((1 + POOL_STATE + SUBLANES, POOL_WIDTH), F32),
                        pltpu.VMEM((s_new, POOL_WIDTH), F32)],
        compiler_params=_cparams(("parallel",)),
        name="sample_mix",
    )(sinks, q, k, v, p, cache_k, cache_v, state, biasc, biasn)


def _layer_norm(x, g, b):
    mu = jnp.mean(x, axis=-1, keepdims=True)
    xc = x - mu
    var = jnp.mean(xc * xc, axis=-1, keepdims=True)
    return xc * lax.rsqrt(var + LN_EPS) * g + b


def _merge_kernel(x_ref, pooled_ref, attn_ref, wg_ref, wmix_ref, scale_ref, wup_ref, wua_ref,
                  wout_ref, g_ref, b_ref, y_ref):
    x = x_ref[...]
    glog = jnp.dot(x.astype(BF16), wg_ref[...], preferred_element_type=F32)
    g_pool = jax.nn.sigmoid(glog[:, :D_MODEL])
    g_attn = jax.nn.sigmoid(glog[:, D_MODEL:])
    pooled = pooled_ref[...].astype(BF16)
    mixed = jnp.concatenate(
        [jnp.dot(pooled[:, g * POOL_GROUP_DIM:(g + 1) * POOL_GROUP_DIM], wmix_ref[g],
                 preferred_element_type=F32) for g in range(len(POOL_WINDOWS))], axis=1)
    pool_out = (mixed * scale_ref[...]).astype(BF16)
    a = jnp.dot(pool_out, wup_ref[...], preferred_element_type=F32)
    b = jnp.dot(attn_ref[...].astype(BF16), wua_ref[...], preferred_element_type=F32)
    m = (g_pool * a + g_attn * b).astype(BF16)
    r = jnp.dot(m, wout_ref[...], preferred_element_type=F32)
    y_ref[...] = _layer_norm(ALPHA * x + r, g_ref[...], b_ref[...])


def _merge(x, pooled, attn, wg, wmix, scale, wup, wua, wout, ln_g, ln_b):
    rows = x.shape[0]
    tm = min(TOK_TILE, rows)
    assert rows % tm == 0
    row = lambda c: pl.BlockSpec((tm, c), lambda i: (i, 0))
    full = lambda a: pl.BlockSpec(a.shape, lambda i: (0,) * a.ndim)
    weights = (wg, wmix, scale, wup, wua, wout, ln_g, ln_b)
    return pl.pallas_call(
        _merge_kernel,
        grid=(rows // tm,),
        in_specs=[row(D_MODEL), row(POOL_WIDTH), row(Q_DIM)] + [full(a) for a in weights],
        out_specs=row(D_MODEL),
        out_shape=jax.ShapeDtypeStruct((rows, D_MODEL), F32),
        compiler_params=_cparams(("parallel",)),
        name="merge",
    )(x, pooled, attn, *weights)


def _oddeven_merge_sort_pairs(n):
    pairs = []
    p = 1
    while p < n:
        k = p
        while k >= 1:
            for j in range(k % p, n - k, 2 * k):
                for i in range(min(k, n - j - k)):
                    if (i + j) // (2 * p) == (i + j + k) // (2 * p):
                        pairs.append((i + j, i + j + k))
            k //= 2
        p *= 2
    return pairs


def _bitonic_merge_pairs(n):
    pairs = []
    k = n // 2
    while k >= 1:
        pairs += [(i, i + k) for i in range(n) if not i & k]
        k //= 2
    return pairs


def _apply_network(vals, pairs):
    vals = list(vals)
    for i, j in pairs:
        a, b = vals[i], vals[j]
        if b is None:
            continue
        if a is None:
            vals[i], vals[j] = b, None
        else:
            vals[i], vals[j] = jnp.maximum(a, b), jnp.minimum(a, b)
    return vals


_SORT16 = _oddeven_merge_sort_pairs(PEER_TOPK)
_MERGE16 = _bitonic_merge_pairs(PEER_TOPK)


def _top16_rows(sc):
    n = sc.shape[0] // SUBLANES
    assert n == PEER_TOPK
    x = _apply_network([sc[k * SUBLANES:(k + 1) * SUBLANES, :] for k in range(n)], _SORT16)
    for shift in (4, 2, 1):
        y = [jnp.maximum(x[k], pltpu.roll(x[n - 1 - k], shift, 0)) for k in range(n)]
        x = _apply_network(y, _MERGE16)
    return x


def _best_sums(v1, v2):
    row = lax.broadcasted_iota(jnp.int32, v1[0].shape, 0)

    def one_per_sublane(vals):
        x = vals[SUBLANES - 1]
        for s in range(SUBLANES - 2, -1, -1):
            x = jnp.where(row == s, vals[s], x)
        return x

    lo, hi = one_per_sublane(v1[:SUBLANES]), one_per_sublane(v1[SUBLANES:])
    sums_lo = []
    for b in range(PEER_TOPK):
        n_valid = sum((a + 1) * (b + 1) <= PEER_TOPK for a in range(SUBLANES))
        s = lo + v2[b]
        sums_lo.append(s if n_valid == SUBLANES else jnp.where(row < n_valid, s, NEG))
    sums_hi = hi + v2[0]
    x, best = sums_hi, []
    for k in range(PEER_TOPK):
        best.append(jnp.maximum(sums_lo[k], x))
        x = jnp.minimum(sums_lo[k], x)
    for shift in (4, 2, 1):
        y = [jnp.maximum(best[k], pltpu.roll(best[PEER_TOPK - 1 - k], shift, 0)) for k in range(PEER_TOPK)]
        best = _apply_network(y, _MERGE16)
    return best, sums_lo, sums_hi


def _peer_scores_kernel(x_ref, wq_ref, keys_ref, cnt_ref, e1_ref, sel_ref):
    q = jnp.dot(x_ref[...].astype(BF16), wq_ref[...], preferred_element_type=F32).astype(BF16)
    for h in range(PEER_HEADS):
        sc = []
        for c in range(2):
            col = (h * 2 + c) * PEER_HALF
            sc.append(lax.dot_general(keys_ref[h * 2 + c], q[:, col:col + PEER_HALF], _NT,
                                      preferred_element_type=F32))
        for t in range(sc[0].shape[1] // LANES):
            lanes = slice(t * LANES, (t + 1) * LANES)
            s1, s2 = sc[0][:, lanes], sc[1][:, lanes]
            v1, v2 = _top16_rows(s1), _top16_rows(s2)
            best, sums_lo, sums_hi = _best_sums(v1, v2)
            tau = best[PEER_TOPK - 1]
            z = jnp.ones_like(tau)
            for r in range(1, PEER_TOPK):
                z = z + jnp.exp(best[r] - best[0])
            inv_z = 1.0 / z
            cnt_lo = jnp.zeros_like(tau)
            for s in sums_lo:
                cnt_lo = cnt_lo + jnp.where(s >= tau, 1.0, 0.0)
            cnt_hi = jnp.where(sums_hi >= tau, 1.0, 0.0)
            row = lambda r: r[0:1, :]
            cnt = jnp.zeros_like(s1)
            rank = jnp.zeros_like(s2)
            for a in range(PEER_TOPK):
                cnt_a = (cnt_lo if a < SUBLANES else cnt_hi)[a % SUBLANES:a % SUBLANES + 1, :]
                cnt = jnp.where(s1 == row(v1[a]), cnt_a, cnt)
                rank = jnp.where(row(v2[a]) > s2, float(a + 1), rank)
            cnt_ref[h, t] = cnt
            e1_ref[h, t] = jnp.exp(s1 - row(v1[0])) * row(inv_z)
            rank_words = pltpu.bitcast(rank.astype(BF16), jnp.uint32)
            e2_words = pltpu.bitcast(jnp.exp(s2 - row(v2[0])).astype(BF16), jnp.uint32)
            for k in range(N_KEYS // (2 * SUBLANES)):
                words = slice(k * SUBLANES, (k + 1) * SUBLANES)
                sel_ref[h, t, 2 * k * SUBLANES:(2 * k + 1) * SUBLANES, :] = rank_words[words]
                sel_ref[h, t, (2 * k + 1) * SUBLANES:(2 * k + 2) * SUBLANES, :] = e2_words[words]


def _peer_scores(x, wq, keys):
    rows = x.shape[0]
    tm = min(TOK_TILE, rows)
    assert rows % tm == 0
    return pl.pallas_call(
        _peer_scores_kernel,
        grid=(rows // tm,),
        in_specs=[pl.BlockSpec((tm, D_MODEL), lambda i: (i, 0)),
                  pl.BlockSpec(wq.shape, lambda i: (0, 0)),
                  pl.BlockSpec(keys.shape, lambda i: (0, 0, 0))],
        out_specs=[pl.BlockSpec((PEER_HEADS, tm // LANES, N_KEYS, LANES), lambda i: (0, i, 0, 0))] * 3,
        out_shape=[jax.ShapeDtypeStruct((PEER_HEADS, rows // LANES, N_KEYS, LANES), dt)
                   for dt in (F32, F32, jnp.uint32)],
        compiler_params=_cparams(("parallel",)),
        name="peer_scores",
    )(x, wq, keys)


def _gelu(x):
    return 0.5 * x * (1.0 + lax.erf(x * math.sqrt(0.5)))


def _peer_dense_kernel(x_ref, cnt_ref, e1_ref, sel_ref, u_ref, v_ref, g_ref, b_ref, y_ref,
                       xt, rowb, hid, wt, acc):
    c = pl.program_id(1)
    rows_per_chunk = EXPERT_CHUNK // N_KEYS
    ncol = xt.shape[1] // LANES
    packed = 2 * SUBLANES
    assert rows_per_chunk == SUBLANES

    @pl.when(c == 0)
    def _():
        xt[...] = x_ref[...].T.astype(BF16)
        acc[...] = jnp.zeros_like(acc)

    def build_rows(chunk, heads):
        for h in heads:
            for col in range(ncol):
                for r in range(SUBLANES):
                    row = pl.ds(chunk * SUBLANES + r, packed, stride=0)
                    rowb[r, h, col] = cnt_ref[h, col, row, :].astype(BF16)
                    rowb[r, PEER_HEADS + h, col] = e1_ref[h, col, row, :].astype(BF16)

    def mask_row(il, carry):
        base = pl.multiple_of(il * N_KEYS, N_KEYS)
        for col in range(ncol):
            gate = [None] * (N_KEYS // packed)
            for h in range(PEER_HEADS):
                cnt = rowb[il, h, col]
                g1 = rowb[il, PEER_HEADS + h, col]
                for k in range(N_KEYS // packed):
                    rank = pltpu.bitcast(sel_ref[h, col, 2 * k * SUBLANES:(2 * k + 1) * SUBLANES, :], BF16)
                    g2 = pltpu.bitcast(sel_ref[h, col, (2 * k + 1) * SUBLANES:(2 * k + 2) * SUBLANES, :], BF16)
                    g = jnp.where(rank < cnt, g2 * g1, jnp.zeros((), BF16))
                    gate[k] = g if gate[k] is None else gate[k] + g
            for k in range(N_KEYS // packed):
                act = _gelu(hid[col, pl.ds(base + k * packed, packed), :]).astype(BF16)
                wt[pl.ds(base + k * packed, packed), col * LANES:(col + 1) * LANES] = act * gate[k]
        return carry

    n_slabs = 4
    slab = EXPERT_CHUNK // n_slabs

    for i in range(n_slabs):
        rows = slice(i * slab, (i + 1) * slab)
        res = jnp.dot(u_ref[rows, :].astype(BF16), xt[...], preferred_element_type=F32)
        for col in range(ncol):
            hid[col, rows, :] = res[:, col * LANES:(col + 1) * LANES]
        build_rows(c, range(i * PEER_HEADS // n_slabs, (i + 1) * PEER_HEADS // n_slabs))
    lax.fori_loop(0, rows_per_chunk, mask_row, 0)
    acc[...] += lax.dot_general(v_ref[...].astype(BF16), wt[...], (((0,), (0,)), ((), ())),
                                preferred_element_type=F32)

    @pl.when(c == pl.num_programs(1) - 1)
    def _():
        y_ref[...] = _layer_norm(ALPHA * x_ref[...] + acc[...].T, g_ref[...], b_ref[...])


def _peer_dense(x, cnt, e1, sel, u, vt, ln_g, ln_b):
    rows = x.shape[0]
    tm = min(TOK_TILE, rows)
    assert rows % tm == 0
    n_exp = u.shape[0]
    assert n_exp % EXPERT_CHUNK == 0
    return pl.pallas_call(
        _peer_dense_kernel,
        grid=(rows // tm, n_exp // EXPERT_CHUNK),
        in_specs=[pl.BlockSpec((tm, D_MODEL), lambda t, c: (t, 0))]
                 + [pl.BlockSpec((PEER_HEADS, tm // LANES, N_KEYS, LANES), lambda t, c: (0, t, 0, 0))] * 3
                 + [pl.BlockSpec((EXPERT_CHUNK, D_MODEL), lambda t, c: (c, 0)),
                    pl.BlockSpec((EXPERT_CHUNK, D_MODEL), lambda t, c: (c, 0)),
                    pl.BlockSpec((1, D_MODEL), lambda t, c: (0, 0)),
                    pl.BlockSpec((1, D_MODEL), lambda t, c: (0, 0))],
        out_specs=pl.BlockSpec((tm, D_MODEL), lambda t, c: (t, 0)),
        out_shape=jax.ShapeDtypeStruct((rows, D_MODEL), F32),
        scratch_shapes=[pltpu.VMEM((D_MODEL, tm), BF16),
                        pltpu.VMEM((SUBLANES, 2 * PEER_HEADS, tm // LANES, 2 * SUBLANES, LANES), BF16),
                        pltpu.VMEM((tm // LANES, EXPERT_CHUNK, LANES), F32),
                        pltpu.VMEM((EXPERT_CHUNK, tm), BF16),
                        pltpu.VMEM((D_MODEL, tm), F32)],
        compiler_params=_cparams(("parallel", "arbitrary")),
        name="peer_dense",
    )(x, cnt, e1, sel, u, vt, ln_g, ln_b)


def _rel_bucket_np(dist):
    n = np.maximum(dist, 0)
    max_exact = NUM_BUCKETS // 2
    large = max_exact + (np.log(np.maximum(n, max_exact).astype(np.float32) / max_exact)
                         / math.log(MAX_DISTANCE / max_exact) * (NUM_BUCKETS - max_exact)).astype(np.int32)
    return np.where(n < max_exact, n, np.minimum(large, NUM_BUCKETS - 1))


def _bias_kernel(table_ref, bucket_ref, out_ref, *, head_stride):
    head = pl.program_id(0) * head_stride + pl.program_id(1)
    bucket = bucket_ref[0]
    acc = jnp.full(bucket.shape, NEG, F32)
    for b in range(NUM_BUCKETS):
        acc = jnp.where(bucket == b, table_ref[b, head], acc)
    out_ref[0, 0] = acc


def _bias_tiles(table, bucket, n_outer, n_inner, head_stride):
    r, c = bucket.shape[1:]
    bmap = (lambda a, b: (a, 0, 0)) if bucket.shape[0] > 1 else (lambda a, b: (0, 0, 0))
    return pl.pallas_call(
        functools.partial(_bias_kernel, head_stride=head_stride),
        grid=(n_outer, n_inner),
        in_specs=[pl.BlockSpec(memory_space=pltpu.SMEM), pl.BlockSpec((1, r, c), bmap)],
        out_specs=pl.BlockSpec((1, 1, r, c), lambda a, b: (a, b, 0, 0)),
        out_shape=jax.ShapeDtypeStruct((n_outer, n_inner, r, c), F32),
        compiler_params=_cparams(("parallel", "parallel")),
        name="bias_tiles",
    )(table, jnp.asarray(bucket, jnp.int32))


def _prompt_bias(table):
    qi = np.arange(ATTN_BLOCK)[:, None]
    kj = np.arange(2 * ATTN_BLOCK)[None, :]
    dist = ATTN_BLOCK + qi - kj
    ok = (dist >= 0) & (dist < WINDOW)
    ok = np.stack([ok, ok & (kj >= ATTN_BLOCK - N_META)])
    bucket = np.where(ok, _rel_bucket_np(dist)[None], -1)
    tiles = _bias_tiles(table, bucket, 2, N_HEADS, 0)
    return tiles.reshape(2, N_KV_HEADS, GROUP * ATTN_BLOCK, 2 * ATTN_BLOCK)


def _sample_bias(table, s_new):
    s, t = np.meshgrid(np.arange(SAMPLE_SEQS), np.arange(s_new), indexing="ij")
    s, t = s.reshape(-1, 1), t.reshape(-1, 1)
    out = []
    for cols, offset in ((WINDOW, WINDOW), (s_new, 0)):
        s2, c = np.meshgrid(np.arange(SAMPLE_SEQS), np.arange(cols), indexing="ij")
        s2, c = s2.reshape(1, -1), c.reshape(1, -1)
        dist = offset + t - c + 0 * s2
        ok = (s == s2) & (dist >= 0) & (dist < WINDOW)
        bucket = np.where(ok, _rel_bucket_np(dist), -1)[None]
        tiles = _bias_tiles(table, bucket, N_KV_HEADS, GROUP, GROUP)
        out.append(tiles.reshape(N_KV_HEADS, GROUP * bucket.shape[1], bucket.shape[2]))
    return out


def kernel(x_prompt, x_sample, cache_k, cache_v, state_pool, meta_tokens, rel_bias_table, w_in,
           w_pool_mix, pool_scale, attn_sinks, w_up_pool, w_up_attn, w_out, ln1_g, ln1_b,
           peer_w_query, peer_sub_keys, peer_u, peer_v, ln2_g, ln2_b):
    batch, seq, d = x_prompt.shape
    nb, s_new, _ = x_sample.shape
    assert w_in.shape[0] == DEPTH and d == D_MODEL and seq % ATTN_TILE == 0
    assert N_META >= max(POOL_WINDOWS) - 1 and cache_k.shape[2] == WINDOW

    w_in0 = w_in[0].astype(BF16)
    w_pqkv, w_gates = w_in0[:, :OFF_GA], w_in0[:, OFF_GA:]
    wmix = w_pool_mix[0].astype(BF16)
    scale = pool_scale[0].reshape(1, POOL_WIDTH)
    wup, wua, wout = w_up_pool[0].astype(BF16), w_up_attn[0].astype(BF16), w_out[0].astype(BF16)
    g1, b1 = ln1_g[0].reshape(1, d), ln1_b[0].reshape(1, d)
    g2, b2 = ln2_g[0].reshape(1, d), ln2_b[0].reshape(1, d)
    wq = peer_w_query[0].astype(BF16)
    keys = peer_sub_keys[0].reshape(2 * PEER_HEADS, N_KEYS, PEER_HALF).astype(BF16)
    u = peer_u[0]
    vt = peer_v[0]
    sinks = attn_sinks[0].astype(F32)
    table = rel_bias_table.astype(F32)
    bias_p = _prompt_bias(table)
    bias_c, bias_n = _sample_bias(table, s_new)

    xp = x_prompt.reshape(batch * seq, d)
    xs = x_sample.reshape(nb * s_new, d)
    xm = jnp.concatenate([jnp.zeros((ATTN_BLOCK - N_META, d), F32), meta_tokens.astype(F32)], axis=0)

    p_p, q_p, k_p, v_p = _inproj(xp, w_pqkv)
    p_s, q_s, k_s, v_s = _inproj(xs, w_pqkv)
    p_m, _, k_m, v_m = _inproj(xm, w_pqkv)

    pooled_p, attn_p = _prompt_mix(sinks, q_p, k_p, v_p, p_p, k_m, v_m, p_m[ATTN_BLOCK - N_META:],
                                   bias_p, batch, seq)
    ck = cache_k[0].reshape(nb, WINDOW, KV_DIM)
    cv = cache_v[0].reshape(nb, WINDOW, KV_DIM)
    pooled_s, attn_s, newk, newv, newp = _sample_mix(sinks, q_s, k_s, v_s, p_s, ck, cv,
                                                     state_pool[0], bias_c, bias_n, s_new)

    outs = []
    for x, pooled, attn in ((xp, pooled_p, attn_p), (xs, pooled_s, attn_s)):
        x1 = _merge(x, pooled, attn, w_gates, wmix, scale, wup, wua, wout, g1, b1)
        cnt, e1, sel = _peer_scores(x1, wq, keys)
        outs.append(_peer_dense(x1, cnt, e1, sel, u, vt, g2, b2))

    w_keep = min(WINDOW, seq + N_META)
    kv_shape = (batch, seq, N_KV_HEADS, HEAD_DIM)
    return (outs[0].reshape(batch, seq, d),
            outs[1].reshape(nb, s_new, d),
            k_p.reshape(kv_shape)[None, :, seq - w_keep:],
            v_p.reshape(kv_shape)[None, :, seq - w_keep:],
            p_p.reshape(batch, seq, POOL_WIDTH)[None, :, seq - POOL_STATE:],
            newk.reshape(1, nb, WINDOW, N_KV_HEADS, HEAD_DIM),
            newv.reshape(1, nb, WINDOW, N_KV_HEADS, HEAD_DIM),
            newp[None])
```

```python
import functools
import math

import jax
import jax.numpy as jnp
import numpy as np
from jax import lax
from jax.experimental import pallas as pl
from jax.experimental.pallas import tpu as pltpu

F32 = jnp.float32
BF16 = jnp.bfloat16

D_MODEL = 1024
N_META = 16
POOL_WIDTH = 512
POOL_WINDOWS = (2, 4, 8, 16)
POOL_GROUP_DIM = 128
POOL_STATE = 15
HEAD_DIM = 64
N_HEADS = 8
N_KV_HEADS = 2
GROUP = N_HEADS // N_KV_HEADS
WINDOW = 128
ATTN_BLOCK = 128
ATTN_SCALE = HEAD_DIM ** -0.5
NUM_BUCKETS = 32
MAX_DISTANCE = 128
Q_DIM = N_HEADS * HEAD_DIM
KV_DIM = N_KV_HEADS * HEAD_DIM
OFF_Q = POOL_WIDTH
OFF_K = OFF_Q + Q_DIM
OFF_V = OFF_K + KV_DIM
OFF_GA = OFF_V + KV_DIM
PEER_HEADS = 8
N_KEYS = 128
PEER_TOPK = 16
PEER_HALF = 128
DEPTH = 1
ALPHA = (2 * DEPTH) ** 0.25
LN_EPS = 1e-5
NEG = -1e30

LANES = 128
SUBLANES = 8
VMEM_LIMIT = 56 * 1024 * 1024

TOK_TILE = 512
ATTN_TILE = 512
SAMPLE_SEQS = 8
EXPERT_CHUNK = 1024


def _cparams(sem):
    return pltpu.CompilerParams(dimension_semantics=sem, vmem_limit_bytes=VMEM_LIMIT)


def _inproj_kernel(x_ref, w_ref, p_ref, q_ref, k_ref, v_ref):
    z = jnp.dot(x_ref[...].astype(BF16), w_ref[...], preferred_element_type=F32)
    p_ref[...] = z[:, :OFF_Q]
    q_ref[...] = (z[:, OFF_Q:OFF_K] * ATTN_SCALE).astype(BF16)
    k_ref[...] = z[:, OFF_K:OFF_V]
    v_ref[...] = z[:, OFF_V:OFF_GA]


def _inproj(x, w_pqkv):
    rows = x.shape[0]
    tm = min(TOK_TILE, rows)
    assert rows % tm == 0
    row = lambda c: pl.BlockSpec((tm, c), lambda i: (i, 0))
    return pl.pallas_call(
        _inproj_kernel,
        grid=(rows // tm,),
        in_specs=[row(D_MODEL), pl.BlockSpec((D_MODEL, OFF_GA), lambda i: (0, 0))],
        out_specs=[row(POOL_WIDTH), row(Q_DIM), row(KV_DIM), row(KV_DIM)],
        out_shape=[jax.ShapeDtypeStruct((rows, POOL_WIDTH), F32),
                   jax.ShapeDtypeStruct((rows, Q_DIM), BF16),
                   jax.ShapeDtypeStruct((rows, KV_DIM), F32),
                   jax.ShapeDtypeStruct((rows, KV_DIM), F32)],
        compiler_params=_cparams(("parallel",)),
        name="inproj",
    )(x, w_pqkv)


def _window_pool(ext_ref, first, rows, out_ref):
    for g, w in enumerate(POOL_WINDOWS):
        cols = slice(g * POOL_GROUP_DIM, (g + 1) * POOL_GROUP_DIM)
        cur = ext_ref[first:first + rows, cols]
        acc = cur
        for r in range(1, w):
            acc = acc + ext_ref[first - r:first - r + rows, cols]
        out_ref[:, cols] = (acc * (1.0 / w) - cur).astype(out_ref.dtype)


def _window_pool_tall(ext_ref, first, rows, out_ref):
    for g, w in enumerate(POOL_WINDOWS):
        cols = slice(g * POOL_GROUP_DIM, (g + 1) * POOL_GROUP_DIM)
        x = ext_ref[:, cols]
        s, span = x, 1
        while span < w:
            s = s + pltpu.roll(s, span, 0)
            span *= 2
        out_ref[:, cols] = (s[first:first + rows] * (1.0 / w) - x[first:first + rows]).astype(out_ref.dtype)


def _sink_softmax(s, sink):
    m = jnp.maximum(jnp.max(s, axis=-1, keepdims=True), sink)
    e = jnp.exp(s - m)
    denom = jnp.sum(e, axis=-1, keepdims=True) + jnp.exp(sink - m)
    return e * (1.0 / denom)


_NT = (((1,), (1,)), ((), ()))


def _prompt_mix_kernel(sink_ref, q_ref, k_ref, v_ref, p_ref, kprev_ref, vprev_ref, pprev_ref,
                       mk_ref, mv_ref, mp_ref, bias_ref, pooled_ref, attn_ref,
                       kbuf, vbuf, pbuf, sbuf, prob_buf):
    first = pl.program_id(1) == 0
    hist = ATTN_BLOCK
    kbuf[0:hist, :] = jnp.where(first, mk_ref[...], kprev_ref[...]).astype(BF16)
    vbuf[0:hist, :] = jnp.where(first, mv_ref[...], vprev_ref[...]).astype(BF16)
    kbuf[hist:, :] = k_ref[...].astype(BF16)
    vbuf[hist:, :] = v_ref[...].astype(BF16)
    pbuf[0:N_META, :] = jnp.where(first, mp_ref[...], pprev_ref[...])
    pbuf[N_META:, :] = p_ref[...]

    _window_pool_tall(pbuf, N_META, ATTN_TILE, pooled_ref)

    first_i = jnp.where(first, 1, 0)
    blocks = [(j, kh) for j in range(ATTN_TILE // ATTN_BLOCK) for kh in range(N_KV_HEADS)]
    rows = lambda j: slice(j * ATTN_BLOCK, (j + 1) * ATTN_BLOCK)
    keys = lambda j: slice(j * ATTN_BLOCK, j * ATTN_BLOCK + 2 * ATTN_BLOCK)
    head = lambda h: slice(h * HEAD_DIM, (h + 1) * HEAD_DIM)
    for i, (j, kh) in enumerate(blocks):
        qg = jnp.concatenate([q_ref[rows(j), head(kh * GROUP + g)] for g in range(GROUP)], axis=0)
        s = lax.dot_general(qg, kbuf[keys(j), head(kh)], _NT, preferred_element_type=F32)
        sbuf[i] = s + (bias_ref[first_i, kh] if j == 0 else bias_ref[0, kh])
    for i, (j, kh) in enumerate(blocks):
        sink = jnp.concatenate([jnp.full((ATTN_BLOCK, 1), sink_ref[kh * GROUP + g], F32) for g in range(GROUP)], axis=0)
        prob_buf[i] = _sink_softmax(sbuf[i], sink).astype(BF16)
    for i, (j, kh) in enumerate(blocks):
        o = jnp.dot(prob_buf[i], vbuf[keys(j), head(kh)], preferred_element_type=F32)
        for g in range(GROUP):
            attn_ref[rows(j), head(kh * GROUP + g)] = o[g * ATTN_BLOCK:(g + 1) * ATTN_BLOCK].astype(attn_ref.dtype)


def _prompt_mix(sinks, q, k, v, p, mk, mv, mp, bias, batch, seq):
    nt = seq // ATTN_TILE
    blocks_per_tile = ATTN_TILE // ATTN_BLOCK
    cur = lambda c: pl.BlockSpec((ATTN_TILE, c), lambda b, s: (b * nt + s, 0))
    prev_blk = lambda b, s: (jnp.maximum((b * nt + s) * blocks_per_tile - 1, 0), 0)
    prev_p = lambda b, s: (jnp.maximum((b * nt + s) * (ATTN_TILE // N_META) - 1, 0), 0)
    const2 = lambda b, s: (0, 0)
    rows = batch * seq
    return pl.pallas_call(
        _prompt_mix_kernel,
        grid=(batch, nt),
        in_specs=[pl.BlockSpec(memory_space=pltpu.SMEM),
                  cur(Q_DIM), cur(KV_DIM), cur(KV_DIM), cur(POOL_WIDTH),
                  pl.BlockSpec((ATTN_BLOCK, KV_DIM), prev_blk),
                  pl.BlockSpec((ATTN_BLOCK, KV_DIM), prev_blk),
                  pl.BlockSpec((N_META, POOL_WIDTH), prev_p),
                  pl.BlockSpec((ATTN_BLOCK, KV_DIM), const2),
                  pl.BlockSpec((ATTN_BLOCK, KV_DIM), const2),
                  pl.BlockSpec((N_META, POOL_WIDTH), const2),
                  pl.BlockSpec((2, N_KV_HEADS, GROUP * ATTN_BLOCK, 2 * ATTN_BLOCK), lambda b, s: (0, 0, 0, 0))],
        out_specs=[cur(POOL_WIDTH), cur(Q_DIM)],
        out_shape=[jax.ShapeDtypeStruct((rows, POOL_WIDTH), BF16),
                   jax.ShapeDtypeStruct((rows, Q_DIM), BF16)],
        scratch_shapes=[pltpu.VMEM((ATTN_BLOCK + ATTN_TILE, KV_DIM), BF16),
                        pltpu.VMEM((ATTN_BLOCK + ATTN_TILE, KV_DIM), BF16),
                        pltpu.VMEM((N_META + ATTN_TILE, POOL_WIDTH), F32),
                        pltpu.VMEM((blocks_per_tile * N_KV_HEADS, GROUP * ATTN_BLOCK, 2 * ATTN_BLOCK), F32),
                        pltpu.VMEM((blocks_per_tile * N_KV_HEADS, GROUP * ATTN_BLOCK, 2 * ATTN_BLOCK), BF16)],
        compiler_params=_cparams(("parallel", "parallel")),
        name="prompt_mix",
    )(sinks, q, k, v, p, k, v, p, mk, mv, mp, bias)


def _sample_mix_kernel(sink_ref, q_ref, k_ref, v_ref, p_ref, ck_ref, cv_ref, st_ref,
                       biasc_ref, biasn_ref,
                       pooled_ref, attn_ref, newk_ref, newv_ref, newp_ref, ext, pooled_buf):
    nseq, w_cache, s_new = SAMPLE_SEQS, WINDOW, q_ref.shape[0] // SAMPLE_SEQS
    keep = POOL_STATE - s_new
    for i in range(nseq):
        new = slice(i * s_new, (i + 1) * s_new)
        ext[1:1 + POOL_STATE, :] = st_ref[i]
        ext[1 + POOL_STATE:1 + POOL_STATE + s_new, :] = p_ref[new, :]
        _window_pool(ext, 1 + POOL_STATE, s_new, pooled_buf)
        pooled_ref[new, :] = pooled_buf[...].astype(pooled_ref.dtype)
        newp_ref[i, 0:keep, :] = st_ref[i, s_new:POOL_STATE, :]
        newp_ref[i, keep:POOL_STATE, :] = p_ref[new, :]
        newk_ref[i, 0:w_cache - s_new, :] = ck_ref[i, s_new:w_cache, :]
        newk_ref[i, w_cache - s_new:w_cache, :] = k_ref[new, :]
        newv_ref[i, 0:w_cache - s_new, :] = cv_ref[i, s_new:w_cache, :]
        newv_ref[i, w_cache - s_new:w_cache, :] = v_ref[new, :]

    kc = ck_ref[...].reshape(nseq * w_cache, KV_DIM).astype(BF16)
    vc = cv_ref[...].reshape(nseq * w_cache, KV_DIM).astype(BF16)
    kn = k_ref[...].astype(BF16)
    vn = v_ref[...].astype(BF16)
    q = q_ref[...]
    nq = q.shape[0]
    for kh in range(N_KV_HEADS):
        kvc = slice(kh * HEAD_DIM, (kh + 1) * HEAD_DIM)
        qg = jnp.concatenate([q[:, (kh * GROUP + g) * HEAD_DIM:(kh * GROUP + g + 1) * HEAD_DIM]
                              for g in range(GROUP)], axis=0)
        sc = lax.dot_general(qg, kc[:, kvc], _NT, preferred_element_type=F32) + biasc_ref[kh]
        sn = lax.dot_general(qg, kn[:, kvc], _NT, preferred_element_type=F32) + biasn_ref[kh]
        sink = jnp.concatenate([jnp.full((nq, 1), sink_ref[kh * GROUP + g], F32) for g in range(GROUP)], axis=0)
        m = jnp.maximum(jnp.maximum(jnp.max(sc, axis=-1, keepdims=True),
                                    jnp.max(sn, axis=-1, keepdims=True)), sink)
        ec = jnp.exp(sc - m)
        en = jnp.exp(sn - m)
        denom = (jnp.sum(ec, axis=-1, keepdims=True) + jnp.sum(en, axis=-1, keepdims=True)
                 + jnp.exp(sink - m))
        inv = 1.0 / denom
        o = (jnp.dot((ec * inv).astype(BF16), vc[:, kvc], preferred_element_type=F32)
             + jnp.dot((en * inv).astype(BF16), vn[:, kvc], preferred_element_type=F32))
        for g in range(GROUP):
            h = kh * GROUP + g
            attn_ref[:, h * HEAD_DIM:(h + 1) * HEAD_DIM] = o[g * nq:(g + 1) * nq].astype(attn_ref.dtype)


def _sample_mix(sinks, q, k, v, p, cache_k, cache_v, state, biasc, biasn, s_new):
    nb = cache_k.shape[0]
    rows = SAMPLE_SEQS * s_new
    tok = lambda c: pl.BlockSpec((rows, c), lambda i: (i, 0))
    seq3 = lambda r, c: pl.BlockSpec((SAMPLE_SEQS, r, c), lambda i: (i, 0, 0))
    const3 = lambda a: pl.BlockSpec(a.shape, lambda i: (0, 0, 0))
    return pl.pallas_call(
        _sample_mix_kernel,
        grid=(nb // SAMPLE_SEQS,),
        in_specs=[pl.BlockSpec(memory_space=pltpu.SMEM),
                  tok(Q_DIM), tok(KV_DIM), tok(KV_DIM), tok(POOL_WIDTH),
                  seq3(WINDOW, KV_DIM), seq3(WINDOW, KV_DIM), seq3(POOL_STATE, POOL_WIDTH),
                  const3(biasc), const3(biasn)],
        out_specs=[tok(POOL_WIDTH), tok(Q_DIM),
                   seq3(WINDOW, KV_DIM), seq3(WINDOW, KV_DIM), seq3(POOL_STATE, POOL_WIDTH)],
        out_shape=[jax.ShapeDtypeStruct((nb * s_new, POOL_WIDTH), F32),
                   jax.ShapeDtypeStruct((nb * s_new, Q_DIM), F32),
                   jax.ShapeDtypeStruct((nb, WINDOW, KV_DIM), F32),
                   jax.ShapeDtypeStruct((nb, WINDOW, KV_DIM), F32),
                   jax.ShapeDtypeStruct((nb, POOL_STATE, POOL_WIDTH), F32)],
        scratch_shapes=[pltpu.VMEM((1 + POOL_STATE + SUBLANES, POOL_WIDTH), F32),
                        pltpu.VMEM((s_new, POOL_WIDTH), F32)],
        compiler_params=_cparams(("parallel",)),
        name="sample_mix",
    )(sinks, q, k, v, p, cache_k, cache_v, state, biasc, biasn)


def _layer_norm(x, g, b):
    mu = jnp.mean(x, axis=-1, keepdims=True)
    xc = x - mu
    var = jnp.mean(xc * xc, axis=-1, keepdims=True)
    return xc * lax.rsqrt(var + LN_EPS) * g + b


def _merge_kernel(x_ref, pooled_ref, attn_ref, wg_ref, wmix_ref, scale_ref, wup_ref, wua_ref,
                  wout_ref, g_ref, b_ref, y_ref):
    x = x_ref[...]
    glog = jnp.dot(x.astype(BF16), wg_ref[...], preferred_element_type=F32)
    g_pool = jax.nn.sigmoid(glog[:, :D_MODEL])
    g_attn = jax.nn.sigmoid(glog[:, D_MODEL:])
    pooled = pooled_ref[...].astype(BF16)
    mixed = jnp.concatenate(
        [jnp.dot(pooled[:, g * POOL_GROUP_DIM:(g + 1) * POOL_GROUP_DIM], wmix_ref[g],
                 preferred_element_type=F32) for g in range(len(POOL_WINDOWS))], axis=1)
    pool_out = (mixed * scale_ref[...]).astype(BF16)
    a = jnp.dot(pool_out, wup_ref[...], preferred_element_type=F32)
    b = jnp.dot(attn_ref[...].astype(BF16), wua_ref[...], preferred_element_type=F32)
    m = (g_pool * a + g_attn * b).astype(BF16)
    r = jnp.dot(m, wout_ref[...], preferred_element_type=F32)
    y_ref[...] = _layer_norm(ALPHA * x + r, g_ref[...], b_ref[...])


def _merge(x, pooled, attn, wg, wmix, scale, wup, wua, wout, ln_g, ln_b):
    rows = x.shape[0]
    tm = min(TOK_TILE, rows)
    assert rows % tm == 0
    row = lambda c: pl.BlockSpec((tm, c), lambda i: (i, 0))
    full = lambda a: pl.BlockSpec(a.shape, lambda i: (0,) * a.ndim)
    weights = (wg, wmix, scale, wup, wua, wout, ln_g, ln_b)
    return pl.pallas_call(
        _merge_kernel,
        grid=(rows // tm,),
        in_specs=[row(D_MODEL), row(POOL_WIDTH), row(Q_DIM)] + [full(a) for a in weights],
        out_specs=row(D_MODEL),
        out_shape=jax.ShapeDtypeStruct((rows, D_MODEL), F32),
        compiler_params=_cparams(("parallel",)),
        name="merge",
    )(x, pooled, attn, *weights)


def _oddeven_merge_sort_pairs(n):
    pairs = []
    p = 1
    while p < n:
        k = p
        while k >= 1:
            for j in range(k % p, n - k, 2 * k):
                for i in range(min(k, n - j - k)):
                    if (i + j) // (2 * p) == (i + j + k) // (2 * p):
                        pairs.append((i + j, i + j + k))
            k //= 2
        p *= 2
    return pairs


def _bitonic_merge_pairs(n):
    pairs = []
    k = n // 2
    while k >= 1:
        pairs += [(i, i + k) for i in range(n) if not i & k]
        k //= 2
    return pairs


def _apply_network(vals, pairs):
    vals = list(vals)
    for i, j in pairs:
        a, b = vals[i], vals[j]
        if b is None:
            continue
        if a is None:
            vals[i], vals[j] = b, None
        else:
            vals[i], vals[j] = jnp.maximum(a, b), jnp.minimum(a, b)
    return vals


_SORT16 = _oddeven_merge_sort_pairs(PEER_TOPK)
_MERGE16 = _bitonic_merge_pairs(PEER_TOPK)


def _top16_rows(sc):
    n = sc.shape[0] // SUBLANES
    assert n == PEER_TOPK
    x = _apply_network([sc[k * SUBLANES:(k + 1) * SUBLANES, :] for k in range(n)], _SORT16)
    for shift in (4, 2, 1):
        y = [jnp.maximum(x[k], pltpu.roll(x[n - 1 - k], shift, 0)) for k in range(n)]
        x = _apply_network(y, _MERGE16)
    return x


def _best_sums(v1, v2):
    row = lax.broadcasted_iota(jnp.int32, v1[0].shape, 0)

    def one_per_sublane(vals):
        x = vals[SUBLANES - 1]
        for s in range(SUBLANES - 2, -1, -1):
            x = jnp.where(row == s, vals[s], x)
        return x

    lo, hi = one_per_sublane(v1[:SUBLANES]), one_per_sublane(v1[SUBLANES:])
    sums_lo = []
    for b in range(PEER_TOPK):
        n_valid = sum((a + 1) * (b + 1) <= PEER_TOPK for a in range(SUBLANES))
        s = lo + v2[b]
        sums_lo.append(s if n_valid == SUBLANES else jnp.where(row < n_valid, s, NEG))
    sums_hi = hi + v2[0]
    x, best = sums_hi, []
    for k in range(PEER_TOPK):
        best.append(jnp.maximum(sums_lo[k], x))
        x = jnp.minimum(sums_lo[k], x)
    for shift in (4, 2, 1):
        y = [jnp.maximum(best[k], pltpu.roll(best[PEER_TOPK - 1 - k], shift, 0)) for k in range(PEER_TOPK)]
        best = _apply_network(y, _MERGE16)
    return best, sums_lo, sums_hi


def _peer_scores_kernel(x_ref, wq_ref, keys_ref, cnt_ref, e1_ref, sel_ref):
    q = jnp.dot(x_ref[...].astype(BF16), wq_ref[...], preferred_element_type=F32).astype(BF16)
    for h in range(PEER_HEADS):
        sc = []
        for c in range(2):
            col = (h * 2 + c) * PEER_HALF
            sc.append(lax.dot_general(keys_ref[h * 2 + c], q[:, col:col + PEER_HALF], _NT,
                                      preferred_element_type=F32))
        for t in range(sc[0].shape[1] // LANES):
            lanes = slice(t * LANES, (t + 1) * LANES)
            s1, s2 = sc[0][:, lanes], sc[1][:, lanes]
            v1, v2 = _top16_rows(s1), _top16_rows(s2)
            best, sums_lo, _ = _best_sums(v1, v2)
            tau = best[PEER_TOPK - 1]
            z = jnp.ones_like(tau)
            for r in range(1, PEER_TOPK):
                z = z + jnp.exp(best[r] - best[0])
            inv_z = 1.0 / z
            cnt_lo = jnp.zeros_like(tau)
            for s in sums_lo:
                cnt_lo = cnt_lo + jnp.where(s >= tau, 1.0, 0.0)
            row = lambda r: r[0:1, :]
            cnt = jnp.where((s1 <= row(v1[SUBLANES])) & (s1 + row(v2[0]) >= row(tau)), 1.0, 0.0)
            rank = jnp.zeros_like(s2)
            for a in range(PEER_TOPK):
                if a < SUBLANES:
                    cnt = jnp.where(s1 == row(v1[a]), cnt_lo[a:a + 1, :], cnt)
                rank = jnp.where(row(v2[a]) > s2, float(a + 1), rank)
            cnt_ref[h, t] = cnt
            e1_ref[h, t] = jnp.exp(s1 - row(v1[0])) * row(inv_z)
            rank_words = pltpu.bitcast(rank.astype(BF16), jnp.uint32)
            e2_words = pltpu.bitcast(jnp.exp(s2 - row(v2[0])).astype(BF16), jnp.uint32)
            for k in range(N_KEYS // (2 * SUBLANES)):
                words = slice(k * SUBLANES, (k + 1) * SUBLANES)
                sel_ref[h, t, 2 * k * SUBLANES:(2 * k + 1) * SUBLANES, :] = rank_words[words]
                sel_ref[h, t, (2 * k + 1) * SUBLANES:(2 * k + 2) * SUBLANES, :] = e2_words[words]


def _peer_scores(x, wq, keys):
    rows = x.shape[0]
    tm = min(TOK_TILE, rows)
    assert rows % tm == 0
    return pl.pallas_call(
        _peer_scores_kernel,
        grid=(rows // tm,),
        in_specs=[pl.BlockSpec((tm, D_MODEL), lambda i: (i, 0)),
                  pl.BlockSpec(wq.shape, lambda i: (0, 0)),
                  pl.BlockSpec(keys.shape, lambda i: (0, 0, 0))],
        out_specs=[pl.BlockSpec((PEER_HEADS, tm // LANES, N_KEYS, LANES), lambda i: (0, i, 0, 0))] * 3,
        out_shape=[jax.ShapeDtypeStruct((PEER_HEADS, rows // LANES, N_KEYS, LANES), dt)
                   for dt in (F32, F32, jnp.uint32)],
        compiler_params=_cparams(("parallel",)),
        name="peer_scores",
    )(x, wq, keys)


def _gelu(x):
    return 0.5 * x * (1.0 + lax.erf(x * math.sqrt(0.5)))


def _peer_dense_kernel(x_ref, cnt_ref, e1_ref, sel_ref, u_ref, v_ref, g_ref, b_ref, y_ref,
                       xt, rowb, hid, wt, acc):
    c = pl.program_id(1)
    rows_per_chunk = EXPERT_CHUNK // N_KEYS
    ncol = xt.shape[1] // LANES
    packed = 2 * SUBLANES
    assert rows_per_chunk == SUBLANES

    @pl.when(c == 0)
    def _():
        xt[...] = x_ref[...].T.astype(BF16)
        acc[...] = jnp.zeros_like(acc)

    def build_rows(chunk, heads):
        for h in heads:
            for col in range(ncol):
                for r in range(SUBLANES):
                    row = pl.ds(chunk * SUBLANES + r, packed, stride=0)
                    rowb[r, h, col] = cnt_ref[h, col, row, :].astype(BF16)
                    rowb[r, PEER_HEADS + h, col] = e1_ref[h, col, row, :].astype(BF16)

    def mask_row(il, carry):
        base = pl.multiple_of(il * N_KEYS, N_KEYS)
        for col in range(ncol):
            gate = [None] * (N_KEYS // packed)
            for h in range(PEER_HEADS):
                cnt = rowb[il, h, col]
                g1 = rowb[il, PEER_HEADS + h, col]
                for k in range(N_KEYS // packed):
                    rank = pltpu.bitcast(sel_ref[h, col, 2 * k * SUBLANES:(2 * k + 1) * SUBLANES, :], BF16)
                    g2 = pltpu.bitcast(sel_ref[h, col, (2 * k + 1) * SUBLANES:(2 * k + 2) * SUBLANES, :], BF16)
                    g = jnp.where(rank < cnt, g2 * g1, jnp.zeros((), BF16))
                    gate[k] = g if gate[k] is None else gate[k] + g
            for k in range(N_KEYS // packed):
                act = _gelu(hid[col, pl.ds(base + k * packed, packed), :]).astype(BF16)
                wt[pl.ds(base + k * packed, packed), col * LANES:(col + 1) * LANES] = act * gate[k]
        return carry

    n_slabs = 4
    slab = EXPERT_CHUNK // n_slabs

    for i in range(n_slabs):
        rows = slice(i * slab, (i + 1) * slab)
        res = jnp.dot(u_ref[rows, :].astype(BF16), xt[...], preferred_element_type=F32)
        for col in range(ncol):
            hid[col, rows, :] = res[:, col * LANES:(col + 1) * LANES]
        build_rows(c, range(i * PEER_HEADS // n_slabs, (i + 1) * PEER_HEADS // n_slabs))
    lax.fori_loop(0, rows_per_chunk, mask_row, 0)
    acc[...] += lax.dot_general(v_ref[...].astype(BF16), wt[...], (((0,), (0,)), ((), ())),
                                preferred_element_type=F32)

    @pl.when(c == pl.num_programs(1) - 1)
    def _():
        y_ref[...] = _layer_norm(ALPHA * x_ref[...] + acc[...].T, g_ref[...], b_ref[...])


def _peer_dense(x, cnt, e1, sel, u, vt, ln_g, ln_b):
    rows = x.shape[0]
    tm = min(TOK_TILE, rows)
    assert rows % tm == 0
    n_exp = u.shape[0]
    assert n_exp % EXPERT_CHUNK == 0
    return pl.pallas_call(
        _peer_dense_kernel,
        grid=(rows // tm, n_exp // EXPERT_CHUNK),
        in_specs=[pl.BlockSpec((tm, D_MODEL), lambda t, c: (t, 0))]
                 + [pl.BlockSpec((PEER_HEADS, tm // LANES, N_KEYS, LANES), lambda t, c: (0, t, 0, 0))] * 3
                 + [pl.BlockSpec((EXPERT_CHUNK, D_MODEL), lambda t, c: (c, 0)),
                    pl.BlockSpec((EXPERT_CHUNK, D_MODEL), lambda t, c: (c, 0)),
                    pl.BlockSpec((1, D_MODEL), lambda t, c: (0, 0)),
                    pl.BlockSpec((1, D_MODEL), lambda t, c: (0, 0))],
        out_specs=pl.BlockSpec((tm, D_MODEL), lambda t, c: (t, 0)),
        out_shape=jax.ShapeDtypeStruct((rows, D_MODEL), F32),
        scratch_shapes=[pltpu.VMEM((D_MODEL, tm), BF16),
                        pltpu.VMEM((SUBLANES, 2 * PEER_HEADS, tm // LANES, 2 * SUBLANES, LANES), BF16),
                        pltpu.VMEM((tm // LANES, EXPERT_CHUNK, LANES), F32),
                        pltpu.VMEM((EXPERT_CHUNK, tm), BF16),
                        pltpu.VMEM((D_MODEL, tm), F32)],
        compiler_params=_cparams(("parallel", "arbitrary")),
        name="peer_dense",
    )(x, cnt, e1, sel, u, vt, ln_g, ln_b)


def _rel_bucket_np(dist):
    n = np.maximum(dist, 0)
    max_exact = NUM_BUCKETS // 2
    large = max_exact + (np.log(np.maximum(n, max_exact).astype(np.float32) / max_exact)
                         / math.log(MAX_DISTANCE / max_exact) * (NUM_BUCKETS - max_exact)).astype(np.int32)
    return np.where(n < max_exact, n, np.minimum(large, NUM_BUCKETS - 1))


def _bias_kernel(table_ref, bucket_ref, out_ref, *, head_stride):
    head = pl.program_id(0) * head_stride + pl.program_id(1)
    bucket = bucket_ref[0]
    acc = jnp.full(bucket.shape, NEG, F32)
    for b in range(NUM_BUCKETS):
        acc = jnp.where(bucket == b, table_ref[b, head], acc)
    out_ref[0, 0] = acc


def _bias_tiles(table, bucket, n_outer, n_inner, head_stride):
    r, c = bucket.shape[1:]
    bmap = (lambda a, b: (a, 0, 0)) if bucket.shape[0] > 1 else (lambda a, b: (0, 0, 0))
    return pl.pallas_call(
        functools.partial(_bias_kernel, head_stride=head_stride),
        grid=(n_outer, n_inner),
        in_specs=[pl.BlockSpec(memory_space=pltpu.SMEM), pl.BlockSpec((1, r, c), bmap)],
        out_specs=pl.BlockSpec((1, 1, r, c), lambda a, b: (a, b, 0, 0)),
        out_shape=jax.ShapeDtypeStruct((n_outer, n_inner, r, c), F32),
        compiler_params=_cparams(("parallel", "parallel")),
        name="bias_tiles",
    )(table, jnp.asarray(bucket, jnp.int32))


def _prompt_bias(table):
    qi = np.arange(ATTN_BLOCK)[:, None]
    kj = np.arange(2 * ATTN_BLOCK)[None, :]
    dist = ATTN_BLOCK + qi - kj
    ok = (dist >= 0) & (dist < WINDOW)
    ok = np.stack([ok, ok & (kj >= ATTN_BLOCK - N_META)])
    bucket = np.where(ok, _rel_bucket_np(dist)[None], -1)
    tiles = _bias_tiles(table, bucket, 2, N_HEADS, 0)
    return tiles.reshape(2, N_KV_HEADS, GROUP * ATTN_BLOCK, 2 * ATTN_BLOCK)


def _sample_bias(table, s_new):
    s, t = np.meshgrid(np.arange(SAMPLE_SEQS), np.arange(s_new), indexing="ij")
    s, t = s.reshape(-1, 1), t.reshape(-1, 1)
    out = []
    for cols, offset in ((WINDOW, WINDOW), (s_new, 0)):
        s2, c = np.meshgrid(np.arange(SAMPLE_SEQS), np.arange(cols), indexing="ij")
        s2, c = s2.reshape(1, -1), c.reshape(1, -1)
        dist = offset + t - c + 0 * s2
        ok = (s == s2) & (dist >= 0) & (dist < WINDOW)
        bucket = np.where(ok, _rel_bucket_np(dist), -1)[None]
        tiles = _bias_tiles(table, bucket, N_KV_HEADS, GROUP, GROUP)
        out.append(tiles.reshape(N_KV_HEADS, GROUP * bucket.shape[1], bucket.shape[2]))
    return out


def kernel(x_prompt, x_sample, cache_k, cache_v, state_pool, meta_tokens, rel_bias_table, w_in,
           w_pool_mix, pool_scale, attn_sinks, w_up_pool, w_up_attn, w_out, ln1_g, ln1_b,
           peer_w_query, peer_sub_keys, peer_u, peer_v, ln2_g, ln2_b):
    batch, seq, d = x_prompt.shape
    nb, s_new, _ = x_sample.shape
    assert w_in.shape[0] == DEPTH and d == D_MODEL and seq % ATTN_TILE == 0
    assert N_META >= max(POOL_WINDOWS) - 1 and cache_k.shape[2] == WINDOW

    w_in0 = w_in[0].astype(BF16)
    w_pqkv, w_gates = w_in0[:, :OFF_GA], w_in0[:, OFF_GA:]
    wmix = w_pool_mix[0].astype(BF16)
    scale = pool_scale[0].reshape(1, POOL_WIDTH)
    wup, wua, wout = w_up_pool[0].astype(BF16), w_up_attn[0].astype(BF16), w_out[0].astype(BF16)
    g1, b1 = ln1_g[0].reshape(1, d), ln1_b[0].reshape(1, d)
    g2, b2 = ln2_g[0].reshape(1, d), ln2_b[0].reshape(1, d)
    wq = peer_w_query[0].astype(BF16)
    keys = peer_sub_keys[0].reshape(2 * PEER_HEADS, N_KEYS, PEER_HALF).astype(BF16)
    u = peer_u[0]
    vt = peer_v[0]
    sinks = attn_sinks[0].astype(F32)
    table = rel_bias_table.astype(F32)
    bias_p = _prompt_bias(table)
    bias_c, bias_n = _sample_bias(table, s_new)

    xp = x_prompt.reshape(batch * seq, d)
    xs = x_sample.reshape(nb * s_new, d)
    xm = jnp.concatenate([jnp.zeros((ATTN_BLOCK - N_META, d), F32), meta_tokens.astype(F32)], axis=0)

    p_p, q_p, k_p, v_p = _inproj(xp, w_pqkv)
    p_s, q_s, k_s, v_s = _inproj(xs, w_pqkv)
    p_m, _, k_m, v_m = _inproj(xm, w_pqkv)

    pooled_p, attn_p = _prompt_mix(sinks, q_p, k_p, v_p, p_p, k_m, v_m, p_m[ATTN_BLOCK - N_META:],
                                   bias_p, batch, seq)
    ck = cache_k[0].reshape(nb, WINDOW, KV_DIM)
    cv = cache_v[0].reshape(nb, WINDOW, KV_DIM)
    pooled_s, attn_s, newk, newv, newp = _sample_mix(sinks, q_s, k_s, v_s, p_s, ck, cv,
                                                     state_pool[0], bias_c, bias_n, s_new)

    outs = []
    for x, pooled, attn in ((xp, pooled_p, attn_p), (xs, pooled_s, attn_s)):
        x1 = _merge(x, pooled, attn, w_gates, wmix, scale, wup, wua, wout, g1, b1)
        cnt, e1, sel = _peer_scores(x1, wq, keys)
        outs.append(_peer_dense(x1, cnt, e1, sel, u, vt, g2, b2))

    w_keep = min(WINDOW, seq + N_META)
    kv_shape = (batch, seq, N_KV_HEADS, HEAD_DIM)
    return (outs[0].reshape(batch, seq, d),
            outs[1].reshape(nb, s_new, d),
            k_p.reshape(kv_shape)[None, :, seq - w_keep:],
            v_p.reshape(kv_shape)[None, :, seq - w_keep:],
            p_p.reshape(batch, seq, POOL_WIDTH)[None, :, seq - POOL_STATE:],
            newk.reshape(1, nb, WINDOW, N_KV_HEADS, HEAD_DIM),
            newv.reshape(1, nb, WINDOW, N_KV_HEADS, HEAD_DIM),
            newp[None])
```

```python
import functools
import math

import jax
import jax.numpy as jnp
import numpy as np
from jax import lax
from jax.experimental import pallas as pl
from jax.experimental.pallas import tpu as pltpu

F32 = jnp.float32
BF16 = jnp.bfloat16

D_MODEL = 1024
N_META = 16
POOL_WIDTH = 512
POOL_WINDOWS = (2, 4, 8, 16)
POOL_GROUP_DIM = 128
POOL_STATE = 15
HEAD_DIM = 64
N_HEADS = 8
N_KV_HEADS = 2
GROUP = N_HEADS // N_KV_HEADS
WINDOW = 128
ATTN_BLOCK = 128
ATTN_SCALE = HEAD_DIM ** -0.5
NUM_BUCKETS = 32
MAX_DISTANCE = 128
Q_DIM = N_HEADS * HEAD_DIM
KV_DIM = N_KV_HEADS * HEAD_DIM
OFF_Q = POOL_WIDTH
OFF_K = OFF_Q + Q_DIM
OFF_V = OFF_K + KV_DIM
OFF_GA = OFF_V + KV_DIM
PEER_HEADS = 8
N_KEYS = 128
PEER_TOPK = 16
PEER_HALF = 128
DEPTH = 1
ALPHA = (2 * DEPTH) ** 0.25
LN_EPS = 1e-5
NEG = -1e30

LANES = 128
SUBLANES = 8
VMEM_LIMIT = 56 * 1024 * 1024

TOK_TILE = 512
ATTN_TILE = 512
SAMPLE_SEQS = 8
EXPERT_CHUNK = 1024


def _cparams(sem):
    return pltpu.CompilerParams(dimension_semantics=sem, vmem_limit_bytes=VMEM_LIMIT)


def _inproj_kernel(x_ref, w_ref, p_ref, q_ref, k_ref, v_ref):
    z = jnp.dot(x_ref[...].astype(BF16), w_ref[...], preferred_element_type=F32)
    p_ref[...] = z[:, :OFF_Q]
    q_ref[...] = (z[:, OFF_Q:OFF_K] * ATTN_SCALE).astype(BF16)
    k_ref[...] = z[:, OFF_K:OFF_V]
    v_ref[...] = z[:, OFF_V:OFF_GA]


def _inproj(x, w_pqkv):
    rows = x.shape[0]
    tm = min(TOK_TILE, rows)
    assert rows % tm == 0
    row = lambda c: pl.BlockSpec((tm, c), lambda i: (i, 0))
    return pl.pallas_call(
        _inproj_kernel,
        grid=(rows // tm,),
        in_specs=[row(D_MODEL), pl.BlockSpec((D_MODEL, OFF_GA), lambda i: (0, 0))],
        out_specs=[row(POOL_WIDTH), row(Q_DIM), row(KV_DIM), row(KV_DIM)],
        out_shape=[jax.ShapeDtypeStruct((rows, POOL_WIDTH), F32),
                   jax.ShapeDtypeStruct((rows, Q_DIM), BF16),
                   jax.ShapeDtypeStruct((rows, KV_DIM), F32),
                   jax.ShapeDtypeStruct((rows, KV_DIM), F32)],
        compiler_params=_cparams(("parallel",)),
        name="inproj",
    )(x, w_pqkv)


def _window_pool(ext_ref, first, rows, out_ref):
    for g, w in enumerate(POOL_WINDOWS):
        cols = slice(g * POOL_GROUP_DIM, (g + 1) * POOL_GROUP_DIM)
        cur = ext_ref[first:first + rows, cols]
        acc = cur
        for r in range(1, w):
            acc = acc + ext_ref[first - r:first - r + rows, cols]
        out_ref[:, cols] = (acc * (1.0 / w) - cur).astype(out_ref.dtype)


def _window_pool_tall(ext_ref, first, rows, out_ref):
    for g, w in enumerate(POOL_WINDOWS):
        cols = slice(g * POOL_GROUP_DIM, (g + 1) * POOL_GROUP_DIM)
        x = ext_ref[:, cols]
        s, span = x, 1
        while span < w:
            s = s + pltpu.roll(s, span, 0)
            span *= 2
        out_ref[:, cols] = (s[first:first + rows] * (1.0 / w) - x[first:first + rows]).astype(out_ref.dtype)


def _sink_softmax(s, sink):
    m = jnp.maximum(jnp.max(s, axis=-1, keepdims=True), sink)
    e = jnp.exp(s - m)
    denom = jnp.sum(e, axis=-1, keepdims=True) + jnp.exp(sink - m)
    return e * (1.0 / denom)


_NT = (((1,), (1,)), ((), ()))


def _prompt_mix_kernel(sink_ref, q_ref, k_ref, v_ref, p_ref, kprev_ref, vprev_ref, pprev_ref,
                       mk_ref, mv_ref, mp_ref, bias_ref, pooled_ref, attn_ref,
                       kbuf, vbuf, pbuf, sbuf, prob_buf):
    first = pl.program_id(1) == 0
    hist = ATTN_BLOCK
    kbuf[0:hist, :] = jnp.where(first, mk_ref[...], kprev_ref[...]).astype(BF16)
    vbuf[0:hist, :] = jnp.where(first, mv_ref[...], vprev_ref[...]).astype(BF16)
    kbuf[hist:, :] = k_ref[...].astype(BF16)
    vbuf[hist:, :] = v_ref[...].astype(BF16)
    pbuf[0:N_META, :] = jnp.where(first, mp_ref[...], pprev_ref[...])
    pbuf[N_META:, :] = p_ref[...]

    _window_pool_tall(pbuf, N_META, ATTN_TILE, pooled_ref)

    first_i = jnp.where(first, 1, 0)
    blocks = [(j, kh) for j in range(ATTN_TILE // ATTN_BLOCK) for kh in range(N_KV_HEADS)]
    rows = lambda j: slice(j * ATTN_BLOCK, (j + 1) * ATTN_BLOCK)
    keys = lambda j: slice(j * ATTN_BLOCK, j * ATTN_BLOCK + 2 * ATTN_BLOCK)
    head = lambda h: slice(h * HEAD_DIM, (h + 1) * HEAD_DIM)
    for i, (j, kh) in enumerate(blocks):
        qg = jnp.concatenate([q_ref[rows(j), head(kh * GROUP + g)] for g in range(GROUP)], axis=0)
        s = lax.dot_general(qg, kbuf[keys(j), head(kh)], _NT, preferred_element_type=F32)
        sbuf[i] = s + (bias_ref[first_i, kh] if j == 0 else bias_ref[0, kh])
    for i, (j, kh) in enumerate(blocks):
        sink = jnp.concatenate([jnp.full((ATTN_BLOCK, 1), sink_ref[kh * GROUP + g], F32) for g in range(GROUP)], axis=0)
        prob_buf[i] = _sink_softmax(sbuf[i], sink).astype(BF16)
    for i, (j, kh) in enumerate(blocks):
        o = jnp.dot(prob_buf[i], vbuf[keys(j), head(kh)], preferred_element_type=F32)
        for g in range(GROUP):
            attn_ref[rows(j), head(kh * GROUP + g)] = o[g * ATTN_BLOCK:(g + 1) * ATTN_BLOCK].astype(attn_ref.dtype)


def _prompt_mix(sinks, q, k, v, p, mk, mv, mp, bias, batch, seq):
    nt = seq // ATTN_TILE
    blocks_per_tile = ATTN_TILE // ATTN_BLOCK
    cur = lambda c: pl.BlockSpec((ATTN_TILE, c), lambda b, s: (b * nt + s, 0))
    prev_blk = lambda b, s: (jnp.maximum((b * nt + s) * blocks_per_tile - 1, 0), 0)
    prev_p = lambda b, s: (jnp.maximum((b * nt + s) * (ATTN_TILE // N_META) - 1, 0), 0)
    const2 = lambda b, s: (0, 0)
    rows = batch * seq
    return pl.pallas_call(
        _prompt_mix_kernel,
        grid=(batch, nt),
        in_specs=[pl.BlockSpec(memory_space=pltpu.SMEM),
                  cur(Q_DIM), cur(KV_DIM), cur(KV_DIM), cur(POOL_WIDTH),
                  pl.BlockSpec((ATTN_BLOCK, KV_DIM), prev_blk),
                  pl.BlockSpec((ATTN_BLOCK, KV_DIM), prev_blk),
                  pl.BlockSpec((N_META, POOL_WIDTH), prev_p),
                  pl.BlockSpec((ATTN_BLOCK, KV_DIM), const2),
                  pl.BlockSpec((ATTN_BLOCK, KV_DIM), const2),
                  pl.BlockSpec((N_META, POOL_WIDTH), const2),
                  pl.BlockSpec((2, N_KV_HEADS, GROUP * ATTN_BLOCK, 2 * ATTN_BLOCK), lambda b, s: (0, 0, 0, 0))],
        out_specs=[cur(POOL_WIDTH), cur(Q_DIM)],
        out_shape=[jax.ShapeDtypeStruct((rows, POOL_WIDTH), BF16),
                   jax.ShapeDtypeStruct((rows, Q_DIM), BF16)],
        scratch_shapes=[pltpu.VMEM((ATTN_BLOCK + ATTN_TILE, KV_DIM), BF16),
                        pltpu.VMEM((ATTN_BLOCK + ATTN_TILE, KV_DIM), BF16),
                        pltpu.VMEM((N_META + ATTN_TILE, POOL_WIDTH), F32),
                        pltpu.VMEM((blocks_per_tile * N_KV_HEADS, GROUP * ATTN_BLOCK, 2 * ATTN_BLOCK), F32),
                        pltpu.VMEM((blocks_per_tile * N_KV_HEADS, GROUP * ATTN_BLOCK, 2 * ATTN_BLOCK), BF16)],
        compiler_params=_cparams(("parallel", "parallel")),
        name="prompt_mix",
    )(sinks, q, k, v, p, k, v, p, mk, mv, mp, bias)


def _sample_mix_kernel(sink_ref, q_ref, k_ref, v_ref, p_ref, ck_ref, cv_ref, st_ref,
                       biasc_ref, biasn_ref,
                       pooled_ref, attn_ref, newk_ref, newv_ref, newp_ref, ext, pooled_buf):
    nseq, w_cache, s_new = SAMPLE_SEQS, WINDOW, q_ref.shape[0] // SAMPLE_SEQS
    keep = POOL_STATE - s_new
    for i in range(nseq):
        new = slice(i * s_new, (i + 1) * s_new)
        ext[1:1 + POOL_STATE, :] = st_ref[i]
        ext[1 + POOL_STATE:1 + POOL_STATE + s_new, :] = p_ref[new, :]
        _window_pool(ext, 1 + POOL_STATE, s_new, pooled_buf)
        pooled_ref[new, :] = pooled_buf[...].astype(pooled_ref.dtype)
        newp_ref[i, 0:keep, :] = st_ref[i, s_new:POOL_STATE, :]
        newp_ref[i, keep:POOL_STATE, :] = p_ref[new, :]
        newk_ref[i, 0:w_cache - s_new, :] = ck_ref[i, s_new:w_cache, :]
        newk_ref[i, w_cache - s_new:w_cache, :] = k_ref[new, :]
        newv_ref[i, 0:w_cache - s_new, :] = cv_ref[i, s_new:w_cache, :]
        newv_ref[i, w_cache - s_new:w_cache, :] = v_ref[new, :]

    kc = ck_ref[...].reshape(nseq * w_cache, KV_DIM).astype(BF16)
    vc = cv_ref[...].reshape(nseq * w_cache, KV_DIM).astype(BF16)
    kn = k_ref[...].astype(BF16)
    vn = v_ref[...].astype(BF16)
    q = q_ref[...]
    nq = q.shape[0]
    for kh in range(N_KV_HEADS):
        kvc = slice(kh * HEAD_DIM, (kh + 1) * HEAD_DIM)
        qg = jnp.concatenate([q[:, (kh * GROUP + g) * HEAD_DIM:(kh * GROUP + g + 1) * HEAD_DIM]
                              for g in range(GROUP)], axis=0)
        sc = lax.dot_general(qg, kc[:, kvc], _NT, preferred_element_type=F32) + biasc_ref[kh]
        sn = lax.dot_general(qg, kn[:, kvc], _NT, preferred_element_type=F32) + biasn_ref[kh]
        sink = jnp.concatenate([jnp.full((nq, 1), sink_ref[kh * GROUP + g], F32) for g in range(GROUP)], axis=0)
        m = jnp.maximum(jnp.maximum(jnp.max(sc, axis=-1, keepdims=True),
                                    jnp.max(sn, axis=-1, keepdims=True)), sink)
        ec = jnp.exp(sc - m)
        en = jnp.exp(sn - m)
        denom = (jnp.sum(ec, axis=-1, keepdims=True) + jnp.sum(en, axis=-1, keepdims=True)
                 + jnp.exp(sink - m))
        inv = 1.0 / denom
        o = (jnp.dot((ec * inv).astype(BF16), vc[:, kvc], preferred_element_type=F32)
             + jnp.dot((en * inv).astype(BF16), vn[:, kvc], preferred_element_type=F32))
        for g in range(GROUP):
            h = kh * GROUP + g
            attn_ref[:, h * HEAD_DIM:(h + 1) * HEAD_DIM] = o[g * nq:(g + 1) * nq].astype(attn_ref.dtype)


def _sample_mix(sinks, q, k, v, p, cache_k, cache_v, state, biasc, biasn, s_new):
    nb = cache_k.shape[0]
    rows = SAMPLE_SEQS * s_new
    tok = lambda c: pl.BlockSpec((rows, c), lambda i: (i, 0))
    seq3 = lambda r, c: pl.BlockSpec((SAMPLE_SEQS, r, c), lambda i: (i, 0, 0))
    const3 = lambda a: pl.BlockSpec(a.shape, lambda i: (0, 0, 0))
    return pl.pallas_call(
        _sample_mix_kernel,
        grid=(nb // SAMPLE_SEQS,),
        in_specs=[pl.BlockSpec(memory_space=pltpu.SMEM),
                  tok(Q_DIM), tok(KV_DIM), tok(KV_DIM), tok(POOL_WIDTH),
                  seq3(WINDOW, KV_DIM), seq3(WINDOW, KV_DIM), seq3(POOL_STATE, POOL_WIDTH),
                  const3(biasc), const3(biasn)],
        out_specs=[tok(POOL_WIDTH), tok(Q_DIM),
                   seq3(WINDOW, KV_DIM), seq3(WINDOW, KV_DIM), seq3(POOL_STATE, POOL_WIDTH)],
        out_shape=[jax.ShapeDtypeStruct((nb * s_new, POOL_WIDTH), F32),
                   jax.ShapeDtypeStruct((nb * s_new, Q_DIM), F32),
                   jax.ShapeDtypeStruct((nb, WINDOW, KV_DIM), F32),
                   jax.ShapeDtypeStruct((nb, WINDOW, KV_DIM), F32),
                   jax.ShapeDtypeStruct((nb, POOL_STATE, POOL_WIDTH), F32)],
        scratch_shapes=[pltpu.VMEM((1 + POOL_STATE + SUBLANES, POOL_WIDTH), F32),
                        pltpu.VMEM((s_new, POOL_WIDTH), F32)],
        compiler_params=_cparams(("parallel",)),
        name="sample_mix",
    )(sinks, q, k, v, p, cache_k, cache_v, state, biasc, biasn)


def _layer_norm(x, g, b):
    mu = jnp.mean(x, axis=-1, keepdims=True)
    xc = x - mu
    var = jnp.mean(xc * xc, axis=-1, keepdims=True)
    return xc * lax.rsqrt(var + LN_EPS) * g + b


def _merge_kernel(x_ref, pooled_ref, attn_ref, wg_ref, wmix_ref, scale_ref, wup_ref, wua_ref,
                  wout_ref, g_ref, b_ref, y_ref):
    x = x_ref[...]
    glog = jnp.dot(x.astype(BF16), wg_ref[...], preferred_element_type=F32)
    g_pool = jax.nn.sigmoid(glog[:, :D_MODEL])
    g_attn = jax.nn.sigmoid(glog[:, D_MODEL:])
    pooled = pooled_ref[...].astype(BF16)
    mixed = jnp.concatenate(
        [jnp.dot(pooled[:, g * POOL_GROUP_DIM:(g + 1) * POOL_GROUP_DIM], wmix_ref[g],
                 preferred_element_type=F32) for g in range(len(POOL_WINDOWS))], axis=1)
    pool_out = (mixed * scale_ref[...]).astype(BF16)
    a = jnp.dot(pool_out, wup_ref[...], preferred_element_type=F32)
    b = jnp.dot(attn_ref[...].astype(BF16), wua_ref[...], preferred_element_type=F32)
    m = (g_pool * a + g_attn * b).astype(BF16)
    r = jnp.dot(m, wout_ref[...], preferred_element_type=F32)
    y_ref[...] = _layer_norm(ALPHA * x + r, g_ref[...], b_ref[...])


def _merge(x, pooled, attn, wg, wmix, scale, wup, wua, wout, ln_g, ln_b):
    rows = x.shape[0]
    tm = min(TOK_TILE, rows)
    assert rows % tm == 0
    row = lambda c: pl.BlockSpec((tm, c), lambda i: (i, 0))
    full = lambda a: pl.BlockSpec(a.shape, lambda i: (0,) * a.ndim)
    weights = (wg, wmix, scale, wup, wua, wout, ln_g, ln_b)
    return pl.pallas_call(
        _merge_kernel,
        grid=(rows // tm,),
        in_specs=[row(D_MODEL), row(POOL_WIDTH), row(Q_DIM)] + [full(a) for a in weights],
        out_specs=row(D_MODEL),
        out_shape=jax.ShapeDtypeStruct((rows, D_MODEL), F32),
        compiler_params=_cparams(("parallel",)),
        name="merge",
    )(x, pooled, attn, *weights)


def _oddeven_merge_sort_pairs(n):
    pairs = []
    p = 1
    while p < n:
        k = p
        while k >= 1:
            for j in range(k % p, n - k, 2 * k):
                for i in range(min(k, n - j - k)):
                    if (i + j) // (2 * p) == (i + j + k) // (2 * p):
                        pairs.append((i + j, i + j + k))
            k //= 2
        p *= 2
    return pairs


def _bitonic_merge_pairs(n):
    pairs = []
    k = n // 2
    while k >= 1:
        pairs += [(i, i + k) for i in range(n) if not i & k]
        k //= 2
    return pairs


def _apply_network(vals, pairs):
    vals = list(vals)
    for i, j in pairs:
        a, b = vals[i], vals[j]
        if b is None:
            continue
        if a is None:
            vals[i], vals[j] = b, None
        else:
            vals[i], vals[j] = jnp.maximum(a, b), jnp.minimum(a, b)
    return vals


_SORT16 = _oddeven_merge_sort_pairs(PEER_TOPK)
_MERGE16 = _bitonic_merge_pairs(PEER_TOPK)


def _top16_rows(sc):
    n = sc.shape[0] // SUBLANES
    assert n == PEER_TOPK
    x = _apply_network([sc[k * SUBLANES:(k + 1) * SUBLANES, :] for k in range(n)], _SORT16)
    for shift in (4, 2, 1):
        y = [jnp.maximum(x[k], pltpu.roll(x[n - 1 - k], shift, 0)) for k in range(n)]
        x = _apply_network(y, _MERGE16)
    return x


def _best_sums(v1, v2):
    row = lax.broadcasted_iota(jnp.int32, v1[0].shape, 0)

    def one_per_sublane(vals):
        x = vals[SUBLANES - 1]
        for s in range(SUBLANES - 2, -1, -1):
            x = jnp.where(row == s, vals[s], x)
        return x

    lo, hi = one_per_sublane(v1[:SUBLANES]), one_per_sublane(v1[SUBLANES:])
    sums_lo = []
    for b in range(PEER_TOPK):
        n_valid = sum((a + 1) * (b + 1) <= PEER_TOPK for a in range(SUBLANES))
        s = lo + v2[b]
        sums_lo.append(s if n_valid == SUBLANES else jnp.where(row < n_valid, s, NEG))
    sums_hi = hi + v2[0]
    x, best = sums_hi, []
    for k in range(PEER_TOPK):
        best.append(jnp.maximum(sums_lo[k], x))
        x = jnp.minimum(sums_lo[k], x)
    for shift in (4, 2, 1):
        y = [jnp.maximum(best[k], pltpu.roll(best[PEER_TOPK - 1 - k], shift, 0)) for k in range(PEER_TOPK)]
        best = _apply_network(y, _MERGE16)
    return best, sums_lo, sums_hi


def _peer_scores_kernel(x_ref, wq_ref, keys_ref, cnt_ref, e1_ref, sel_ref):
    q = jnp.dot(x_ref[...].astype(BF16), wq_ref[...], preferred_element_type=F32).astype(BF16)
    for h in range(PEER_HEADS):
        sc = []
        for c in range(2):
            col = (h * 2 + c) * PEER_HALF
            sc.append(lax.dot_general(keys_ref[h * 2 + c], q[:, col:col + PEER_HALF], _NT,
                                      preferred_element_type=F32))
        for t in range(sc[0].shape[1] // LANES):
            lanes = slice(t * LANES, (t + 1) * LANES)
            s1, s2 = sc[0][:, lanes], sc[1][:, lanes]
            v1, v2 = _top16_rows(s1), _top16_rows(s2)
            best, sums_lo, _ = _best_sums(v1, v2)
            tau = best[PEER_TOPK - 1]
            z = jnp.ones_like(tau)
            for r in range(1, PEER_TOPK):
                z = z + jnp.exp(best[r] - best[0])
            inv_z = 1.0 / z
            cnt_lo = jnp.zeros_like(tau)
            for s in sums_lo:
                cnt_lo = cnt_lo + jnp.where(s >= tau, 1.0, 0.0)
            row = lambda r: r[0:1, :]
            half = float(SUBLANES)
            cnt_lo = jnp.where(cnt_lo > half, half + 1.0, cnt_lo)
            cnt = jnp.where((s1 <= row(v1[SUBLANES])) & (s1 + row(v2[0]) >= row(tau)), 1.0, 0.0)
            rank = jnp.where((s2 <= row(v2[SUBLANES])) & (row(v1[0]) + s2 >= row(tau)), half, half + 1.0)
            for a in range(SUBLANES - 1, -1, -1):
                cnt = jnp.where(s1 == row(v1[a]), cnt_lo[a:a + 1, :], cnt)
                rank = jnp.where(s2 >= row(v2[a]), float(a), rank)
            cnt_ref[h, t] = cnt
            e1_ref[h, t] = jnp.exp(s1 - row(v1[0])) * row(inv_z)
            rank_words = pltpu.bitcast(rank.astype(BF16), jnp.uint32)
            e2_words = pltpu.bitcast(jnp.exp(s2 - row(v2[0])).astype(BF16), jnp.uint32)
            for k in range(N_KEYS // (2 * SUBLANES)):
                words = slice(k * SUBLANES, (k + 1) * SUBLANES)
                sel_ref[h, t, 2 * k * SUBLANES:(2 * k + 1) * SUBLANES, :] = rank_words[words]
                sel_ref[h, t, (2 * k + 1) * SUBLANES:(2 * k + 2) * SUBLANES, :] = e2_words[words]


def _peer_scores(x, wq, keys):
    rows = x.shape[0]
    tm = min(TOK_TILE, rows)
    assert rows % tm == 0
    return pl.pallas_call(
        _peer_scores_kernel,
        grid=(rows // tm,),
        in_specs=[pl.BlockSpec((tm, D_MODEL), lambda i: (i, 0)),
                  pl.BlockSpec(wq.shape, lambda i: (0, 0)),
                  pl.BlockSpec(keys.shape, lambda i: (0, 0, 0))],
        out_specs=[pl.BlockSpec((PEER_HEADS, tm // LANES, N_KEYS, LANES), lambda i: (0, i, 0, 0))] * 3,
        out_shape=[jax.ShapeDtypeStruct((PEER_HEADS, rows // LANES, N_KEYS, LANES), dt)
                   for dt in (F32, F32, jnp.uint32)],
        compiler_params=_cparams(("parallel",)),
        name="peer_scores",
    )(x, wq, keys)


def _gelu(x):
    return 0.5 * x * (1.0 + lax.erf(x * math.sqrt(0.5)))


def _peer_dense_kernel(x_ref, cnt_ref, e1_ref, sel_ref, u_ref, v_ref, g_ref, b_ref, y_ref,
                       xt, rowb, hid, wt, acc):
    c = pl.program_id(1)
    rows_per_chunk = EXPERT_CHUNK // N_KEYS
    ncol = xt.shape[1] // LANES
    packed = 2 * SUBLANES
    assert rows_per_chunk == SUBLANES

    @pl.when(c == 0)
    def _():
        xt[...] = x_ref[...].T.astype(BF16)
        acc[...] = jnp.zeros_like(acc)

    def build_rows(chunk, heads):
        for h in heads:
            for col in range(ncol):
                for r in range(SUBLANES):
                    row = pl.ds(chunk * SUBLANES + r, packed, stride=0)
                    rowb[r, h, col] = cnt_ref[h, col, row, :].astype(BF16)
                    rowb[r, PEER_HEADS + h, col] = e1_ref[h, col, row, :].astype(BF16)

    def mask_row(il, carry):
        base = pl.multiple_of(il * N_KEYS, N_KEYS)
        for col in range(ncol):
            gate = [None] * (N_KEYS // packed)
            for h in range(PEER_HEADS):
                cnt = rowb[il, h, col]
                g1 = rowb[il, PEER_HEADS + h, col]
                for k in range(N_KEYS // packed):
                    rank = pltpu.bitcast(sel_ref[h, col, 2 * k * SUBLANES:(2 * k + 1) * SUBLANES, :], BF16)
                    g2 = pltpu.bitcast(sel_ref[h, col, (2 * k + 1) * SUBLANES:(2 * k + 2) * SUBLANES, :], BF16)
                    g = jnp.where(rank < cnt, g2 * g1, jnp.zeros((), BF16))
                    gate[k] = g if gate[k] is None else gate[k] + g
            for k in range(N_KEYS // packed):
                act = _gelu(hid[col, pl.ds(base + k * packed, packed), :]).astype(BF16)
                wt[pl.ds(base + k * packed, packed), col * LANES:(col + 1) * LANES] = act * gate[k]
        return carry

    n_slabs = 4
    slab = EXPERT_CHUNK // n_slabs

    for i in range(n_slabs):
        rows = slice(i * slab, (i + 1) * slab)
        res = jnp.dot(u_ref[rows, :].astype(BF16), xt[...], preferred_element_type=F32)
        for col in range(ncol):
            hid[col, rows, :] = res[:, col * LANES:(col + 1) * LANES]
        build_rows(c, range(i * PEER_HEADS // n_slabs, (i + 1) * PEER_HEADS // n_slabs))
    lax.fori_loop(0, rows_per_chunk, mask_row, 0)
    acc[...] += lax.dot_general(v_ref[...].astype(BF16), wt[...], (((0,), (0,)), ((), ())),
                                preferred_element_type=F32)

    @pl.when(c == pl.num_programs(1) - 1)
    def _():
        y_ref[...] = _layer_norm(ALPHA * x_ref[...] + acc[...].T, g_ref[...], b_ref[...])


def _peer_dense(x, cnt, e1, sel, u, vt, ln_g, ln_b):
    rows = x.shape[0]
    tm = min(TOK_TILE, rows)
    assert rows % tm == 0
    n_exp = u.shape[0]
    assert n_exp % EXPERT_CHUNK == 0
    return pl.pallas_call(
        _peer_dense_kernel,
        grid=(rows // tm, n_exp // EXPERT_CHUNK),
        in_specs=[pl.BlockSpec((tm, D_MODEL), lambda t, c: (t, 0))]
                 + [pl.BlockSpec((PEER_HEADS, tm // LANES, N_KEYS, LANES), lambda t, c: (0, t, 0, 0))] * 3
                 + [pl.BlockSpec((EXPERT_CHUNK, D_MODEL), lambda t, c: (c, 0)),
                    pl.BlockSpec((EXPERT_CHUNK, D_MODEL), lambda t, c: (c, 0)),
                    pl.BlockSpec((1, D_MODEL), lambda t, c: (0, 0)),
                    pl.BlockSpec((1, D_MODEL), lambda t, c: (0, 0))],
        out_specs=pl.BlockSpec((tm, D_MODEL), lambda t, c: (t, 0)),
        out_shape=jax.ShapeDtypeStruct((rows, D_MODEL), F32),
        scratch_shapes=[pltpu.VMEM((D_MODEL, tm), BF16),
                        pltpu.VMEM((SUBLANES, 2 * PEER_HEADS, tm // LANES, 2 * SUBLANES, LANES), BF16),
                        pltpu.VMEM((tm // LANES, EXPERT_CHUNK, LANES), F32),
                        pltpu.VMEM((EXPERT_CHUNK, tm), BF16),
                        pltpu.VMEM((D_MODEL, tm), F32)],
        compiler_params=_cparams(("parallel", "arbitrary")),
        name="peer_dense",
    )(x, cnt, e1, sel, u, vt, ln_g, ln_b)


def _rel_bucket_np(dist):
    n = np.maximum(dist, 0)
    max_exact = NUM_BUCKETS // 2
    large = max_exact + (np.log(np.maximum(n, max_exact).astype(np.float32) / max_exact)
                         / math.log(MAX_DISTANCE / max_exact) * (NUM_BUCKETS - max_exact)).astype(np.int32)
    return np.where(n < max_exact, n, np.minimum(large, NUM_BUCKETS - 1))


def _bias_kernel(table_ref, bucket_ref, out_ref, *, head_stride):
    head = pl.program_id(0) * head_stride + pl.program_id(1)
    bucket = bucket_ref[0]
    acc = jnp.full(bucket.shape, NEG, F32)
    for b in range(NUM_BUCKETS):
        acc = jnp.where(bucket == b, table_ref[b, head], acc)
    out_ref[0, 0] = acc


def _bias_tiles(table, bucket, n_outer, n_inner, head_stride):
    r, c = bucket.shape[1:]
    bmap = (lambda a, b: (a, 0, 0)) if bucket.shape[0] > 1 else (lambda a, b: (0, 0, 0))
    return pl.pallas_call(
        functools.partial(_bias_kernel, head_stride=head_stride),
        grid=(n_outer, n_inner),
        in_specs=[pl.BlockSpec(memory_space=pltpu.SMEM), pl.BlockSpec((1, r, c), bmap)],
        out_specs=pl.BlockSpec((1, 1, r, c), lambda a, b: (a, b, 0, 0)),
        out_shape=jax.ShapeDtypeStruct((n_outer, n_inner, r, c), F32),
        compiler_params=_cparams(("parallel", "parallel")),
        name="bias_tiles",
    )(table, jnp.asarray(bucket, jnp.int32))


def _prompt_bias(table):
    qi = np.arange(ATTN_BLOCK)[:, None]
    kj = np.arange(2 * ATTN_BLOCK)[None, :]
    dist = ATTN_BLOCK + qi - kj
    ok = (dist >= 0) & (dist < WINDOW)
    ok = np.stack([ok, ok & (kj >= ATTN_BLOCK - N_META)])
    bucket = np.where(ok, _rel_bucket_np(dist)[None], -1)
    tiles = _bias_tiles(table, bucket, 2, N_HEADS, 0)
    return tiles.reshape(2, N_KV_HEADS, GROUP * ATTN_BLOCK, 2 * ATTN_BLOCK)


def _sample_bias(table, s_new):
    s, t = np.meshgrid(np.arange(SAMPLE_SEQS), np.arange(s_new), indexing="ij")
    s, t = s.reshape(-1, 1), t.reshape(-1, 1)
    out = []
    for cols, offset in ((WINDOW, WINDOW), (s_new, 0)):
        s2, c = np.meshgrid(np.arange(SAMPLE_SEQS), np.arange(cols), indexing="ij")
        s2, c = s2.reshape(1, -1), c.reshape(1, -1)
        dist = offset + t - c + 0 * s2
        ok = (s == s2) & (dist >= 0) & (dist < WINDOW)
        bucket = np.where(ok, _rel_bucket_np(dist), -1)[None]
        tiles = _bias_tiles(table, bucket, N_KV_HEADS, GROUP, GROUP)
        out.append(tiles.reshape(N_KV_HEADS, GROUP * bucket.shape[1], bucket.shape[2]))
    return out


def kernel(x_prompt, x_sample, cache_k, cache_v, state_pool, meta_tokens, rel_bias_table, w_in,
           w_pool_mix, pool_scale, attn_sinks, w_up_pool, w_up_attn, w_out, ln1_g, ln1_b,
           peer_w_query, peer_sub_keys, peer_u, peer_v, ln2_g, ln2_b):
    batch, seq, d = x_prompt.shape
    nb, s_new, _ = x_sample.shape
    assert w_in.shape[0] == DEPTH and d == D_MODEL and seq % ATTN_TILE == 0
    assert N_META >= max(POOL_WINDOWS) - 1 and cache_k.shape[2] == WINDOW

    w_in0 = w_in[0].astype(BF16)
    w_pqkv, w_gates = w_in0[:, :OFF_GA], w_in0[:, OFF_GA:]
    wmix = w_pool_mix[0].astype(BF16)
    scale = pool_scale[0].reshape(1, POOL_WIDTH)
    wup, wua, wout = w_up_pool[0].astype(BF16), w_up_attn[0].astype(BF16), w_out[0].astype(BF16)
    g1, b1 = ln1_g[0].reshape(1, d), ln1_b[0].reshape(1, d)
    g2, b2 = ln2_g[0].reshape(1, d), ln2_b[0].reshape(1, d)
    wq = peer_w_query[0].astype(BF16)
    keys = peer_sub_keys[0].reshape(2 * PEER_HEADS, N_KEYS, PEER_HALF).astype(BF16)
    u = peer_u[0]
    vt = peer_v[0]
    sinks = attn_sinks[0].astype(F32)
    table = rel_bias_table.astype(F32)
    bias_p = _prompt_bias(table)
    bias_c, bias_n = _sample_bias(table, s_new)

    xp = x_prompt.reshape(batch * seq, d)
    xs = x_sample.reshape(nb * s_new, d)
    xm = jnp.concatenate([jnp.zeros((ATTN_BLOCK - N_META, d), F32), meta_tokens.astype(F32)], axis=0)

    p_p, q_p, k_p, v_p = _inproj(xp, w_pqkv)
    p_s, q_s, k_s, v_s = _inproj(xs, w_pqkv)
    p_m, _, k_m, v_m = _inproj(xm, w_pqkv)

    pooled_p, attn_p = _prompt_mix(sinks, q_p, k_p, v_p, p_p, k_m, v_m, p_m[ATTN_BLOCK - N_META:],
                                   bias_p, batch, seq)
    ck = cache_k[0].reshape(nb, WINDOW, KV_DIM)
    cv = cache_v[0].reshape(nb, WINDOW, KV_DIM)
    pooled_s, attn_s, newk, newv, newp = _sample_mix(sinks, q_s, k_s, v_s, p_s, ck, cv,
                                                     state_pool[0], bias_c, bias_n, s_new)

    outs = []
    for x, pooled, attn in ((xp, pooled_p, attn_p), (xs, pooled_s, attn_s)):
        x1 = _merge(x, pooled, attn, w_gates, wmix, scale, wup, wua, wout, g1, b1)
        cnt, e1, sel = _peer_scores(x1, wq, keys)
        outs.append(_peer_dense(x1, cnt, e1, sel, u, vt, g2, b2))

    w_keep = min(WINDOW, seq + N_META)
    kv_shape = (batch, seq, N_KV_HEADS, HEAD_DIM)
    return (outs[0].reshape(batch, seq, d),
            outs[1].reshape(nb, s_new, d),
            k_p.reshape(kv_shape)[None, :, seq - w_keep:],
            v_p.reshape(kv_shape)[None, :, seq - w_keep:],
            p_p.reshape(batch, seq, POOL_WIDTH)[None, :, seq - POOL_STATE:],
            newk.reshape(1, nb, WINDOW, N_KV_HEADS, HEAD_DIM),
            newv.reshape(1, nb, WINDOW, N_KV_HEADS, HEAD_DIM),
            newp[None])
```

```python
import functools
import math

import jax
import jax.numpy as jnp
import numpy as np
from jax import lax
from jax.experimental import pallas as pl
from jax.experimental.pallas import tpu as pltpu

F32 = jnp.float32
BF16 = jnp.bfloat16

D_MODEL = 1024
N_META = 16
POOL_WIDTH = 512
POOL_WINDOWS = (2, 4, 8, 16)
POOL_GROUP_DIM = 128
POOL_STATE = 15
HEAD_DIM = 64
N_HEADS = 8
N_KV_HEADS = 2
GROUP = N_HEADS // N_KV_HEADS
WINDOW = 128
ATTN_BLOCK = 128
ATTN_SCALE = HEAD_DIM ** -0.5
NUM_BUCKETS = 32
MAX_DISTANCE = 128
Q_DIM = N_HEADS * HEAD_DIM
KV_DIM = N_KV_HEADS * HEAD_DIM
OFF_Q = POOL_WIDTH
OFF_K = OFF_Q + Q_DIM
OFF_V = OFF_K + KV_DIM
OFF_GA = OFF_V + KV_DIM
PEER_HEADS = 8
N_KEYS = 128
PEER_TOPK = 16
PEER_HALF = 128
DEPTH = 1
ALPHA = (2 * DEPTH) ** 0.25
LN_EPS = 1e-5
NEG = -1e30

LANES = 128
SUBLANES = 8
VMEM_LIMIT = 56 * 1024 * 1024

TOK_TILE = 512
ATTN_TILE = 512
SAMPLE_SEQS = 8
EXPERT_CHUNK = 1024


def _cparams(sem):
    return pltpu.CompilerParams(dimension_semantics=sem, vmem_limit_bytes=VMEM_LIMIT)


def _inproj_kernel(x_ref, w_ref, p_ref, q_ref, k_ref, v_ref):
    z = jnp.dot(x_ref[...].astype(BF16), w_ref[...], preferred_element_type=F32)
    p_ref[...] = z[:, :OFF_Q]
    q_ref[...] = (z[:, OFF_Q:OFF_K] * ATTN_SCALE).astype(BF16)
    k_ref[...] = z[:, OFF_K:OFF_V]
    v_ref[...] = z[:, OFF_V:OFF_GA]


def _inproj(x, w_pqkv):
    rows = x.shape[0]
    tm = min(TOK_TILE, rows)
    assert rows % tm == 0
    row = lambda c: pl.BlockSpec((tm, c), lambda i: (i, 0))
    return pl.pallas_call(
        _inproj_kernel,
        grid=(rows // tm,),
        in_specs=[row(D_MODEL), pl.BlockSpec((D_MODEL, OFF_GA), lambda i: (0, 0))],
        out_specs=[row(POOL_WIDTH), row(Q_DIM), row(KV_DIM), row(KV_DIM)],
        out_shape=[jax.ShapeDtypeStruct((rows, POOL_WIDTH), F32),
                   jax.ShapeDtypeStruct((rows, Q_DIM), BF16),
                   jax.ShapeDtypeStruct((rows, KV_DIM), F32),
                   jax.ShapeDtypeStruct((rows, KV_DIM), F32)],
        compiler_params=_cparams(("parallel",)),
        name="inproj",
    )(x, w_pqkv)


def _window_pool(ext_ref, first, rows, out_ref):
    for g, w in enumerate(POOL_WINDOWS):
        cols = slice(g * POOL_GROUP_DIM, (g + 1) * POOL_GROUP_DIM)
        cur = ext_ref[first:first + rows, cols]
        acc = cur
        for r in range(1, w):
            acc = acc + ext_ref[first - r:first - r + rows, cols]
        out_ref[:, cols] = (acc * (1.0 / w) - cur).astype(out_ref.dtype)


def _window_pool_tall(ext_ref, first, rows, out_ref):
    for g, w in enumerate(POOL_WINDOWS):
        cols = slice(g * POOL_GROUP_DIM, (g + 1) * POOL_GROUP_DIM)
        x = ext_ref[:, cols]
        s, span = x, 1
        while span < w:
            s = s + pltpu.roll(s, span, 0)
            span *= 2
        out_ref[:, cols] = (s[first:first + rows] * (1.0 / w) - x[first:first + rows]).astype(out_ref.dtype)


def _sink_softmax(s, sink):
    m = jnp.maximum(jnp.max(s, axis=-1, keepdims=True), sink)
    e = jnp.exp(s - m)
    denom = jnp.sum(e, axis=-1, keepdims=True) + jnp.exp(sink - m)
    return e * (1.0 / denom)


_NT = (((1,), (1,)), ((), ()))


def _prompt_mix_kernel(sink_ref, q_ref, k_ref, v_ref, p_ref, kprev_ref, vprev_ref, pprev_ref,
                       mk_ref, mv_ref, mp_ref, bias_ref, pooled_ref, attn_ref,
                       kbuf, vbuf, pbuf, sbuf, prob_buf):
    first = pl.program_id(1) == 0
    hist = ATTN_BLOCK
    kbuf[0:hist, :] = jnp.where(first, mk_ref[...], kprev_ref[...]).astype(BF16)
    vbuf[0:hist, :] = jnp.where(first, mv_ref[...], vprev_ref[...]).astype(BF16)
    kbuf[hist:, :] = k_ref[...].astype(BF16)
    vbuf[hist:, :] = v_ref[...].astype(BF16)
    pbuf[0:N_META, :] = jnp.where(first, mp_ref[...], pprev_ref[...])
    pbuf[N_META:, :] = p_ref[...]

    _window_pool_tall(pbuf, N_META, ATTN_TILE, pooled_ref)

    first_i = jnp.where(first, 1, 0)
    blocks = [(j, kh) for j in range(ATTN_TILE // ATTN_BLOCK) for kh in range(N_KV_HEADS)]
    rows = lambda j: slice(j * ATTN_BLOCK, (j + 1) * ATTN_BLOCK)
    keys = lambda j: slice(j * ATTN_BLOCK, j * ATTN_BLOCK + 2 * ATTN_BLOCK)
    head = lambda h: slice(h * HEAD_DIM, (h + 1) * HEAD_DIM)
    for i, (j, kh) in enumerate(blocks):
        qg = jnp.concatenate([q_ref[rows(j), head(kh * GROUP + g)] for g in range(GROUP)], axis=0)
        s = lax.dot_general(qg, kbuf[keys(j), head(kh)], _NT, preferred_element_type=F32)
        sbuf[i] = s + (bias_ref[first_i, kh] if j == 0 else bias_ref[0, kh])
    for i, (j, kh) in enumerate(blocks):
        sink = jnp.concatenate([jnp.full((ATTN_BLOCK, 1), sink_ref[kh * GROUP + g], F32) for g in range(GROUP)], axis=0)
        prob_buf[i] = _sink_softmax(sbuf[i], sink).astype(BF16)
    for i, (j, kh) in enumerate(blocks):
        o = jnp.dot(prob_buf[i], vbuf[keys(j), head(kh)], preferred_element_type=F32)
        for g in range(GROUP):
            attn_ref[rows(j), head(kh * GROUP + g)] = o[g * ATTN_BLOCK:(g + 1) * ATTN_BLOCK].astype(attn_ref.dtype)


def _prompt_mix(sinks, q, k, v, p, mk, mv, mp, bias, batch, seq):
    nt = seq // ATTN_TILE
    blocks_per_tile = ATTN_TILE // ATTN_BLOCK
    cur = lambda c: pl.BlockSpec((ATTN_TILE, c), lambda b, s: (b * nt + s, 0))
    prev_blk = lambda b, s: (jnp.maximum((b * nt + s) * blocks_per_tile - 1, 0), 0)
    prev_p = lambda b, s: (jnp.maximum((b * nt + s) * (ATTN_TILE // N_META) - 1, 0), 0)
    const2 = lambda b, s: (0, 0)
    rows = batch * seq
    return pl.pallas_call(
        _prompt_mix_kernel,
        grid=(batch, nt),
        in_specs=[pl.BlockSpec(memory_space=pltpu.SMEM),
                  cur(Q_DIM), cur(KV_DIM), cur(KV_DIM), cur(POOL_WIDTH),
                  pl.BlockSpec((ATTN_BLOCK, KV_DIM), prev_blk),
                  pl.BlockSpec((ATTN_BLOCK, KV_DIM), prev_blk),
                  pl.BlockSpec((N_META, POOL_WIDTH), prev_p),
                  pl.BlockSpec((ATTN_BLOCK, KV_DIM), const2),
                  pl.BlockSpec((ATTN_BLOCK, KV_DIM), const2),
                  pl.BlockSpec((N_META, POOL_WIDTH), const2),
                  pl.BlockSpec((2, N_KV_HEADS, GROUP * ATTN_BLOCK, 2 * ATTN_BLOCK), lambda b, s: (0, 0, 0, 0))],
        out_specs=[cur(POOL_WIDTH), cur(Q_DIM)],
        out_shape=[jax.ShapeDtypeStruct((rows, POOL_WIDTH), BF16),
                   jax.ShapeDtypeStruct((rows, Q_DIM), BF16)],
        scratch_shapes=[pltpu.VMEM((ATTN_BLOCK + ATTN_TILE, KV_DIM), BF16),
                        pltpu.VMEM((ATTN_BLOCK + ATTN_TILE, KV_DIM), BF16),
                        pltpu.VMEM((N_META + ATTN_TILE, POOL_WIDTH), F32),
                        pltpu.VMEM((blocks_per_tile * N_KV_HEADS, GROUP * ATTN_BLOCK, 2 * ATTN_BLOCK), F32),
                        pltpu.VMEM((blocks_per_tile * N_KV_HEADS, GROUP * ATTN_BLOCK, 2 * ATTN_BLOCK), BF16)],
        compiler_params=_cparams(("parallel", "parallel")),
        name="prompt_mix",
    )(sinks, q, k, v, p, k, v, p, mk, mv, mp, bias)


def _sample_mix_kernel(sink_ref, q_ref, k_ref, v_ref, p_ref, ck_ref, cv_ref, st_ref,
                       biasc_ref, biasn_ref,
                       pooled_ref, attn_ref, newk_ref, newv_ref, newp_ref, ext, pooled_buf):
    nseq, w_cache, s_new = SAMPLE_SEQS, WINDOW, q_ref.shape[0] // SAMPLE_SEQS
    keep = POOL_STATE - s_new
    for i in range(nseq):
        new = slice(i * s_new, (i + 1) * s_new)
        ext[1:1 + POOL_STATE, :] = st_ref[i]
        ext[1 + POOL_STATE:1 + POOL_STATE + s_new, :] = p_ref[new, :]
        _window_pool(ext, 1 + POOL_STATE, s_new, pooled_buf)
        pooled_ref[new, :] = pooled_buf[...].astype(pooled_ref.dtype)
        newp_ref[i, 0:keep, :] = st_ref[i, s_new:POOL_STATE, :]
        newp_ref[i, keep:POOL_STATE, :] = p_ref[new, :]
        newk_ref[i, 0:w_cache - s_new, :] = ck_ref[i, s_new:w_cache, :]
        newk_ref[i, w_cache - s_new:w_cache, :] = k_ref[new, :]
        newv_ref[i, 0:w_cache - s_new, :] = cv_ref[i, s_new:w_cache, :]
        newv_ref[i, w_cache - s_new:w_cache, :] = v_ref[new, :]

    kc = ck_ref[...].reshape(nseq * w_cache, KV_DIM).astype(BF16)
    vc = cv_ref[...].reshape(nseq * w_cache, KV_DIM).astype(BF16)
    kn = k_ref[...].astype(BF16)
    vn = v_ref[...].astype(BF16)
    q = q_ref[...]
    nq = q.shape[0]
    for kh in range(N_KV_HEADS):
        kvc = slice(kh * HEAD_DIM, (kh + 1) * HEAD_DIM)
        qg = jnp.concatenate([q[:, (kh * GROUP + g) * HEAD_DIM:(kh * GROUP + g + 1) * HEAD_DIM]
                              for g in range(GROUP)], axis=0)
        sc = lax.dot_general(qg, kc[:, kvc], _NT, preferred_element_type=F32) + biasc_ref[kh]
        sn = lax.dot_general(qg, kn[:, kvc], _NT, preferred_element_type=F32) + biasn_ref[kh]
        sink = jnp.concatenate([jnp.full((nq, 1), sink_ref[kh * GROUP + g], F32) for g in range(GROUP)], axis=0)
        m = jnp.maximum(jnp.maximum(jnp.max(sc, axis=-1, keepdims=True),
                                    jnp.max(sn, axis=-1, keepdims=True)), sink)
        ec = jnp.exp(sc - m)
        en = jnp.exp(sn - m)
        denom = (jnp.sum(ec, axis=-1, keepdims=True) + jnp.sum(en, axis=-1, keepdims=True)
                 + jnp.exp(sink - m))
        inv = 1.0 / denom
        o = (jnp.dot((ec * inv).astype(BF16), vc[:, kvc], preferred_element_type=F32)
             + jnp.dot((en * inv).astype(BF16), vn[:, kvc], preferred_element_type=F32))
        for g in range(GROUP):
            h = kh * GROUP + g
            attn_ref[:, h * HEAD_DIM:(h + 1) * HEAD_DIM] = o[g * nq:(g + 1) * nq].astype(attn_ref.dtype)


def _sample_mix(sinks, q, k, v, p, cache_k, cache_v, state, biasc, biasn, s_new):
    nb = cache_k.shape[0]
    rows = SAMPLE_SEQS * s_new
    tok = lambda c: pl.BlockSpec((rows, c), lambda i: (i, 0))
    seq3 = lambda r, c: pl.BlockSpec((SAMPLE_SEQS, r, c), lambda i: (i, 0, 0))
    const3 = lambda a: pl.BlockSpec(a.shape, lambda i: (0, 0, 0))
    return pl.pallas_call(
        _sample_mix_kernel,
        grid=(nb // SAMPLE_SEQS,),
        in_specs=[pl.BlockSpec(memory_space=pltpu.SMEM),
                  tok(Q_DIM), tok(KV_DIM), tok(KV_DIM), tok(POOL_WIDTH),
                  seq3(WINDOW, KV_DIM), seq3(WINDOW, KV_DIM), seq3(POOL_STATE, POOL_WIDTH),
                  const3(biasc), const3(biasn)],
        out_specs=[tok(POOL_WIDTH), tok(Q_DIM),
                   seq3(WINDOW, KV_DIM), seq3(WINDOW, KV_DIM), seq3(POOL_STATE, POOL_WIDTH)],
        out_shape=[jax.ShapeDtypeStruct((nb * s_new, POOL_WIDTH), F32),
                   jax.ShapeDtypeStruct((nb * s_new, Q_DIM), F32),
                   jax.ShapeDtypeStruct((nb, WINDOW, KV_DIM), F32),
                   jax.ShapeDtypeStruct((nb, WINDOW, KV_DIM), F32),
                   jax.ShapeDtypeStruct((nb, POOL_STATE, POOL_WIDTH), F32)],
        scratch_shapes=[pltpu.VMEM((1 + POOL_STATE + SUBLANES, POOL_WIDTH), F32),
                        pltpu.VMEM((s_new, POOL_WIDTH), F32)],
        compiler_params=_cparams(("parallel",)),
        name="sample_mix",
    )(sinks, q, k, v, p, cache_k, cache_v, state, biasc, biasn)


def _layer_norm(x, g, b):
    mu = jnp.mean(x, axis=-1, keepdims=True)
    xc = x - mu
    var = jnp.mean(xc * xc, axis=-1, keepdims=True)
    return xc * lax.rsqrt(var + LN_EPS) * g + b


def _merge_kernel(x_ref, pooled_ref, attn_ref, wg_ref, wmix_ref, scale_ref, wup_ref, wua_ref,
                  wout_ref, g_ref, b_ref, y_ref):
    x = x_ref[...]
    glog = jnp.dot(x.astype(BF16), wg_ref[...], preferred_element_type=F32)
    g_pool = jax.nn.sigmoid(glog[:, :D_MODEL])
    g_attn = jax.nn.sigmoid(glog[:, D_MODEL:])
    pooled = pooled_ref[...].astype(BF16)
    mixed = jnp.concatenate(
        [jnp.dot(pooled[:, g * POOL_GROUP_DIM:(g + 1) * POOL_GROUP_DIM], wmix_ref[g],
                 preferred_element_type=F32) for g in range(len(POOL_WINDOWS))], axis=1)
    pool_out = (mixed * scale_ref[...]).astype(BF16)
    a = jnp.dot(pool_out, wup_ref[...], preferred_element_type=F32)
    b = jnp.dot(attn_ref[...].astype(BF16), wua_ref[...], preferred_element_type=F32)
    m = (g_pool * a + g_attn * b).astype(BF16)
    r = jnp.dot(m, wout_ref[...], preferred_element_type=F32)
    y_ref[...] = _layer_norm(ALPHA * x + r, g_ref[...], b_ref[...])


def _merge(x, pooled, attn, wg, wmix, scale, wup, wua, wout, ln_g, ln_b):
    rows = x.shape[0]
    tm = min(TOK_TILE, rows)
    assert rows % tm == 0
    row = lambda c: pl.BlockSpec((tm, c), lambda i: (i, 0))
    full = lambda a: pl.BlockSpec(a.shape, lambda i: (0,) * a.ndim)
    weights = (wg, wmix, scale, wup, wua, wout, ln_g, ln_b)
    return pl.pallas_call(
        _merge_kernel,
        grid=(rows // tm,),
        in_specs=[row(D_MODEL), row(POOL_WIDTH), row(Q_DIM)] + [full(a) for a in weights],
        out_specs=row(D_MODEL),
        out_shape=jax.ShapeDtypeStruct((rows, D_MODEL), F32),
        compiler_params=_cparams(("parallel",)),
        name="merge",
    )(x, pooled, attn, *weights)


def _oddeven_merge_sort_pairs(n):
    pairs = []
    p = 1
    while p < n:
        k = p
        while k >= 1:
            for j in range(k % p, n - k, 2 * k):
                for i in range(min(k, n - j - k)):
                    if (i + j) // (2 * p) == (i + j + k) // (2 * p):
                        pairs.append((i + j, i + j + k))
            k //= 2
        p *= 2
    return pairs


def _bitonic_merge_pairs(n):
    pairs = []
    k = n // 2
    while k >= 1:
        pairs += [(i, i + k) for i in range(n) if not i & k]
        k //= 2
    return pairs


def _apply_network(vals, pairs):
    vals = list(vals)
    for i, j in pairs:
        a, b = vals[i], vals[j]
        if b is None:
            continue
        if a is None:
            vals[i], vals[j] = b, None
        else:
            vals[i], vals[j] = jnp.maximum(a, b), jnp.minimum(a, b)
    return vals


_SORT16 = _oddeven_merge_sort_pairs(PEER_TOPK)
_MERGE16 = _bitonic_merge_pairs(PEER_TOPK)


def _top16_rows(sc):
    n = sc.shape[0] // SUBLANES
    assert n == PEER_TOPK
    x = _apply_network([sc[k * SUBLANES:(k + 1) * SUBLANES, :] for k in range(n)], _SORT16)
    for shift in (4, 2, 1):
        y = [jnp.maximum(x[k], pltpu.roll(x[n - 1 - k], shift, 0)) for k in range(n)]
        x = _apply_network(y, _MERGE16)
    return x


def _best_sums(v1, v2):
    row = lax.broadcasted_iota(jnp.int32, v1[0].shape, 0)

    def one_per_sublane(vals):
        x = vals[SUBLANES - 1]
        for s in range(SUBLANES - 2, -1, -1):
            x = jnp.where(row == s, vals[s], x)
        return x

    lo, hi = one_per_sublane(v1[:SUBLANES]), one_per_sublane(v1[SUBLANES:])
    sums_lo = []
    for b in range(PEER_TOPK):
        n_valid = sum((a + 1) * (b + 1) <= PEER_TOPK for a in range(SUBLANES))
        s = lo + v2[b]
        sums_lo.append(s if n_valid == SUBLANES else jnp.where(row < n_valid, s, NEG))
    sums_hi = hi + v2[0]
    x, best = sums_hi, []
    for k in range(PEER_TOPK):
        best.append(jnp.maximum(sums_lo[k], x))
        x = jnp.minimum(sums_lo[k], x)
    for shift in (4, 2, 1):
        y = [jnp.maximum(best[k], pltpu.roll(best[PEER_TOPK - 1 - k], shift, 0)) for k in range(PEER_TOPK)]
        best = _apply_network(y, _MERGE16)
    return best, sums_lo, sums_hi


def _peer_scores_kernel(x_ref, wq_ref, keys_ref, cnt_ref, e1_ref, sel_ref):
    q = jnp.dot(x_ref[...].astype(BF16), wq_ref[...], preferred_element_type=F32).astype(BF16)
    for h in range(PEER_HEADS):
        sc = []
        for c in range(2):
            col = (h * 2 + c) * PEER_HALF
            sc.append(lax.dot_general(keys_ref[h * 2 + c], q[:, col:col + PEER_HALF], _NT,
                                      preferred_element_type=F32))
        for t in range(sc[0].shape[1] // LANES):
            lanes = slice(t * LANES, (t + 1) * LANES)
            s1, s2 = sc[0][:, lanes], sc[1][:, lanes]
            v1, v2 = _top16_rows(s1), _top16_rows(s2)
            best, sums_lo, _ = _best_sums(v1, v2)
            tau = best[PEER_TOPK - 1]
            z = jnp.ones_like(tau)
            for r in range(1, PEER_TOPK):
                z = z + jnp.exp(best[r] - best[0])
            inv_z = (0.5 / GELU_ARG) / z
            cnt_lo = jnp.zeros_like(tau)
            for s in sums_lo:
                cnt_lo = cnt_lo + jnp.where(s >= tau, 1.0, 0.0)
            row = lambda r: r[0:1, :]
            half = float(SUBLANES)
            cnt_lo = jnp.where(cnt_lo > half, half + 1.0, cnt_lo)
            cnt = jnp.where((s1 <= row(v1[SUBLANES])) & (s1 + row(v2[0]) >= row(tau)), 1.0, 0.0)
            rank = jnp.where((s2 <= row(v2[SUBLANES])) & (row(v1[0]) + s2 >= row(tau)), half, half + 1.0)
            for a in range(SUBLANES - 1, -1, -1):
                cnt = jnp.where(s1 == row(v1[a]), cnt_lo[a:a + 1, :], cnt)
                rank = jnp.where(s2 >= row(v2[a]), float(a), rank)
            cnt_ref[h, t] = cnt
            e1_ref[h, t] = jnp.exp(s1 - row(v1[0])) * row(inv_z)
            rank_words = pltpu.bitcast(rank.astype(BF16), jnp.uint32)
            e2_words = pltpu.bitcast(jnp.exp(s2 - row(v2[0])).astype(BF16), jnp.uint32)
            for k in range(N_KEYS // (2 * SUBLANES)):
                words = slice(k * SUBLANES, (k + 1) * SUBLANES)
                sel_ref[h, t, 2 * k * SUBLANES:(2 * k + 1) * SUBLANES, :] = rank_words[words]
                sel_ref[h, t, (2 * k + 1) * SUBLANES:(2 * k + 2) * SUBLANES, :] = e2_words[words]


def _peer_scores(x, wq, keys):
    rows = x.shape[0]
    tm = min(TOK_TILE, rows)
    assert rows % tm == 0
    return pl.pallas_call(
        _peer_scores_kernel,
        grid=(rows // tm,),
        in_specs=[pl.BlockSpec((tm, D_MODEL), lambda i: (i, 0)),
                  pl.BlockSpec(wq.shape, lambda i: (0, 0)),
                  pl.BlockSpec(keys.shape, lambda i: (0, 0, 0))],
        out_specs=[pl.BlockSpec((PEER_HEADS, tm // LANES, N_KEYS, LANES), lambda i: (0, i, 0, 0))] * 3,
        out_shape=[jax.ShapeDtypeStruct((PEER_HEADS, rows // LANES, N_KEYS, LANES), dt)
                   for dt in (F32, F32, jnp.uint32)],
        compiler_params=_cparams(("parallel",)),
        name="peer_scores",
    )(x, wq, keys)


GELU_ARG = math.sqrt(0.5)


def _gelu_scaled(s):
    return s * (1.0 + lax.erf(s))


def _peer_dense_kernel(x_ref, cnt_ref, e1_ref, sel_ref, u_ref, v_ref, g_ref, b_ref, y_ref,
                       xt, rowb, hid, wt, acc):
    c = pl.program_id(1)
    rows_per_chunk = EXPERT_CHUNK // N_KEYS
    ncol = xt.shape[1] // LANES
    packed = 2 * SUBLANES
    assert rows_per_chunk == SUBLANES

    @pl.when(c == 0)
    def _():
        xt[...] = x_ref[...].T.astype(BF16)
        acc[...] = jnp.zeros_like(acc)

    def build_rows(chunk, heads):
        for h in heads:
            for col in range(ncol):
                for r in range(SUBLANES):
                    row = pl.ds(chunk * SUBLANES + r, packed, stride=0)
                    rowb[r, h, col] = cnt_ref[h, col, row, :].astype(BF16)
                    rowb[r, PEER_HEADS + h, col] = e1_ref[h, col, row, :].astype(BF16)

    def mask_row(il, carry):
        base = pl.multiple_of(il * N_KEYS, N_KEYS)
        for col in range(ncol):
            gate = [None] * (N_KEYS // packed)
            for h in range(PEER_HEADS):
                cnt = rowb[il, h, col]
                g1 = rowb[il, PEER_HEADS + h, col]
                for k in range(N_KEYS // packed):
                    rank = pltpu.bitcast(sel_ref[h, col, 2 * k * SUBLANES:(2 * k + 1) * SUBLANES, :], BF16)
                    g2 = pltpu.bitcast(sel_ref[h, col, (2 * k + 1) * SUBLANES:(2 * k + 2) * SUBLANES, :], BF16)
                    g = jnp.where(rank < cnt, g2 * g1, jnp.zeros((), BF16))
                    gate[k] = g if gate[k] is None else gate[k] + g
            for k in range(N_KEYS // packed):
                act = _gelu_scaled(hid[col, pl.ds(base + k * packed, packed), :]).astype(BF16)
                wt[pl.ds(base + k * packed, packed), col * LANES:(col + 1) * LANES] = act * gate[k]
        return carry

    n_slabs = 4
    slab = EXPERT_CHUNK // n_slabs

    for i in range(n_slabs):
        rows = slice(i * slab, (i + 1) * slab)
        res = jnp.dot(u_ref[rows, :].astype(BF16), xt[...], preferred_element_type=F32)
        for col in range(ncol):
            hid[col, rows, :] = res[:, col * LANES:(col + 1) * LANES] * GELU_ARG
        build_rows(c, range(i * PEER_HEADS // n_slabs, (i + 1) * PEER_HEADS // n_slabs))
    lax.fori_loop(0, rows_per_chunk, mask_row, 0)
    acc[...] += lax.dot_general(v_ref[...].astype(BF16), wt[...], (((0,), (0,)), ((), ())),
                                preferred_element_type=F32)

    @pl.when(c == pl.num_programs(1) - 1)
    def _():
        y_ref[...] = _layer_norm(ALPHA * x_ref[...] + acc[...].T, g_ref[...], b_ref[...])


def _peer_dense(x, cnt, e1, sel, u, vt, ln_g, ln_b):
    rows = x.shape[0]
    tm = min(TOK_TILE, rows)
    assert rows % tm == 0
    n_exp = u.shape[0]
    assert n_exp % EXPERT_CHUNK == 0
    return pl.pallas_call(
        _peer_dense_kernel,
        grid=(rows // tm, n_exp // EXPERT_CHUNK),
        in_specs=[pl.BlockSpec((tm, D_MODEL), lambda t, c: (t, 0))]
                 + [pl.BlockSpec((PEER_HEADS, tm // LANES, N_KEYS, LANES), lambda t, c: (0, t, 0, 0))] * 3
                 + [pl.BlockSpec((EXPERT_CHUNK, D_MODEL), lambda t, c: (c, 0)),
                    pl.BlockSpec((EXPERT_CHUNK, D_MODEL), lambda t, c: (c, 0)),
                    pl.BlockSpec((1, D_MODEL), lambda t, c: (0, 0)),
                    pl.BlockSpec((1, D_MODEL), lambda t, c: (0, 0))],
        out_specs=pl.BlockSpec((tm, D_MODEL), lambda t, c: (t, 0)),
        out_shape=jax.ShapeDtypeStruct((rows, D_MODEL), F32),
        scratch_shapes=[pltpu.VMEM((D_MODEL, tm), BF16),
                        pltpu.VMEM((SUBLANES, 2 * PEER_HEADS, tm // LANES, 2 * SUBLANES, LANES), BF16),
                        pltpu.VMEM((tm // LANES, EXPERT_CHUNK, LANES), F32),
                        pltpu.VMEM((EXPERT_CHUNK, tm), BF16),
                        pltpu.VMEM((D_MODEL, tm), F32)],
        compiler_params=_cparams(("parallel", "arbitrary")),
        name="peer_dense",
    )(x, cnt, e1, sel, u, vt, ln_g, ln_b)


def _rel_bucket_np(dist):
    n = np.maximum(dist, 0)
    max_exact = NUM_BUCKETS // 2
    large = max_exact + (np.log(np.maximum(n, max_exact).astype(np.float32) / max_exact)
                         / math.log(MAX_DISTANCE / max_exact) * (NUM_BUCKETS - max_exact)).astype(np.int32)
    return np.where(n < max_exact, n, np.minimum(large, NUM_BUCKETS - 1))


def _bias_kernel(table_ref, bucket_ref, out_ref, *, head_stride):
    head = pl.program_id(0) * head_stride + pl.program_id(1)
    bucket = bucket_ref[0]
    acc = jnp.full(bucket.shape, NEG, F32)
    for b in range(NUM_BUCKETS):
        acc = jnp.where(bucket == b, table_ref[b, head], acc)
    out_ref[0, 0] = acc


def _bias_tiles(table, bucket, n_outer, n_inner, head_stride):
    r, c = bucket.shape[1:]
    bmap = (lambda a, b: (a, 0, 0)) if bucket.shape[0] > 1 else (lambda a, b: (0, 0, 0))
    return pl.pallas_call(
        functools.partial(_bias_kernel, head_stride=head_stride),
        grid=(n_outer, n_inner),
        in_specs=[pl.BlockSpec(memory_space=pltpu.SMEM), pl.BlockSpec((1, r, c), bmap)],
        out_specs=pl.BlockSpec((1, 1, r, c), lambda a, b: (a, b, 0, 0)),
        out_shape=jax.ShapeDtypeStruct((n_outer, n_inner, r, c), F32),
        compiler_params=_cparams(("parallel", "parallel")),
        name="bias_tiles",
    )(table, jnp.asarray(bucket, jnp.int32))


def _prompt_bias(table):
    qi = np.arange(ATTN_BLOCK)[:, None]
    kj = np.arange(2 * ATTN_BLOCK)[None, :]
    dist = ATTN_BLOCK + qi - kj
    ok = (dist >= 0) & (dist < WINDOW)
    ok = np.stack([ok, ok & (kj >= ATTN_BLOCK - N_META)])
    bucket = np.where(ok, _rel_bucket_np(dist)[None], -1)
    tiles = _bias_tiles(table, bucket, 2, N_HEADS, 0)
    return tiles.reshape(2, N_KV_HEADS, GROUP * ATTN_BLOCK, 2 * ATTN_BLOCK)


def _sample_bias(table, s_new):
    s, t = np.meshgrid(np.arange(SAMPLE_SEQS), np.arange(s_new), indexing="ij")
    s, t = s.reshape(-1, 1), t.reshape(-1, 1)
    out = []
    for cols, offset in ((WINDOW, WINDOW), (s_new, 0)):
        s2, c = np.meshgrid(np.arange(SAMPLE_SEQS), np.arange(cols), indexing="ij")
        s2, c = s2.reshape(1, -1), c.reshape(1, -1)
        dist = offset + t - c + 0 * s2
        ok = (s == s2) & (dist >= 0) & (dist < WINDOW)
        bucket = np.where(ok, _rel_bucket_np(dist), -1)[None]
        tiles = _bias_tiles(table, bucket, N_KV_HEADS, GROUP, GROUP)
        out.append(tiles.reshape(N_KV_HEADS, GROUP * bucket.shape[1], bucket.shape[2]))
    return out


def kernel(x_prompt, x_sample, cache_k, cache_v, state_pool, meta_tokens, rel_bias_table, w_in,
           w_pool_mix, pool_scale, attn_sinks, w_up_pool, w_up_attn, w_out, ln1_g, ln1_b,
           peer_w_query, peer_sub_keys, peer_u, peer_v, ln2_g, ln2_b):
    batch, seq, d = x_prompt.shape
    nb, s_new, _ = x_sample.shape
    assert w_in.shape[0] == DEPTH and d == D_MODEL and seq % ATTN_TILE == 0
    assert N_META >= max(POOL_WINDOWS) - 1 and cache_k.shape[2] == WINDOW

    w_in0 = w_in[0].astype(BF16)
    w_pqkv, w_gates = w_in0[:, :OFF_GA], w_in0[:, OFF_GA:]
    wmix = w_pool_mix[0].astype(BF16)
    scale = pool_scale[0].reshape(1, POOL_WIDTH)
    wup, wua, wout = w_up_pool[0].astype(BF16), w_up_attn[0].astype(BF16), w_out[0].astype(BF16)
    g1, b1 = ln1_g[0].reshape(1, d), ln1_b[0].reshape(1, d)
    g2, b2 = ln2_g[0].reshape(1, d), ln2_b[0].reshape(1, d)
    wq = peer_w_query[0].astype(BF16)
    keys = peer_sub_keys[0].reshape(2 * PEER_HEADS, N_KEYS, PEER_HALF).astype(BF16)
    u = peer_u[0]
    vt = peer_v[0]
    sinks = attn_sinks[0].astype(F32)
    table = rel_bias_table.astype(F32)
    bias_p = _prompt_bias(table)
    bias_c, bias_n = _sample_bias(table, s_new)

    xp = x_prompt.reshape(batch * seq, d)
    xs = x_sample.reshape(nb * s_new, d)
    xm = jnp.concatenate([jnp.zeros((ATTN_BLOCK - N_META, d), F32), meta_tokens.astype(F32)], axis=0)

    p_p, q_p, k_p, v_p = _inproj(xp, w_pqkv)
    p_s, q_s, k_s, v_s = _inproj(xs, w_pqkv)
    p_m, _, k_m, v_m = _inproj(xm, w_pqkv)

    pooled_p, attn_p = _prompt_mix(sinks, q_p, k_p, v_p, p_p, k_m, v_m, p_m[ATTN_BLOCK - N_META:],
                                   bias_p, batch, seq)
    ck = cache_k[0].reshape(nb, WINDOW, KV_DIM)
    cv = cache_v[0].reshape(nb, WINDOW, KV_DIM)
    pooled_s, attn_s, newk, newv, newp = _sample_mix(sinks, q_s, k_s, v_s, p_s, ck, cv,
                                                     state_pool[0], bias_c, bias_n, s_new)

    outs = []
    for x, pooled, attn in ((xp, pooled_p, attn_p), (xs, pooled_s, attn_s)):
        x1 = _merge(x, pooled, attn, w_gates, wmix, scale, wup, wua, wout, g1, b1)
        cnt, e1, sel = _peer_scores(x1, wq, keys)
        outs.append(_peer_dense(x1, cnt, e1, sel, u, vt, g2, b2))

    w_keep = min(WINDOW, seq + N_META)
    kv_shape = (batch, seq, N_KV_HEADS, HEAD_DIM)
    return (outs[0].reshape(batch, seq, d),
            outs[1].reshape(nb, s_new, d),
            k_p.reshape(kv_shape)[None, :, seq - w_keep:],
            v_p.reshape(kv_shape)[None, :, seq - w_keep:],
            p_p.reshape(batch, seq, POOL_WIDTH)[None, :, seq - POOL_STATE:],
            newk.reshape(1, nb, WINDOW, N_KV_HEADS, HEAD_DIM),
            newv.reshape(1, nb, WINDOW, N_KV_HEADS, HEAD_DIM),
            newp[None])
```

```python
import functools
import math

import jax
import jax.numpy as jnp
import numpy as np
from jax import lax
from jax.experimental import pallas as pl
from jax.experimental.pallas import tpu as pltpu

F32 = jnp.float32
BF16 = jnp.bfloat16

D_MODEL = 1024
N_META = 16
POOL_WIDTH = 512
POOL_WINDOWS = (2, 4, 8, 16)
POOL_GROUP_DIM = 128
POOL_STATE = 15
HEAD_DIM = 64
N_HEADS = 8
N_KV_HEADS = 2
GROUP = N_HEADS // N_KV_HEADS
WINDOW = 128
ATTN_BLOCK = 128
ATTN_SCALE = HEAD_DIM ** -0.5
NUM_BUCKETS = 32
MAX_DISTANCE = 128
Q_DIM = N_HEADS * HEAD_DIM
KV_DIM = N_KV_HEADS * HEAD_DIM
OFF_Q = POOL_WIDTH
OFF_K = OFF_Q + Q_DIM
OFF_V = OFF_K + KV_DIM
OFF_GA = OFF_V + KV_DIM
PEER_HEADS = 8
N_KEYS = 128
PEER_TOPK = 16
PEER_HALF = 128
DEPTH = 1
ALPHA = (2 * DEPTH) ** 0.25
LN_EPS = 1e-5
NEG = -1e30

LANES = 128
SUBLANES = 8
VMEM_LIMIT = 56 * 1024 * 1024

TOK_TILE = 512
ATTN_TILE = 512
SAMPLE_SEQS = 8
EXPERT_CHUNK = 1024


def _cparams(sem):
    return pltpu.CompilerParams(dimension_semantics=sem, vmem_limit_bytes=VMEM_LIMIT)


def _inproj_kernel(x_ref, w_ref, p_ref, q_ref, k_ref, v_ref):
    z = jnp.dot(x_ref[...].astype(BF16), w_ref[...], preferred_element_type=F32)
    p_ref[...] = z[:, :OFF_Q]
    q_ref[...] = (z[:, OFF_Q:OFF_K] * ATTN_SCALE).astype(BF16)
    k_ref[...] = z[:, OFF_K:OFF_V]
    v_ref[...] = z[:, OFF_V:OFF_GA]


def _inproj(x, w_pqkv):
    rows = x.shape[0]
    tm = min(TOK_TILE, rows)
    assert rows % tm == 0
    row = lambda c: pl.BlockSpec((tm, c), lambda i: (i, 0))
    return pl.pallas_call(
        _inproj_kernel,
        grid=(rows // tm,),
        in_specs=[row(D_MODEL), pl.BlockSpec((D_MODEL, OFF_GA), lambda i: (0, 0))],
        out_specs=[row(POOL_WIDTH), row(Q_DIM), row(KV_DIM), row(KV_DIM)],
        out_shape=[jax.ShapeDtypeStruct((rows, POOL_WIDTH), F32),
                   jax.ShapeDtypeStruct((rows, Q_DIM), BF16),
                   jax.ShapeDtypeStruct((rows, KV_DIM), F32),
                   jax.ShapeDtypeStruct((rows, KV_DIM), F32)],
        compiler_params=_cparams(("parallel",)),
        name="inproj",
    )(x, w_pqkv)


def _window_pool(ext_ref, first, rows, out_ref):
    for g, w in enumerate(POOL_WINDOWS):
        cols = slice(g * POOL_GROUP_DIM, (g + 1) * POOL_GROUP_DIM)
        cur = ext_ref[first:first + rows, cols]
        acc = cur
        for r in range(1, w):
            acc = acc + ext_ref[first - r:first - r + rows, cols]
        out_ref[:, cols] = (acc * (1.0 / w) - cur).astype(out_ref.dtype)


def _window_pool_tall(ext_ref, first, rows, out_ref):
    for g, w in enumerate(POOL_WINDOWS):
        cols = slice(g * POOL_GROUP_DIM, (g + 1) * POOL_GROUP_DIM)
        x = ext_ref[:, cols]
        s, span = x, 1
        while span < w:
            s = s + pltpu.roll(s, span, 0)
            span *= 2
        out_ref[:, cols] = (s[first:first + rows] * (1.0 / w) - x[first:first + rows]).astype(out_ref.dtype)


def _sink_softmax(s, sink):
    m = jnp.maximum(jnp.max(s, axis=-1, keepdims=True), sink)
    e = jnp.exp(s - m)
    denom = jnp.sum(e, axis=-1, keepdims=True) + jnp.exp(sink - m)
    return e * (1.0 / denom)


_NT = (((1,), (1,)), ((), ()))


def _prompt_mix_kernel(sink_ref, q_ref, k_ref, v_ref, p_ref, kprev_ref, vprev_ref, pprev_ref,
                       mk_ref, mv_ref, mp_ref, bias_ref, pooled_ref, attn_ref,
                       kbuf, vbuf, pbuf, sbuf, prob_buf):
    first = pl.program_id(1) == 0
    hist = ATTN_BLOCK
    kbuf[0:hist, :] = jnp.where(first, mk_ref[...], kprev_ref[...]).astype(BF16)
    vbuf[0:hist, :] = jnp.where(first, mv_ref[...], vprev_ref[...]).astype(BF16)
    kbuf[hist:, :] = k_ref[...].astype(BF16)
    vbuf[hist:, :] = v_ref[...].astype(BF16)
    pbuf[0:N_META, :] = jnp.where(first, mp_ref[...], pprev_ref[...])
    pbuf[N_META:, :] = p_ref[...]

    _window_pool_tall(pbuf, N_META, ATTN_TILE, pooled_ref)

    first_i = jnp.where(first, 1, 0)
    blocks = [(j, kh) for j in range(ATTN_TILE // ATTN_BLOCK) for kh in range(N_KV_HEADS)]
    rows = lambda j: slice(j * ATTN_BLOCK, (j + 1) * ATTN_BLOCK)
    keys = lambda j: slice(j * ATTN_BLOCK, j * ATTN_BLOCK + 2 * ATTN_BLOCK)
    head = lambda h: slice(h * HEAD_DIM, (h + 1) * HEAD_DIM)
    for i, (j, kh) in enumerate(blocks):
        qg = jnp.concatenate([q_ref[rows(j), head(kh * GROUP + g)] for g in range(GROUP)], axis=0)
        s = lax.dot_general(qg, kbuf[keys(j), head(kh)], _NT, preferred_element_type=F32)
        sbuf[i] = s + (bias_ref[first_i, kh] if j == 0 else bias_ref[0, kh])
    for i, (j, kh) in enumerate(blocks):
        sink = jnp.concatenate([jnp.full((ATTN_BLOCK, 1), sink_ref[kh * GROUP + g], F32) for g in range(GROUP)], axis=0)
        prob_buf[i] = _sink_softmax(sbuf[i], sink).astype(BF16)
    for i, (j, kh) in enumerate(blocks):
        o = jnp.dot(prob_buf[i], vbuf[keys(j), head(kh)], preferred_element_type=F32)
        for g in range(GROUP):
            attn_ref[rows(j), head(kh * GROUP + g)] = o[g * ATTN_BLOCK:(g + 1) * ATTN_BLOCK].astype(attn_ref.dtype)


def _prompt_mix(sinks, q, k, v, p, mk, mv, mp, bias, batch, seq):
    nt = seq // ATTN_TILE
    blocks_per_tile = ATTN_TILE // ATTN_BLOCK
    cur = lambda c: pl.BlockSpec((ATTN_TILE, c), lambda b, s: (b * nt + s, 0))
    prev_blk = lambda b, s: (jnp.maximum((b * nt + s) * blocks_per_tile - 1, 0), 0)
    prev_p = lambda b, s: (jnp.maximum((b * nt + s) * (ATTN_TILE // N_META) - 1, 0), 0)
    const2 = lambda b, s: (0, 0)
    rows = batch * seq
    return pl.pallas_call(
        _prompt_mix_kernel,
        grid=(batch, nt),
        in_specs=[pl.BlockSpec(memory_space=pltpu.SMEM),
                  cur(Q_DIM), cur(KV_DIM), cur(KV_DIM), cur(POOL_WIDTH),
                  pl.BlockSpec((ATTN_BLOCK, KV_DIM), prev_blk),
                  pl.BlockSpec((ATTN_BLOCK, KV_DIM), prev_blk),
                  pl.BlockSpec((N_META, POOL_WIDTH), prev_p),
                  pl.BlockSpec((ATTN_BLOCK, KV_DIM), const2),
                  pl.BlockSpec((ATTN_BLOCK, KV_DIM), const2),
                  pl.BlockSpec((N_META, POOL_WIDTH), const2),
                  pl.BlockSpec((2, N_KV_HEADS, GROUP * ATTN_BLOCK, 2 * ATTN_BLOCK), lambda b, s: (0, 0, 0, 0))],
        out_specs=[cur(POOL_WIDTH), cur(Q_DIM)],
        out_shape=[jax.ShapeDtypeStruct((rows, POOL_WIDTH), BF16),
                   jax.ShapeDtypeStruct((rows, Q_DIM), BF16)],
        scratch_shapes=[pltpu.VMEM((ATTN_BLOCK + ATTN_TILE, KV_DIM), BF16),
                        pltpu.VMEM((ATTN_BLOCK + ATTN_TILE, KV_DIM), BF16),
                        pltpu.VMEM((N_META + ATTN_TILE, POOL_WIDTH), F32),
                        pltpu.VMEM((blocks_per_tile * N_KV_HEADS, GROUP * ATTN_BLOCK, 2 * ATTN_BLOCK), F32),
                        pltpu.VMEM((blocks_per_tile * N_KV_HEADS, GROUP * ATTN_BLOCK, 2 * ATTN_BLOCK), BF16)],
        compiler_params=_cparams(("parallel", "parallel")),
        name="prompt_mix",
    )(sinks, q, k, v, p, k, v, p, mk, mv, mp, bias)


def _sample_mix_kernel(sink_ref, q_ref, k_ref, v_ref, p_ref, ck_ref, cv_ref, st_ref,
                       biasc_ref, biasn_ref,
                       pooled_ref, attn_ref, newk_ref, newv_ref, newp_ref, ext, pooled_buf):
    nseq, w_cache, s_new = SAMPLE_SEQS, WINDOW, q_ref.shape[0] // SAMPLE_SEQS
    keep = POOL_STATE - s_new
    for i in range(nseq):
        new = slice(i * s_new, (i + 1) * s_new)
        ext[1:1 + POOL_STATE, :] = st_ref[i]
        ext[1 + POOL_STATE:1 + POOL_STATE + s_new, :] = p_ref[new, :]
        _window_pool(ext, 1 + POOL_STATE, s_new, pooled_buf)
        pooled_ref[new, :] = pooled_buf[...].astype(pooled_ref.dtype)
        newp_ref[i, 0:keep, :] = st_ref[i, s_new:POOL_STATE, :]
        newp_ref[i, keep:POOL_STATE, :] = p_ref[new, :]
        newk_ref[i, 0:w_cache - s_new, :] = ck_ref[i, s_new:w_cache, :]
        newk_ref[i, w_cache - s_new:w_cache, :] = k_ref[new, :]
        newv_ref[i, 0:w_cache - s_new, :] = cv_ref[i, s_new:w_cache, :]
        newv_ref[i, w_cache - s_new:w_cache, :] = v_ref[new, :]

    kc = ck_ref[...].reshape(nseq * w_cache, KV_DIM).astype(BF16)
    vc = cv_ref[...].reshape(nseq * w_cache, KV_DIM).astype(BF16)
    kn = k_ref[...].astype(BF16)
    vn = v_ref[...].astype(BF16)
    q = q_ref[...]
    nq = q.shape[0]
    for kh in range(N_KV_HEADS):
        kvc = slice(kh * HEAD_DIM, (kh + 1) * HEAD_DIM)
        qg = jnp.concatenate([q[:, (kh * GROUP + g) * HEAD_DIM:(kh * GROUP + g + 1) * HEAD_DIM]
                              for g in range(GROUP)], axis=0)
        sc = lax.dot_general(qg, kc[:, kvc], _NT, preferred_element_type=F32) + biasc_ref[kh]
        sn = lax.dot_general(qg, kn[:, kvc], _NT, preferred_element_type=F32) + biasn_ref[kh]
        sink = jnp.concatenate([jnp.full((nq, 1), sink_ref[kh * GROUP + g], F32) for g in range(GROUP)], axis=0)
        m = jnp.maximum(jnp.maximum(jnp.max(sc, axis=-1, keepdims=True),
                                    jnp.max(sn, axis=-1, keepdims=True)), sink)
        ec = jnp.exp(sc - m)
        en = jnp.exp(sn - m)
        denom = (jnp.sum(ec, axis=-1, keepdims=True) + jnp.sum(en, axis=-1, keepdims=True)
                 + jnp.exp(sink - m))
        inv = 1.0 / denom
        o = (jnp.dot((ec * inv).astype(BF16), vc[:, kvc], preferred_element_type=F32)
             + jnp.dot((en * inv).astype(BF16), vn[:, kvc], preferred_element_type=F32))
        for g in range(GROUP):
            h = kh * GROUP + g
            attn_ref[:, h * HEAD_DIM:(h + 1) * HEAD_DIM] = o[g * nq:(g + 1) * nq].astype(attn_ref.dtype)


def _sample_mix(sinks, q, k, v, p, cache_k, cache_v, state, biasc, biasn, s_new):
    nb = cache_k.shape[0]
    rows = SAMPLE_SEQS * s_new
    tok = lambda c: pl.BlockSpec((rows, c), lambda i: (i, 0))
    seq3 = lambda r, c: pl.BlockSpec((SAMPLE_SEQS, r, c), lambda i: (i, 0, 0))
    const3 = lambda a: pl.BlockSpec(a.shape, lambda i: (0, 0, 0))
    return pl.pallas_call(
        _sample_mix_kernel,
        grid=(nb // SAMPLE_SEQS,),
        in_specs=[pl.BlockSpec(memory_space=pltpu.SMEM),
                  tok(Q_DIM), tok(KV_DIM), tok(KV_DIM), tok(POOL_WIDTH),
                  seq3(WINDOW, KV_DIM), seq3(WINDOW, KV_DIM), seq3(POOL_STATE, POOL_WIDTH),
                  const3(biasc), const3(biasn)],
        out_specs=[tok(POOL_WIDTH), tok(Q_DIM),
                   seq3(WINDOW, KV_DIM), seq3(WINDOW, KV_DIM), seq3(POOL_STATE, POOL_WIDTH)],
        out_shape=[jax.ShapeDtypeStruct((nb * s_new, POOL_WIDTH), F32),
                   jax.ShapeDtypeStruct((nb * s_new, Q_DIM), F32),
                   jax.ShapeDtypeStruct((nb, WINDOW, KV_DIM), F32),
                   jax.ShapeDtypeStruct((nb, WINDOW, KV_DIM), F32),
                   jax.ShapeDtypeStruct((nb, POOL_STATE, POOL_WIDTH), F32)],
        scratch_shapes=[pltpu.VMEM((1 + POOL_STATE + SUBLANES, POOL_WIDTH), F32),
                        pltpu.VMEM((s_new, POOL_WIDTH), F32)],
        compiler_params=_cparams(("parallel",)),
        name="sample_mix",
    )(sinks, q, k, v, p, cache_k, cache_v, state, biasc, biasn)


def _layer_norm(x, g, b):
    mu = jnp.mean(x, axis=-1, keepdims=True)
    xc = x - mu
    var = jnp.mean(xc * xc, axis=-1, keepdims=True)
    return xc * lax.rsqrt(var + LN_EPS) * g + b


def _merge_kernel(x_ref, pooled_ref, attn_ref, wg_ref, wmix_ref, scale_ref, wup_ref, wua_ref,
                  wout_ref, g_ref, b_ref, y_ref):
    x = x_ref[...]
    glog = jnp.dot(x.astype(BF16), wg_ref[...], preferred_element_type=F32)
    g_pool = jax.nn.sigmoid(glog[:, :D_MODEL])
    g_attn = jax.nn.sigmoid(glog[:, D_MODEL:])
    pooled = pooled_ref[...].astype(BF16)
    mixed = jnp.concatenate(
        [jnp.dot(pooled[:, g * POOL_GROUP_DIM:(g + 1) * POOL_GROUP_DIM], wmix_ref[g],
                 preferred_element_type=F32) for g in range(len(POOL_WINDOWS))], axis=1)
    pool_out = (mixed * scale_ref[...]).astype(BF16)
    a = jnp.dot(pool_out, wup_ref[...], preferred_element_type=F32)
    b = jnp.dot(attn_ref[...].astype(BF16), wua_ref[...], preferred_element_type=F32)
    m = (g_pool * a + g_attn * b).astype(BF16)
    r = jnp.dot(m, wout_ref[...], preferred_element_type=F32)
    y_ref[...] = _layer_norm(ALPHA * x + r, g_ref[...], b_ref[...])


def _merge(x, pooled, attn, wg, wmix, scale, wup, wua, wout, ln_g, ln_b):
    rows = x.shape[0]
    tm = min(TOK_TILE, rows)
    assert rows % tm == 0
    row = lambda c: pl.BlockSpec((tm, c), lambda i: (i, 0))
    full = lambda a: pl.BlockSpec(a.shape, lambda i: (0,) * a.ndim)
    weights = (wg, wmix, scale, wup, wua, wout, ln_g, ln_b)
    return pl.pallas_call(
        _merge_kernel,
        grid=(rows // tm,),
        in_specs=[row(D_MODEL), row(POOL_WIDTH), row(Q_DIM)] + [full(a) for a in weights],
        out_specs=row(D_MODEL),
        out_shape=jax.ShapeDtypeStruct((rows, D_MODEL), F32),
        compiler_params=_cparams(("parallel",)),
        name="merge",
    )(x, pooled, attn, *weights)


def _oddeven_merge_sort_pairs(n):
    pairs = []
    p = 1
    while p < n:
        k = p
        while k >= 1:
            for j in range(k % p, n - k, 2 * k):
                for i in range(min(k, n - j - k)):
                    if (i + j) // (2 * p) == (i + j + k) // (2 * p):
                        pairs.append((i + j, i + j + k))
            k //= 2
        p *= 2
    return pairs


def _bitonic_merge_pairs(n):
    pairs = []
    k = n // 2
    while k >= 1:
        pairs += [(i, i + k) for i in range(n) if not i & k]
        k //= 2
    return pairs


def _apply_network(vals, pairs):
    vals = list(vals)
    for i, j in pairs:
        a, b = vals[i], vals[j]
        if b is None:
            continue
        if a is None:
            vals[i], vals[j] = b, None
        else:
            vals[i], vals[j] = jnp.maximum(a, b), jnp.minimum(a, b)
    return vals


_SORT16 = _oddeven_merge_sort_pairs(PEER_TOPK)
_MERGE16 = _bitonic_merge_pairs(PEER_TOPK)


def _top16_rows(sc):
    n = sc.shape[0] // SUBLANES
    assert n == PEER_TOPK
    x = _apply_network([sc[k * SUBLANES:(k + 1) * SUBLANES, :] for k in range(n)], _SORT16)
    for shift in (4, 2, 1):
        y = [jnp.maximum(x[k], pltpu.roll(x[n - 1 - k], shift, 0)) for k in range(n)]
        x = _apply_network(y, _MERGE16)
    return x


def _best_sums(v1, v2):
    row = lax.broadcasted_iota(jnp.int32, v1[0].shape, 0)

    def one_per_sublane(vals):
        x = vals[SUBLANES - 1]
        for s in range(SUBLANES - 2, -1, -1):
            x = jnp.where(row == s, vals[s], x)
        return x

    lo, hi = one_per_sublane(v1[:SUBLANES]), one_per_sublane(v1[SUBLANES:])
    sums_lo = []
    for b in range(PEER_TOPK):
        n_valid = sum((a + 1) * (b + 1) <= PEER_TOPK for a in range(SUBLANES))
        s = lo + v2[b]
        sums_lo.append(s if n_valid == SUBLANES else jnp.where(row < n_valid, s, NEG))
    sums_hi = hi + v2[0]
    x, best = sums_hi, []
    for k in range(PEER_TOPK):
        best.append(jnp.maximum(sums_lo[k], x))
        x = jnp.minimum(sums_lo[k], x)
    for shift in (4, 2, 1):
        y = [jnp.maximum(best[k], pltpu.roll(best[PEER_TOPK - 1 - k], shift, 0)) for k in range(PEER_TOPK)]
        best = _apply_network(y, _MERGE16)
    return best, sums_lo, sums_hi


def _peer_scores_kernel(x_ref, wq_ref, keys_ref, cnt_ref, e1_ref, sel_ref, sc_buf):
    q = jnp.dot(x_ref[...].astype(BF16), wq_ref[...], preferred_element_type=F32).astype(BF16)
    ncol = q.shape[0] // LANES
    for hc in range(2 * PEER_HEADS):
        sc = lax.dot_general(keys_ref[hc], q[:, hc * PEER_HALF:(hc + 1) * PEER_HALF], _NT,
                             preferred_element_type=F32)
        for t in range(ncol):
            sc_buf[hc, t] = sc[:, t * LANES:(t + 1) * LANES]

    def select(h, carry):
        for t in range(ncol):
            s1, s2 = sc_buf[2 * h, t], sc_buf[2 * h + 1, t]
            v1, v2 = _top16_rows(s1), _top16_rows(s2)
            best, sums_lo, _ = _best_sums(v1, v2)
            tau = best[PEER_TOPK - 1]
            z = jnp.ones_like(tau)
            for r in range(1, PEER_TOPK):
                z = z + jnp.exp(best[r] - best[0])
            inv_z = (0.5 / GELU_ARG) / z
            cnt_lo = jnp.zeros_like(tau)
            for s in sums_lo:
                cnt_lo = cnt_lo + jnp.where(s >= tau, 1.0, 0.0)
            row = lambda r: r[0:1, :]
            half = float(SUBLANES)
            cnt_lo = jnp.where(cnt_lo > half, half + 1.0, cnt_lo)
            cnt = jnp.where((s1 <= row(v1[SUBLANES])) & (s1 + row(v2[0]) >= row(tau)), 1.0, 0.0)
            rank = jnp.where((s2 <= row(v2[SUBLANES])) & (row(v1[0]) + s2 >= row(tau)), half, half + 1.0)
            for a in range(SUBLANES - 1, -1, -1):
                cnt = jnp.where(s1 == row(v1[a]), cnt_lo[a:a + 1, :], cnt)
                rank = jnp.where(s2 >= row(v2[a]), float(a), rank)
            cnt_ref[h, t] = cnt
            e1_ref[h, t] = jnp.exp(s1 - row(v1[0])) * row(inv_z)
            rank_words = pltpu.bitcast(rank.astype(BF16), jnp.uint32)
            e2_words = pltpu.bitcast(jnp.exp(s2 - row(v2[0])).astype(BF16), jnp.uint32)
            for k in range(N_KEYS // (2 * SUBLANES)):
                words = slice(k * SUBLANES, (k + 1) * SUBLANES)
                sel_ref[h, t, 2 * k * SUBLANES:(2 * k + 1) * SUBLANES, :] = rank_words[words]
                sel_ref[h, t, (2 * k + 1) * SUBLANES:(2 * k + 2) * SUBLANES, :] = e2_words[words]

        return carry

    lax.fori_loop(0, PEER_HEADS, select, 0)


def _peer_scores(x, wq, keys):
    rows = x.shape[0]
    tm = min(TOK_TILE, rows)
    assert rows % tm == 0
    return pl.pallas_call(
        _peer_scores_kernel,
        grid=(rows // tm,),
        in_specs=[pl.BlockSpec((tm, D_MODEL), lambda i: (i, 0)),
                  pl.BlockSpec(wq.shape, lambda i: (0, 0)),
                  pl.BlockSpec(keys.shape, lambda i: (0, 0, 0))],
        out_specs=[pl.BlockSpec((PEER_HEADS, tm // LANES, N_KEYS, LANES), lambda i: (0, i, 0, 0))] * 3,
        out_shape=[jax.ShapeDtypeStruct((PEER_HEADS, rows // LANES, N_KEYS, LANES), dt)
                   for dt in (F32, F32, jnp.uint32)],
        scratch_shapes=[pltpu.VMEM((2 * PEER_HEADS, tm // LANES, N_KEYS, LANES), F32)],
        compiler_params=_cparams(("parallel",)),
        name="peer_scores",
    )(x, wq, keys)


GELU_ARG = math.sqrt(0.5)


def _gelu_scaled(s):
    return s * (1.0 + lax.erf(s))


def _peer_dense_kernel(x_ref, cnt_ref, e1_ref, sel_ref, u_ref, v_ref, g_ref, b_ref, y_ref,
                       xt, rowb, hid, wt, acc):
    c = pl.program_id(1)
    rows_per_chunk = EXPERT_CHUNK // N_KEYS
    ncol = xt.shape[1] // LANES
    packed = 2 * SUBLANES
    assert rows_per_chunk == SUBLANES

    @pl.when(c == 0)
    def _():
        xt[...] = x_ref[...].T.astype(BF16)
        acc[...] = jnp.zeros_like(acc)

    def build_rows(chunk, heads):
        for h in heads:
            for col in range(ncol):
                for r in range(SUBLANES):
                    row = pl.ds(chunk * SUBLANES + r, packed, stride=0)
                    rowb[r, h, col] = cnt_ref[h, col, row, :].astype(BF16)
                    rowb[r, PEER_HEADS + h, col] = e1_ref[h, col, row, :].astype(BF16)

    def mask_row(il, carry):
        base = pl.multiple_of(il * N_KEYS, N_KEYS)
        for col in range(ncol):
            gate = [None] * (N_KEYS // packed)
            for h in range(PEER_HEADS):
                cnt = rowb[il, h, col]
                g1 = rowb[il, PEER_HEADS + h, col]
                for k in range(N_KEYS // packed):
                    rank = pltpu.bitcast(sel_ref[h, col, 2 * k * SUBLANES:(2 * k + 1) * SUBLANES, :], BF16)
                    g2 = pltpu.bitcast(sel_ref[h, col, (2 * k + 1) * SUBLANES:(2 * k + 2) * SUBLANES, :], BF16)
                    g = jnp.where(rank < cnt, g2 * g1, jnp.zeros((), BF16))
                    gate[k] = g if gate[k] is None else gate[k] + g
            for k in range(N_KEYS // packed):
                act = _gelu_scaled(hid[col, pl.ds(base + k * packed, packed), :]).astype(BF16)
                wt[pl.ds(base + k * packed, packed), col * LANES:(col + 1) * LANES] = act * gate[k]
        return carry

    n_slabs = 4
    slab = EXPERT_CHUNK // n_slabs

    for i in range(n_slabs):
        rows = slice(i * slab, (i + 1) * slab)
        res = jnp.dot(u_ref[rows, :].astype(BF16), xt[...], preferred_element_type=F32)
        for col in range(ncol):
            hid[col, rows, :] = res[:, col * LANES:(col + 1) * LANES] * GELU_ARG
        build_rows(c, range(i * PEER_HEADS // n_slabs, (i + 1) * PEER_HEADS // n_slabs))
    lax.fori_loop(0, rows_per_chunk, mask_row, 0)
    acc[...] += lax.dot_general(v_ref[...].astype(BF16), wt[...], (((0,), (0,)), ((), ())),
                                preferred_element_type=F32)

    @pl.when(c == pl.num_programs(1) - 1)
    def _():
        y_ref[...] = _layer_norm(ALPHA * x_ref[...] + acc[...].T, g_ref[...], b_ref[...])


def _peer_dense(x, cnt, e1, sel, u, vt, ln_g, ln_b):
    rows = x.shape[0]
    tm = min(TOK_TILE, rows)
    assert rows % tm == 0
    n_exp = u.shape[0]
    assert n_exp % EXPERT_CHUNK == 0
    return pl.pallas_call(
        _peer_dense_kernel,
        grid=(rows // tm, n_exp // EXPERT_CHUNK),
        in_specs=[pl.BlockSpec((tm, D_MODEL), lambda t, c: (t, 0))]
                 + [pl.BlockSpec((PEER_HEADS, tm // LANES, N_KEYS, LANES), lambda t, c: (0, t, 0, 0))] * 3
                 + [pl.BlockSpec((EXPERT_CHUNK, D_MODEL), lambda t, c: (c, 0)),
                    pl.BlockSpec((EXPERT_CHUNK, D_MODEL), lambda t, c: (c, 0)),
                    pl.BlockSpec((1, D_MODEL), lambda t, c: (0, 0)),
                    pl.BlockSpec((1, D_MODEL), lambda t, c: (0, 0))],
        out_specs=pl.BlockSpec((tm, D_MODEL), lambda t, c: (t, 0)),
        out_shape=jax.ShapeDtypeStruct((rows, D_MODEL), F32),
        scratch_shapes=[pltpu.VMEM((D_MODEL, tm), BF16),
                        pltpu.VMEM((SUBLANES, 2 * PEER_HEADS, tm // LANES, 2 * SUBLANES, LANES), BF16),
                        pltpu.VMEM((tm // LANES, EXPERT_CHUNK, LANES), F32),
                        pltpu.VMEM((EXPERT_CHUNK, tm), BF16),
                        pltpu.VMEM((D_MODEL, tm), F32)],
        compiler_params=_cparams(("parallel", "arbitrary")),
        name="peer_dense",
    )(x, cnt, e1, sel, u, vt, ln_g, ln_b)


def _rel_bucket_np(dist):
    n = np.maximum(dist, 0)
    max_exact = NUM_BUCKETS // 2
    large = max_exact + (np.log(np.maximum(n, max_exact).astype(np.float32) / max_exact)
                         / math.log(MAX_DISTANCE / max_exact) * (NUM_BUCKETS - max_exact)).astype(np.int32)
    return np.where(n < max_exact, n, np.minimum(large, NUM_BUCKETS - 1))


def _bias_kernel(table_ref, bucket_ref, out_ref, *, head_stride):
    head = pl.program_id(0) * head_stride + pl.program_id(1)
    bucket = bucket_ref[0]
    acc = jnp.full(bucket.shape, NEG, F32)
    for b in range(NUM_BUCKETS):
        acc = jnp.where(bucket == b, table_ref[b, head], acc)
    out_ref[0, 0] = acc


def _bias_tiles(table, bucket, n_outer, n_inner, head_stride):
    r, c = bucket.shape[1:]
    bmap = (lambda a, b: (a, 0, 0)) if bucket.shape[0] > 1 else (lambda a, b: (0, 0, 0))
    return pl.pallas_call(
        functools.partial(_bias_kernel, head_stride=head_stride),
        grid=(n_outer, n_inner),
        in_specs=[pl.BlockSpec(memory_space=pltpu.SMEM), pl.BlockSpec((1, r, c), bmap)],
        out_specs=pl.BlockSpec((1, 1, r, c), lambda a, b: (a, b, 0, 0)),
        out_shape=jax.ShapeDtypeStruct((n_outer, n_inner, r, c), F32),
        compiler_params=_cparams(("parallel", "parallel")),
        name="bias_tiles",
    )(table, jnp.asarray(bucket, jnp.int32))


def _prompt_bias(table):
    qi = np.arange(ATTN_BLOCK)[:, None]
    kj = np.arange(2 * ATTN_BLOCK)[None, :]
    dist = ATTN_BLOCK + qi - kj
    ok = (dist >= 0) & (dist < WINDOW)
    ok = np.stack([ok, ok & (kj >= ATTN_BLOCK - N_META)])
    bucket = np.where(ok, _rel_bucket_np(dist)[None], -1)
    tiles = _bias_tiles(table, bucket, 2, N_HEADS, 0)
    return tiles.reshape(2, N_KV_HEADS, GROUP * ATTN_BLOCK, 2 * ATTN_BLOCK)


def _sample_bias(table, s_new):
    s, t = np.meshgrid(np.arange(SAMPLE_SEQS), np.arange(s_new), indexing="ij")
    s, t = s.reshape(-1, 1), t.reshape(-1, 1)
    out = []
    for cols, offset in ((WINDOW, WINDOW), (s_new, 0)):
        s2, c = np.meshgrid(np.arange(SAMPLE_SEQS), np.arange(cols), indexing="ij")
        s2, c = s2.reshape(1, -1), c.reshape(1, -1)
        dist = offset + t - c + 0 * s2
        ok = (s == s2) & (dist >= 0) & (dist < WINDOW)
        bucket = np.where(ok, _rel_bucket_np(dist), -1)[None]
        tiles = _bias_tiles(table, bucket, N_KV_HEADS, GROUP, GROUP)
        out.append(tiles.reshape(N_KV_HEADS, GROUP * bucket.shape[1], bucket.shape[2]))
    return out


def kernel(x_prompt, x_sample, cache_k, cache_v, state_pool, meta_tokens, rel_bias_table, w_in,
           w_pool_mix, pool_scale, attn_sinks, w_up_pool, w_up_attn, w_out, ln1_g, ln1_b,
           peer_w_query, peer_sub_keys, peer_u, peer_v, ln2_g, ln2_b):
    batch, seq, d = x_prompt.shape
    nb, s_new, _ = x_sample.shape
    assert w_in.shape[0] == DEPTH and d == D_MODEL and seq % ATTN_TILE == 0
    assert N_META >= max(POOL_WINDOWS) - 1 and cache_k.shape[2] == WINDOW

    w_in0 = w_in[0].astype(BF16)
    w_pqkv, w_gates = w_in0[:, :OFF_GA], w_in0[:, OFF_GA:]
    wmix = w_pool_mix[0].astype(BF16)
    scale = pool_scale[0].reshape(1, POOL_WIDTH)
    wup, wua, wout = w_up_pool[0].astype(BF16), w_up_attn[0].astype(BF16), w_out[0].astype(BF16)
    g1, b1 = ln1_g[0].reshape(1, d), ln1_b[0].reshape(1, d)
    g2, b2 = ln2_g[0].reshape(1, d), ln2_b[0].reshape(1, d)
    wq = peer_w_query[0].astype(BF16)
    keys = peer_sub_keys[0].reshape(2 * PEER_HEADS, N_KEYS, PEER_HALF).astype(BF16)
    u = peer_u[0]
    vt = peer_v[0]
    sinks = attn_sinks[0].astype(F32)
    table = rel_bias_table.astype(F32)
    bias_p = _prompt_bias(table)
    bias_c, bias_n = _sample_bias(table, s_new)

    xp = x_prompt.reshape(batch * seq, d)
    xs = x_sample.reshape(nb * s_new, d)
    xm = jnp.concatenate([jnp.zeros((ATTN_BLOCK - N_META, d), F32), meta_tokens.astype(F32)], axis=0)

    p_p, q_p, k_p, v_p = _inproj(xp, w_pqkv)
    p_s, q_s, k_s, v_s = _inproj(xs, w_pqkv)
    p_m, _, k_m, v_m = _inproj(xm, w_pqkv)

    pooled_p, attn_p = _prompt_mix(sinks, q_p, k_p, v_p, p_p, k_m, v_m, p_m[ATTN_BLOCK - N_META:],
                                   bias_p, batch, seq)
    ck = cache_k[0].reshape(nb, WINDOW, KV_DIM)
    cv = cache_v[0].reshape(nb, WINDOW, KV_DIM)
    pooled_s, attn_s, newk, newv, newp = _sample_mix(sinks, q_s, k_s, v_s, p_s, ck, cv,
                                                     state_pool[0], bias_c, bias_n, s_new)

    outs = []
    for x, pooled, attn in ((xp, pooled_p, attn_p), (xs, pooled_s, attn_s)):
        x1 = _merge(x, pooled, attn, w_gates, wmix, scale, wup, wua, wout, g1, b1)
        cnt, e1, sel = _peer_scores(x1, wq, keys)
        outs.append(_peer_dense(x1, cnt, e1, sel, u, vt, g2, b2))

    w_keep = min(WINDOW, seq + N_META)
    kv_shape = (batch, seq, N_KV_HEADS, HEAD_DIM)
    return (outs[0].reshape(batch, seq, d),
            outs[1].reshape(nb, s_new, d),
            k_p.reshape(kv_shape)[None, :, seq - w_keep:],
            v_p.reshape(kv_shape)[None, :, seq - w_keep:],
            p_p.reshape(batch, seq, POOL_WIDTH)[None, :, seq - POOL_STATE:],
            newk.reshape(1, nb, WINDOW, N_KV_HEADS, HEAD_DIM),
            newv.reshape(1, nb, WINDOW, N_KV_HEADS, HEAD_DIM),
            newp[None])
```

```python
import functools
import math

import jax
import jax.numpy as jnp
import numpy as np
from jax import lax
from jax.experimental import pallas as pl
from jax.experimental.pallas import tpu as pltpu

F32 = jnp.float32
BF16 = jnp.bfloat16

D_MODEL = 1024
N_META = 16
POOL_WIDTH = 512
POOL_WINDOWS = (2, 4, 8, 16)
POOL_GROUP_DIM = 128
POOL_STATE = 15
HEAD_DIM = 64
N_HEADS = 8
N_KV_HEADS = 2
GROUP = N_HEADS // N_KV_HEADS
WINDOW = 128
ATTN_BLOCK = 128
ATTN_SCALE = HEAD_DIM ** -0.5
NUM_BUCKETS = 32
MAX_DISTANCE = 128
Q_DIM = N_HEADS * HEAD_DIM
KV_DIM = N_KV_HEADS * HEAD_DIM
OFF_Q = POOL_WIDTH
OFF_K = OFF_Q + Q_DIM
OFF_V = OFF_K + KV_DIM
OFF_GA = OFF_V + KV_DIM
PEER_HEADS = 8
N_KEYS = 128
PEER_TOPK = 16
PEER_HALF = 128
DEPTH = 1
ALPHA = (2 * DEPTH) ** 0.25
LN_EPS = 1e-5
NEG = -1e30

LANES = 128
SUBLANES = 8
VMEM_LIMIT = 56 * 1024 * 1024

TOK_TILE = 512
ATTN_TILE = 1024
SAMPLE_SEQS = 8
EXPERT_CHUNK = 1024


def _cparams(sem):
    return pltpu.CompilerParams(dimension_semantics=sem, vmem_limit_bytes=VMEM_LIMIT)


def _inproj_kernel(x_ref, w_ref, p_ref, q_ref, k_ref, v_ref):
    z = jnp.dot(x_ref[...].astype(BF16), w_ref[...], preferred_element_type=F32)
    p_ref[...] = z[:, :OFF_Q]
    q_ref[...] = (z[:, OFF_Q:OFF_K] * ATTN_SCALE).astype(BF16)
    k_ref[...] = z[:, OFF_K:OFF_V]
    v_ref[...] = z[:, OFF_V:OFF_GA]


def _inproj(x, w_pqkv):
    rows = x.shape[0]
    tm = min(TOK_TILE, rows)
    assert rows % tm == 0
    row = lambda c: pl.BlockSpec((tm, c), lambda i: (i, 0))
    return pl.pallas_call(
        _inproj_kernel,
        grid=(rows // tm,),
        in_specs=[row(D_MODEL), pl.BlockSpec((D_MODEL, OFF_GA), lambda i: (0, 0))],
        out_specs=[row(POOL_WIDTH), row(Q_DIM), row(KV_DIM), row(KV_DIM)],
        out_shape=[jax.ShapeDtypeStruct((rows, POOL_WIDTH), F32),
                   jax.ShapeDtypeStruct((rows, Q_DIM), BF16),
                   jax.ShapeDtypeStruct((rows, KV_DIM), F32),
                   jax.ShapeDtypeStruct((rows, KV_DIM), F32)],
        compiler_params=_cparams(("parallel",)),
        name="inproj",
    )(x, w_pqkv)


def _window_pool(ext_ref, first, rows, out_ref):
    for g, w in enumerate(POOL_WINDOWS):
        cols = slice(g * POOL_GROUP_DIM, (g + 1) * POOL_GROUP_DIM)
        cur = ext_ref[first:first + rows, cols]
        acc = cur
        for r in range(1, w):
            acc = acc + ext_ref[first - r:first - r + rows, cols]
        out_ref[:, cols] = (acc * (1.0 / w) - cur).astype(out_ref.dtype)


def _window_pool_tall(ext_ref, first, rows, out_ref):
    for g, w in enumerate(POOL_WINDOWS):
        cols = slice(g * POOL_GROUP_DIM, (g + 1) * POOL_GROUP_DIM)
        x = ext_ref[:, cols]
        s, span = x, 1
        while span < w:
            s = s + pltpu.roll(s, span, 0)
            span *= 2
        out_ref[:, cols] = (s[first:first + rows] * (1.0 / w) - x[first:first + rows]).astype(out_ref.dtype)


def _sink_softmax(s, sink):
    m = jnp.maximum(jnp.max(s, axis=-1, keepdims=True), sink)
    e = jnp.exp(s - m)
    denom = jnp.sum(e, axis=-1, keepdims=True) + jnp.exp(sink - m)
    return e * (1.0 / denom)


_NT = (((1,), (1,)), ((), ()))


def _prompt_mix_kernel(sink_ref, q_ref, k_ref, v_ref, p_ref, kprev_ref, vprev_ref, pprev_ref,
                       mk_ref, mv_ref, mp_ref, bias_ref, pooled_ref, attn_ref,
                       kbuf, vbuf, pbuf, sbuf, prob_buf):
    first = pl.program_id(1) == 0
    hist = ATTN_BLOCK
    kbuf[0:hist, :] = jnp.where(first, mk_ref[...], kprev_ref[...]).astype(BF16)
    vbuf[0:hist, :] = jnp.where(first, mv_ref[...], vprev_ref[...]).astype(BF16)
    kbuf[hist:, :] = k_ref[...].astype(BF16)
    vbuf[hist:, :] = v_ref[...].astype(BF16)
    pbuf[0:N_META, :] = jnp.where(first, mp_ref[...], pprev_ref[...])
    pbuf[N_META:, :] = p_ref[...]

    _window_pool_tall(pbuf, N_META, ATTN_TILE, pooled_ref)

    first_i = jnp.where(first, 1, 0)
    blocks = [(j, kh) for j in range(ATTN_TILE // ATTN_BLOCK) for kh in range(N_KV_HEADS)]
    rows = lambda j: slice(j * ATTN_BLOCK, (j + 1) * ATTN_BLOCK)
    keys = lambda j: slice(j * ATTN_BLOCK, j * ATTN_BLOCK + 2 * ATTN_BLOCK)
    head = lambda h: slice(h * HEAD_DIM, (h + 1) * HEAD_DIM)
    for i, (j, kh) in enumerate(blocks):
        qg = jnp.concatenate([q_ref[rows(j), head(kh * GROUP + g)] for g in range(GROUP)], axis=0)
        s = lax.dot_general(qg, kbuf[keys(j), head(kh)], _NT, preferred_element_type=F32)
        sbuf[i] = s + (bias_ref[first_i, kh] if j == 0 else bias_ref[0, kh])
    for i, (j, kh) in enumerate(blocks):
        sink = jnp.concatenate([jnp.full((ATTN_BLOCK, 1), sink_ref[kh * GROUP + g], F32) for g in range(GROUP)], axis=0)
        prob_buf[i] = _sink_softmax(sbuf[i], sink).astype(BF16)
    for i, (j, kh) in enumerate(blocks):
        o = jnp.dot(prob_buf[i], vbuf[keys(j), head(kh)], preferred_element_type=F32)
        for g in range(GROUP):
            attn_ref[rows(j), head(kh * GROUP + g)] = o[g * ATTN_BLOCK:(g + 1) * ATTN_BLOCK].astype(attn_ref.dtype)


def _prompt_mix(sinks, q, k, v, p, mk, mv, mp, bias, batch, seq):
    nt = seq // ATTN_TILE
    blocks_per_tile = ATTN_TILE // ATTN_BLOCK
    cur = lambda c: pl.BlockSpec((ATTN_TILE, c), lambda b, s: (b * nt + s, 0))
    prev_blk = lambda b, s: (jnp.maximum((b * nt + s) * blocks_per_tile - 1, 0), 0)
    prev_p = lambda b, s: (jnp.maximum((b * nt + s) * (ATTN_TILE // N_META) - 1, 0), 0)
    const2 = lambda b, s: (0, 0)
    rows = batch * seq
    return pl.pallas_call(
        _prompt_mix_kernel,
        grid=(batch, nt),
        in_specs=[pl.BlockSpec(memory_space=pltpu.SMEM),
                  cur(Q_DIM), cur(KV_DIM), cur(KV_DIM), cur(POOL_WIDTH),
                  pl.BlockSpec((ATTN_BLOCK, KV_DIM), prev_blk),
                  pl.BlockSpec((ATTN_BLOCK, KV_DIM), prev_blk),
                  pl.BlockSpec((N_META, POOL_WIDTH), prev_p),
                  pl.BlockSpec((ATTN_BLOCK, KV_DIM), const2),
                  pl.BlockSpec((ATTN_BLOCK, KV_DIM), const2),
                  pl.BlockSpec((N_META, POOL_WIDTH), const2),
                  pl.BlockSpec((2, N_KV_HEADS, GROUP * ATTN_BLOCK, 2 * ATTN_BLOCK), lambda b, s: (0, 0, 0, 0))],
        out_specs=[cur(POOL_WIDTH), cur(Q_DIM)],
        out_shape=[jax.ShapeDtypeStruct((rows, POOL_WIDTH), BF16),
                   jax.ShapeDtypeStruct((rows, Q_DIM), BF16)],
        scratch_shapes=[pltpu.VMEM((ATTN_BLOCK + ATTN_TILE, KV_DIM), BF16),
                        pltpu.VMEM((ATTN_BLOCK + ATTN_TILE, KV_DIM), BF16),
                        pltpu.VMEM((N_META + ATTN_TILE, POOL_WIDTH), F32),
                        pltpu.VMEM((blocks_per_tile * N_KV_HEADS, GROUP * ATTN_BLOCK, 2 * ATTN_BLOCK), F32),
                        pltpu.VMEM((blocks_per_tile * N_KV_HEADS, GROUP * ATTN_BLOCK, 2 * ATTN_BLOCK), BF16)],
        compiler_params=_cparams(("parallel", "parallel")),
        name="prompt_mix",
    )(sinks, q, k, v, p, k, v, p, mk, mv, mp, bias)


def _sample_mix_kernel(sink_ref, q_ref, k_ref, v_ref, p_ref, ck_ref, cv_ref, st_ref,
                       biasc_ref, biasn_ref,
                       pooled_ref, attn_ref, newk_ref, newv_ref, newp_ref, ext, pooled_buf):
    nseq, w_cache, s_new = SAMPLE_SEQS, WINDOW, q_ref.shape[0] // SAMPLE_SEQS
    keep = POOL_STATE - s_new
    for i in range(nseq):
        new = slice(i * s_new, (i + 1) * s_new)
        ext[1:1 + POOL_STATE, :] = st_ref[i]
        ext[1 + POOL_STATE:1 + POOL_STATE + s_new, :] = p_ref[new, :]
        _window_pool(ext, 1 + POOL_STATE, s_new, pooled_buf)
        pooled_ref[new, :] = pooled_buf[...].astype(pooled_ref.dtype)
        newp_ref[i, 0:keep, :] = st_ref[i, s_new:POOL_STATE, :]
        newp_ref[i, keep:POOL_STATE, :] = p_ref[new, :]
        newk_ref[i, 0:w_cache - s_new, :] = ck_ref[i, s_new:w_cache, :]
        newk_ref[i, w_cache - s_new:w_cache, :] = k_ref[new, :]
        newv_ref[i, 0:w_cache - s_new, :] = cv_ref[i, s_new:w_cache, :]
        newv_ref[i, w_cache - s_new:w_cache, :] = v_ref[new, :]

    kc = ck_ref[...].reshape(nseq * w_cache, KV_DIM).astype(BF16)
    vc = cv_ref[...].reshape(nseq * w_cache, KV_DIM).astype(BF16)
    kn = k_ref[...].astype(BF16)
    vn = v_ref[...].astype(BF16)
    q = q_ref[...]
    nq = q.shape[0]
    for kh in range(N_KV_HEADS):
        kvc = slice(kh * HEAD_DIM, (kh + 1) * HEAD_DIM)
        qg = jnp.concatenate([q[:, (kh * GROUP + g) * HEAD_DIM:(kh * GROUP + g + 1) * HEAD_DIM]
                              for g in range(GROUP)], axis=0)
        sc = lax.dot_general(qg, kc[:, kvc], _NT, preferred_element_type=F32) + biasc_ref[kh]
        sn = lax.dot_general(qg, kn[:, kvc], _NT, preferred_element_type=F32) + biasn_ref[kh]
        sink = jnp.concatenate([jnp.full((nq, 1), sink_ref[kh * GROUP + g], F32) for g in range(GROUP)], axis=0)
        m = jnp.maximum(jnp.maximum(jnp.max(sc, axis=-1, keepdims=True),
                                    jnp.max(sn, axis=-1, keepdims=True)), sink)
        ec = jnp.exp(sc - m)
        en = jnp.exp(sn - m)
        denom = (jnp.sum(ec, axis=-1, keepdims=True) + jnp.sum(en, axis=-1, keepdims=True)
                 + jnp.exp(sink - m))
        inv = 1.0 / denom
        o = (jnp.dot((ec * inv).astype(BF16), vc[:, kvc], preferred_element_type=F32)
             + jnp.dot((en * inv).astype(BF16), vn[:, kvc], preferred_element_type=F32))
        for g in range(GROUP):
            h = kh * GROUP + g
            attn_ref[:, h * HEAD_DIM:(h + 1) * HEAD_DIM] = o[g * nq:(g + 1) * nq].astype(attn_ref.dtype)


def _sample_mix(sinks, q, k, v, p, cache_k, cache_v, state, biasc, biasn, s_new):
    nb = cache_k.shape[0]
    rows = SAMPLE_SEQS * s_new
    tok = lambda c: pl.BlockSpec((rows, c), lambda i: (i, 0))
    seq3 = lambda r, c: pl.BlockSpec((SAMPLE_SEQS, r, c), lambda i: (i, 0, 0))
    const3 = lambda a: pl.BlockSpec(a.shape, lambda i: (0, 0, 0))
    return pl.pallas_call(
        _sample_mix_kernel,
        grid=(nb // SAMPLE_SEQS,),
        in_specs=[pl.BlockSpec(memory_space=pltpu.SMEM),
                  tok(Q_DIM), tok(KV_DIM), tok(KV_DIM), tok(POOL_WIDTH),
                  seq3(WINDOW, KV_DIM), seq3(WINDOW, KV_DIM), seq3(POOL_STATE, POOL_WIDTH),
                  const3(biasc), const3(biasn)],
        out_specs=[tok(POOL_WIDTH), tok(Q_DIM),
                   seq3(WINDOW, KV_DIM), seq3(WINDOW, KV_DIM), seq3(POOL_STATE, POOL_WIDTH)],
        out_shape=[jax.ShapeDtypeStruct((nb * s_new, POOL_WIDTH), F32),
                   jax.ShapeDtypeStruct((nb * s_new, Q_DIM), F32),
                   jax.ShapeDtypeStruct((nb, WINDOW, KV_DIM), F32),
                   jax.ShapeDtypeStruct((nb, WINDOW, KV_DIM), F32),
                   jax.ShapeDtypeStruct((nb, POOL_STATE, POOL_WIDTH), F32)],
        scratch_shapes=[pltpu.VMEM((1 + POOL_STATE + SUBLANES, POOL_WIDTH), F32),
                        pltpu.VMEM((s_new, POOL_WIDTH), F32)],
        compiler_params=_cparams(("parallel",)),
        name="sample_mix",
    )(sinks, q, k, v, p, cache_k, cache_v, state, biasc, biasn)


def _layer_norm(x, g, b):
    mu = jnp.mean(x, axis=-1, keepdims=True)
    xc = x - mu
    var = jnp.mean(xc * xc, axis=-1, keepdims=True)
    return xc * lax.rsqrt(var + LN_EPS) * g + b


def _merge_kernel(x_ref, pooled_ref, attn_ref, wg_ref, wmix_ref, scale_ref, wup_ref, wua_ref,
                  wout_ref, g_ref, b_ref, y_ref):
    x = x_ref[...]
    glog = jnp.dot(x.astype(BF16), wg_ref[...], preferred_element_type=F32)
    g_pool = jax.nn.sigmoid(glog[:, :D_MODEL])
    g_attn = jax.nn.sigmoid(glog[:, D_MODEL:])
    pooled = pooled_ref[...].astype(BF16)
    mixed = jnp.concatenate(
        [jnp.dot(pooled[:, g * POOL_GROUP_DIM:(g + 1) * POOL_GROUP_DIM], wmix_ref[g],
                 preferred_element_type=F32) for g in range(len(POOL_WINDOWS))], axis=1)
    pool_out = (mixed * scale_ref[...]).astype(BF16)
    a = jnp.dot(pool_out, wup_ref[...], preferred_element_type=F32)
    b = jnp.dot(attn_ref[...].astype(BF16), wua_ref[...], preferred_element_type=F32)
    m = (g_pool * a + g_attn * b).astype(BF16)
    r = jnp.dot(m, wout_ref[...], preferred_element_type=F32)
    y_ref[...] = _layer_norm(ALPHA * x + r, g_ref[...], b_ref[...])


def _merge(x, pooled, attn, wg, wmix, scale, wup, wua, wout, ln_g, ln_b):
    rows = x.shape[0]
    tm = min(TOK_TILE, rows)
    assert rows % tm == 0
    row = lambda c: pl.BlockSpec((tm, c), lambda i: (i, 0))
    full = lambda a: pl.BlockSpec(a.shape, lambda i: (0,) * a.ndim)
    weights = (wg, wmix, scale, wup, wua, wout, ln_g, ln_b)
    return pl.pallas_call(
        _merge_kernel,
        grid=(rows // tm,),
        in_specs=[row(D_MODEL), row(POOL_WIDTH), row(Q_DIM)] + [full(a) for a in weights],
        out_specs=row(D_MODEL),
        out_shape=jax.ShapeDtypeStruct((rows, D_MODEL), F32),
        compiler_params=_cparams(("parallel",)),
        name="merge",
    )(x, pooled, attn, *weights)


def _oddeven_merge_sort_pairs(n):
    pairs = []
    p = 1
    while p < n:
        k = p
        while k >= 1:
            for j in range(k % p, n - k, 2 * k):
                for i in range(min(k, n - j - k)):
                    if (i + j) // (2 * p) == (i + j + k) // (2 * p):
                        pairs.append((i + j, i + j + k))
            k //= 2
        p *= 2
    return pairs


def _bitonic_merge_pairs(n):
    pairs = []
    k = n // 2
    while k >= 1:
        pairs += [(i, i + k) for i in range(n) if not i & k]
        k //= 2
    return pairs


def _apply_network(vals, pairs):
    vals = list(vals)
    for i, j in pairs:
        a, b = vals[i], vals[j]
        if b is None:
            continue
        if a is None:
            vals[i], vals[j] = b, None
        else:
            vals[i], vals[j] = jnp.maximum(a, b), jnp.minimum(a, b)
    return vals


_SORT16 = _oddeven_merge_sort_pairs(PEER_TOPK)
_MERGE16 = _bitonic_merge_pairs(PEER_TOPK)


def _top16_rows(sc):
    n = sc.shape[0] // SUBLANES
    assert n == PEER_TOPK
    x = _apply_network([sc[k * SUBLANES:(k + 1) * SUBLANES, :] for k in range(n)], _SORT16)
    for shift in (4, 2, 1):
        y = [jnp.maximum(x[k], pltpu.roll(x[n - 1 - k], shift, 0)) for k in range(n)]
        x = _apply_network(y, _MERGE16)
    return x


def _best_sums(v1, v2):
    row = lax.broadcasted_iota(jnp.int32, v1[0].shape, 0)

    def one_per_sublane(vals):
        x = vals[SUBLANES - 1]
        for s in range(SUBLANES - 2, -1, -1):
            x = jnp.where(row == s, vals[s], x)
        return x

    lo, hi = one_per_sublane(v1[:SUBLANES]), one_per_sublane(v1[SUBLANES:])
    sums_lo = []
    for b in range(PEER_TOPK):
        n_valid = sum((a + 1) * (b + 1) <= PEER_TOPK for a in range(SUBLANES))
        s = lo + v2[b]
        sums_lo.append(s if n_valid == SUBLANES else jnp.where(row < n_valid, s, NEG))
    sums_hi = hi + v2[0]
    x, best = sums_hi, []
    for k in range(PEER_TOPK):
        best.append(jnp.maximum(sums_lo[k], x))
        x = jnp.minimum(sums_lo[k], x)
    for shift in (4, 2, 1):
        y = [jnp.maximum(best[k], pltpu.roll(best[PEER_TOPK - 1 - k], shift, 0)) for k in range(PEER_TOPK)]
        best = _apply_network(y, _MERGE16)
    return best, sums_lo, sums_hi


def _peer_scores_kernel(x_ref, wq_ref, keys_ref, cnt_ref, e1_ref, sel_ref, sc_buf):
    q = jnp.dot(x_ref[...].astype(BF16), wq_ref[...], preferred_element_type=F32).astype(BF16)
    ncol = q.shape[0] // LANES
    for hc in range(2 * PEER_HEADS):
        sc = lax.dot_general(keys_ref[hc], q[:, hc * PEER_HALF:(hc + 1) * PEER_HALF], _NT,
                             preferred_element_type=F32)
        for t in range(ncol):
            sc_buf[hc, t] = sc[:, t * LANES:(t + 1) * LANES]

    def select(h, carry):
        for t in range(ncol):
            s1, s2 = sc_buf[2 * h, t], sc_buf[2 * h + 1, t]
            v1, v2 = _top16_rows(s1), _top16_rows(s2)
            best, sums_lo, _ = _best_sums(v1, v2)
            tau = best[PEER_TOPK - 1]
            z = jnp.ones_like(tau)
            for r in range(1, PEER_TOPK):
                z = z + jnp.exp(best[r] - best[0])
            inv_z = (0.5 / GELU_ARG) / z
            cnt_lo = jnp.zeros_like(tau)
            for s in sums_lo:
                cnt_lo = cnt_lo + jnp.where(s >= tau, 1.0, 0.0)
            row = lambda r: r[0:1, :]
            half = float(SUBLANES)
            cnt_lo = jnp.where(cnt_lo > half, half + 1.0, cnt_lo)
            cnt = jnp.where((s1 <= row(v1[SUBLANES])) & (s1 + row(v2[0]) >= row(tau)), 1.0, 0.0)
            rank = jnp.where((s2 <= row(v2[SUBLANES])) & (row(v1[0]) + s2 >= row(tau)), half, half + 1.0)
            for a in range(SUBLANES - 1, -1, -1):
                cnt = jnp.where(s1 == row(v1[a]), cnt_lo[a:a + 1, :], cnt)
                rank = jnp.where(s2 >= row(v2[a]), float(a), rank)
            cnt_ref[h, t] = cnt
            e1_ref[h, t] = jnp.exp(s1 - row(v1[0])) * row(inv_z)
            rank_words = pltpu.bitcast(rank.astype(BF16), jnp.uint32)
            e2_words = pltpu.bitcast(jnp.exp(s2 - row(v2[0])).astype(BF16), jnp.uint32)
            for k in range(N_KEYS // (2 * SUBLANES)):
                words = slice(k * SUBLANES, (k + 1) * SUBLANES)
                sel_ref[h, t, 2 * k * SUBLANES:(2 * k + 1) * SUBLANES, :] = rank_words[words]
                sel_ref[h, t, (2 * k + 1) * SUBLANES:(2 * k + 2) * SUBLANES, :] = e2_words[words]

        return carry

    lax.fori_loop(0, PEER_HEADS, select, 0)


def _peer_scores(x, wq, keys):
    rows = x.shape[0]
    tm = min(TOK_TILE, rows)
    assert rows % tm == 0
    return pl.pallas_call(
        _peer_scores_kernel,
        grid=(rows // tm,),
        in_specs=[pl.BlockSpec((tm, D_MODEL), lambda i: (i, 0)),
                  pl.BlockSpec(wq.shape, lambda i: (0, 0)),
                  pl.BlockSpec(keys.shape, lambda i: (0, 0, 0))],
        out_specs=[pl.BlockSpec((PEER_HEADS, tm // LANES, N_KEYS, LANES), lambda i: (0, i, 0, 0))] * 3,
        out_shape=[jax.ShapeDtypeStruct((PEER_HEADS, rows // LANES, N_KEYS, LANES), dt)
                   for dt in (F32, F32, jnp.uint32)],
        scratch_shapes=[pltpu.VMEM((2 * PEER_HEADS, tm // LANES, N_KEYS, LANES), F32)],
        compiler_params=_cparams(("parallel",)),
        name="peer_scores",
    )(x, wq, keys)


GELU_ARG = math.sqrt(0.5)


def _gelu_scaled(s):
    return s * (1.0 + lax.erf(s))


def _peer_dense_kernel(x_ref, cnt_ref, e1_ref, sel_ref, u_ref, v_ref, g_ref, b_ref, y_ref,
                       xt, rowb, hid, wt, acc):
    c = pl.program_id(1)
    rows_per_chunk = EXPERT_CHUNK // N_KEYS
    ncol = xt.shape[1] // LANES
    packed = 2 * SUBLANES
    assert rows_per_chunk == SUBLANES

    @pl.when(c == 0)
    def _():
        xt[...] = x_ref[...].T.astype(BF16)
        acc[...] = jnp.zeros_like(acc)

    def build_rows(chunk, heads):
        for h in heads:
            for col in range(ncol):
                for r in range(SUBLANES):
                    row = pl.ds(chunk * SUBLANES + r, packed, stride=0)
                    rowb[r, h, col] = cnt_ref[h, col, row, :].astype(BF16)
                    rowb[r, PEER_HEADS + h, col] = e1_ref[h, col, row, :].astype(BF16)

    def mask_row(il, carry):
        base = pl.multiple_of(il * N_KEYS, N_KEYS)
        for col in range(ncol):
            gate = [None] * (N_KEYS // packed)
            for h in range(PEER_HEADS):
                cnt = rowb[il, h, col]
                g1 = rowb[il, PEER_HEADS + h, col]
                for k in range(N_KEYS // packed):
                    rank = pltpu.bitcast(sel_ref[h, col, 2 * k * SUBLANES:(2 * k + 1) * SUBLANES, :], BF16)
                    g2 = pltpu.bitcast(sel_ref[h, col, (2 * k + 1) * SUBLANES:(2 * k + 2) * SUBLANES, :], BF16)
                    g = jnp.where(rank < cnt, g2 * g1, jnp.zeros((), BF16))
                    gate[k] = g if gate[k] is None else gate[k] + g
            for k in range(N_KEYS // packed):
                act = _gelu_scaled(hid[col, pl.ds(base + k * packed, packed), :]).astype(BF16)
                wt[pl.ds(base + k * packed, packed), col * LANES:(col + 1) * LANES] = act * gate[k]
        return carry

    n_slabs = 4
    slab = EXPERT_CHUNK // n_slabs

    for i in range(n_slabs):
        rows = slice(i * slab, (i + 1) * slab)
        res = jnp.dot(u_ref[rows, :].astype(BF16), xt[...], preferred_element_type=F32)
        for col in range(ncol):
            hid[col, rows, :] = res[:, col * LANES:(col + 1) * LANES] * GELU_ARG
        build_rows(c, range(i * PEER_HEADS // n_slabs, (i + 1) * PEER_HEADS // n_slabs))
    lax.fori_loop(0, rows_per_chunk, mask_row, 0)
    acc[...] += lax.dot_general(v_ref[...].astype(BF16), wt[...], (((0,), (0,)), ((), ())),
                                preferred_element_type=F32)

    @pl.when(c == pl.num_programs(1) - 1)
    def _():
        y_ref[...] = _layer_norm(ALPHA * x_ref[...] + acc[...].T, g_ref[...], b_ref[...])


def _peer_dense(x, cnt, e1, sel, u, vt, ln_g, ln_b):
    rows = x.shape[0]
    tm = min(TOK_TILE, rows)
    assert rows % tm == 0
    n_exp = u.shape[0]
    assert n_exp % EXPERT_CHUNK == 0
    return pl.pallas_call(
        _peer_dense_kernel,
        grid=(rows // tm, n_exp // EXPERT_CHUNK),
        in_specs=[pl.BlockSpec((tm, D_MODEL), lambda t, c: (t, 0))]
                 + [pl.BlockSpec((PEER_HEADS, tm // LANES, N_KEYS, LANES), lambda t, c: (0, t, 0, 0))] * 3
                 + [pl.BlockSpec((EXPERT_CHUNK, D_MODEL), lambda t, c: (c, 0)),
                    pl.BlockSpec((EXPERT_CHUNK, D_MODEL), lambda t, c: (c, 0)),
                    pl.BlockSpec((1, D_MODEL), lambda t, c: (0, 0)),
                    pl.BlockSpec((1, D_MODEL), lambda t, c: (0, 0))],
        out_specs=pl.BlockSpec((tm, D_MODEL), lambda t, c: (t, 0)),
        out_shape=jax.ShapeDtypeStruct((rows, D_MODEL), F32),
        scratch_shapes=[pltpu.VMEM((D_MODEL, tm), BF16),
                        pltpu.VMEM((SUBLANES, 2 * PEER_HEADS, tm // LANES, 2 * SUBLANES, LANES), BF16),
                        pltpu.VMEM((tm // LANES, EXPERT_CHUNK, LANES), F32),
                        pltpu.VMEM((EXPERT_CHUNK, tm), BF16),
                        pltpu.VMEM((D_MODEL, tm), F32)],
        compiler_params=_cparams(("parallel", "arbitrary")),
        name="peer_dense",
    )(x, cnt, e1, sel, u, vt, ln_g, ln_b)


def _rel_bucket_np(dist):
    n = np.maximum(dist, 0)
    max_exact = NUM_BUCKETS // 2
    large = max_exact + (np.log(np.maximum(n, max_exact).astype(np.float32) / max_exact)
                         / math.log(MAX_DISTANCE / max_exact) * (NUM_BUCKETS - max_exact)).astype(np.int32)
    return np.where(n < max_exact, n, np.minimum(large, NUM_BUCKETS - 1))


def _bias_kernel(table_ref, bucket_ref, out_ref, *, head_stride):
    head = pl.program_id(0) * head_stride + pl.program_id(1)
    bucket = bucket_ref[0]
    acc = jnp.full(bucket.shape, NEG, F32)
    for b in range(NUM_BUCKETS):
        acc = jnp.where(bucket == b, table_ref[b, head], acc)
    out_ref[0, 0] = acc


def _bias_tiles(table, bucket, n_outer, n_inner, head_stride):
    r, c = bucket.shape[1:]
    bmap = (lambda a, b: (a, 0, 0)) if bucket.shape[0] > 1 else (lambda a, b: (0, 0, 0))
    return pl.pallas_call(
        functools.partial(_bias_kernel, head_stride=head_stride),
        grid=(n_outer, n_inner),
        in_specs=[pl.BlockSpec(memory_space=pltpu.SMEM), pl.BlockSpec((1, r, c), bmap)],
        out_specs=pl.BlockSpec((1, 1, r, c), lambda a, b: (a, b, 0, 0)),
        out_shape=jax.ShapeDtypeStruct((n_outer, n_inner, r, c), F32),
        compiler_params=_cparams(("parallel", "parallel")),
        name="bias_tiles",
    )(table, jnp.asarray(bucket, jnp.int32))


def _prompt_bias(table):
    qi = np.arange(ATTN_BLOCK)[:, None]
    kj = np.arange(2 * ATTN_BLOCK)[None, :]
    dist = ATTN_BLOCK + qi - kj
    ok = (dist >= 0) & (dist < WINDOW)
    ok = np.stack([ok, ok & (kj >= ATTN_BLOCK - N_META)])
    bucket = np.where(ok, _rel_bucket_np(dist)[None], -1)
    tiles = _bias_tiles(table, bucket, 2, N_HEADS, 0)
    return tiles.reshape(2, N_KV_HEADS, GROUP * ATTN_BLOCK, 2 * ATTN_BLOCK)


def _sample_bias(table, s_new):
    s, t = np.meshgrid(np.arange(SAMPLE_SEQS), np.arange(s_new), indexing="ij")
    s, t = s.reshape(-1, 1), t.reshape(-1, 1)
    out = []
    for cols, offset in ((WINDOW, WINDOW), (s_new, 0)):
        s2, c = np.meshgrid(np.arange(SAMPLE_SEQS), np.arange(cols), indexing="ij")
        s2, c = s2.reshape(1, -1), c.reshape(1, -1)
        dist = offset + t - c + 0 * s2
        ok = (s == s2) & (dist >= 0) & (dist < WINDOW)
        bucket = np.where(ok, _rel_bucket_np(dist), -1)[None]
        tiles = _bias_tiles(table, bucket, N_KV_HEADS, GROUP, GROUP)
        out.append(tiles.reshape(N_KV_HEADS, GROUP * bucket.shape[1], bucket.shape[2]))
    return out


def kernel(x_prompt, x_sample, cache_k, cache_v, state_pool, meta_tokens, rel_bias_table, w_in,
           w_pool_mix, pool_scale, attn_sinks, w_up_pool, w_up_attn, w_out, ln1_g, ln1_b,
           peer_w_query, peer_sub_keys, peer_u, peer_v, ln2_g, ln2_b):
    batch, seq, d = x_prompt.shape
    nb, s_new, _ = x_sample.shape
    assert w_in.shape[0] == DEPTH and d == D_MODEL and seq % ATTN_TILE == 0
    assert N_META >= max(POOL_WINDOWS) - 1 and cache_k.shape[2] == WINDOW

    w_in0 = w_in[0].astype(BF16)
    w_pqkv, w_gates = w_in0[:, :OFF_GA], w_in0[:, OFF_GA:]
    wmix = w_pool_mix[0].astype(BF16)
    scale = pool_scale[0].reshape(1, POOL_WIDTH)
    wup, wua, wout = w_up_pool[0].astype(BF16), w_up_attn[0].astype(BF16), w_out[0].astype(BF16)
    g1, b1 = ln1_g[0].reshape(1, d), ln1_b[0].reshape(1, d)
    g2, b2 = ln2_g[0].reshape(1, d), ln2_b[0].reshape(1, d)
    wq = peer_w_query[0].astype(BF16)
    keys = peer_sub_keys[0].reshape(2 * PEER_HEADS, N_KEYS, PEER_HALF).astype(BF16)
    u = peer_u[0]
    vt = peer_v[0]
    sinks = attn_sinks[0].astype(F32)
    table = rel_bias_table.astype(F32)
    bias_p = _prompt_bias(table)
    bias_c, bias_n = _sample_bias(table, s_new)

    xp = x_prompt.reshape(batch * seq, d)
    xs = x_sample.reshape(nb * s_new, d)
    xm = jnp.concatenate([jnp.zeros((ATTN_BLOCK - N_META, d), F32), meta_tokens.astype(F32)], axis=0)

    p_p, q_p, k_p, v_p = _inproj(xp, w_pqkv)
    p_s, q_s, k_s, v_s = _inproj(xs, w_pqkv)
    p_m, _, k_m, v_m = _inproj(xm, w_pqkv)

    pooled_p, attn_p = _prompt_mix(sinks, q_p, k_p, v_p, p_p, k_m, v_m, p_m[ATTN_BLOCK - N_META:],
                                   bias_p, batch, seq)
    ck = cache_k[0].reshape(nb, WINDOW, KV_DIM)
    cv = cache_v[0].reshape(nb, WINDOW, KV_DIM)
    pooled_s, attn_s, newk, newv, newp = _sample_mix(sinks, q_s, k_s, v_s, p_s, ck, cv,
                                                     state_pool[0], bias_c, bias_n, s_new)

    outs = []
    for x, pooled, attn in ((xp, pooled_p, attn_p), (xs, pooled_s, attn_s)):
        x1 = _merge(x, pooled, attn, w_gates, wmix, scale, wup, wua, wout, g1, b1)
        cnt, e1, sel = _peer_scores(x1, wq, keys)
        outs.append(_peer_dense(x1, cnt, e1, sel, u, vt, g2, b2))

    w_keep = min(WINDOW, seq + N_META)
    kv_shape = (batch, seq, N_KV_HEADS, HEAD_DIM)
    return (outs[0].reshape(batch, seq, d),
            outs[1].reshape(nb, s_new, d),
            k_p.reshape(kv_shape)[None, :, seq - w_keep:],
            v_p.reshape(kv_shape)[None, :, seq - w_keep:],
            p_p.reshape(batch, seq, POOL_WIDTH)[None, :, seq - POOL_STATE:],
            newk.reshape(1, nb, WINDOW, N_KV_HEADS, HEAD_DIM),
            newv.reshape(1, nb, WINDOW, N_KV_HEADS, HEAD_DIM),
            newp[None])
```
